```python
import jax, jax.numpy as jnp
from jax import lax
import numpy as np

D_MODEL = 2048
BATCH = 2
SEQ = 16384
DEPTH = 2

N_META = 16
BLOCK_Q = 128
D_MIX = D_MODEL
D_ATTN = D_MIX // 2
D_CONV = D_MIX - D_ATTN
ATTN_HEAD_DIM = 128
N_ATTN_HEADS = D_ATTN // ATTN_HEAD_DIM
CONV_GROUP_DIM = 128
N_CONV_GROUPS = D_CONV // CONV_GROUP_DIM
GROUP_DIM = 128
N_GROUPS = D_MIX // GROUP_DIM
CONV_WIDTH = 3
D_FF = -(-8 * D_MODEL // (3 * 256)) * 256
D_IN = 3 * D_ATTN + 3 * D_CONV + N_ATTN_HEADS
EPS = 1e-6
NEG = -1e30

kernel_name = "hymba_fox_shortconv_hybrid"


def rmsnorm(x, g):
    xf = x.astype(jnp.float32)
    y = xf * lax.rsqrt(jnp.mean(xf * xf, axis=-1, keepdims=True) + EPS)
    return (y * g.astype(jnp.float32)).astype(x.dtype)


def short_conv(u, w):
    L = u.shape[1]
    up = jnp.pad(u, ((0, 0), (CONV_WIDTH - 1, 0), (0, 0)))
    y = w[0] * up[:, 0:L]
    for k in range(1, CONV_WIDTH):
        y = y + w[k] * up[:, k:k + L]
    return y


def forgetting_attention(q, k, v, log_f):
    Bsz, L, H, Dh = q.shape
    n_pad = (-L) % BLOCK_Q
    Lp = L + n_pad
    nb = Lp // BLOCK_Q
    pad4 = ((0, 0), (n_pad, 0), (0, 0), (0, 0))
    qp = jnp.pad(q, pad4).transpose(0, 2, 1, 3)
    kp = jnp.pad(k, pad4).transpose(0, 2, 1, 3)
    vp = jnp.pad(v, pad4).transpose(0, 2, 1, 3)
    lf = jnp.pad(log_f.astype(jnp.float32), ((0, 0), (n_pad, 0), (0, 0)))
    c = jnp.cumsum(lf, axis=1).transpose(0, 2, 1)
    key_pos = jnp.arange(Lp)
    key_valid = key_pos >= n_pad
    scale = ATTN_HEAD_DIM ** -0.5
    qb = qp.reshape(Bsz, H, nb, BLOCK_Q, Dh).transpose(2, 0, 1, 3, 4)
    cqb = c.reshape(Bsz, H, nb, BLOCK_Q).transpose(2, 0, 1, 3)

    def block(args):
        qi, cqi, i = args
        s = jnp.einsum('bhqd,bhkd->bhqk', qi, kp,
                       preferred_element_type=jnp.float32) * scale
        s = s + cqi[..., :, None] - c[:, :, None, :]
        qpos = i * BLOCK_Q + jnp.arange(BLOCK_Q)
        mask = (key_pos[None, :] <= qpos[:, None]) & key_valid[None, :]
        s = jnp.where(mask, s, NEG)
        p = jax.nn.softmax(s, axis=-1)
        return jnp.einsum('bhqk,bhkd->bhqd', p.astype(vp.dtype), vp)

    out = lax.map(block, (qb, cqb, jnp.arange(nb)))
    out = out.transpose(1, 0, 3, 2, 4).reshape(Bsz, Lp, H, Dh)
    return out[:, n_pad:]


def hybrid_mixer(h, w_in, b_f, conv_w, out_gain, w_out):
    Bsz, L, _ = h.shape
    z = h @ w_in
    o = 0
    q = z[..., o:o + D_ATTN]; o += D_ATTN
    k = z[..., o:o + D_ATTN]; o += D_ATTN
    v = z[..., o:o + D_ATTN]; o += D_ATTN
    gate_b = z[..., o:o + D_CONV]; o += D_CONV
    gate_c = z[..., o:o + D_CONV]; o += D_CONV
    hc = z[..., o:o + D_CONV]; o += D_CONV
    f_logit = z[..., o:o + N_ATTN_HEADS]
    shp = (Bsz, L, N_ATTN_HEADS, ATTN_HEAD_DIM)
    log_f = jax.nn.log_sigmoid((f_logit + b_f).astype(jnp.float32))
    attn = forgetting_attention(q.reshape(shp), k.reshape(shp), v.reshape(shp), log_f)
    attn = attn.reshape(Bsz, L, D_ATTN)
    conv = gate_b * short_conv(gate_c * hc, conv_w)
    y = jnp.concatenate([attn, conv], axis=-1).reshape(Bsz, L, N_GROUPS, GROUP_DIM)
    y = rmsnorm(y, out_gain.reshape(N_GROUPS, GROUP_DIM)).reshape(Bsz, L, D_MIX)
    return y @ w_out


def swiglu(h, w_gate, w_up, w_down):
    return (jax.nn.silu(h @ w_gate) * (h @ w_up)) @ w_down


def setup_inputs(seed: int = 0) -> dict:
    key = jax.random.key(seed)
    ks = jax.random.split(key, 13)
    f32 = jnp.float32
    nrm = lambda k, s, sc: jax.random.normal(k, s, f32) * sc
    x = nrm(ks[0], (BATCH, SEQ, D_MODEL), 1.0)
    meta = nrm(ks[1], (N_META, D_MODEL), 1.0)
    norm_mix = 1.0 + nrm(ks[2], (DEPTH, D_MODEL), 0.02)
    w_in = nrm(ks[3], (DEPTH, D_MODEL, D_IN), D_MODEL ** -0.5)
    b_f = (jnp.linspace(1.0, 6.0, N_ATTN_HEADS, dtype=f32)[None, :]
           + nrm(ks[4], (DEPTH, N_ATTN_HEADS), 0.1))
    conv_w = nrm(ks[5], (DEPTH, CONV_WIDTH, D_CONV), CONV_WIDTH ** -0.5)
    out_gain = 1.0 + nrm(ks[6], (DEPTH, D_MIX), 0.02)
    w_out = nrm(ks[7], (DEPTH, D_MIX, D_MODEL), D_MIX ** -0.5)
    norm_ffn = 1.0 + nrm(ks[8], (DEPTH, D_MODEL), 0.02)
    w_gate = nrm(ks[9], (DEPTH, D_MODEL, D_FF), D_MODEL ** -0.5)
    w_up = nrm(ks[10], (DEPTH, D_MODEL, D_FF), D_MODEL ** -0.5)
    w_down = nrm(ks[11], (DEPTH, D_FF, D_MODEL), D_FF ** -0.5)
    final_norm = 1.0 + nrm(ks[12], (D_MODEL,), 0.02)
    return {"x": x, "meta": meta, "norm_mix": norm_mix, "w_in": w_in, "b_f": b_f,
            "conv_w": conv_w, "out_gain": out_gain, "w_out": w_out,
            "norm_ffn": norm_ffn, "w_gate": w_gate, "w_up": w_up,
            "w_down": w_down, "final_norm": final_norm}


def reference(x, meta, norm_mix, w_in, b_f, conv_w, out_gain, w_out,
              norm_ffn, w_gate, w_up, w_down, final_norm):
    Bsz = x.shape[0]
    m = jnp.broadcast_to(meta.astype(x.dtype)[None], (Bsz, N_META, D_MODEL))
    h = jnp.concatenate([m, x], axis=1)
    for l in range(DEPTH):
        h = h + hybrid_mixer(rmsnorm(h, norm_mix[l]), w_in[l], b_f[l], conv_w[l],
                             out_gain[l], w_out[l])
        h = h + swiglu(rmsnorm(h, norm_ffn[l]), w_gate[l], w_up[l], w_down[l])
    h = rmsnorm(h, final_norm)
    return h[:, N_META:]
```

```python
import functools

import jax
import jax.numpy as jnp
from jax import lax
from jax.experimental import pallas as pl
from jax.experimental.pallas import tpu as pltpu

F32 = jnp.float32
BF16 = jnp.bfloat16

HEAD_DIM = 128
CONV_WIDTH = 3
EPS = 1e-6
NEG = -1e30
SEQ_TILE = 512
HEAD_PAD = 16
HALO = 8
VMEM_LIMIT = 56 * 1024 * 1024


def _cparams(n_axes):
    return pltpu.CompilerParams(dimension_semantics=("arbitrary",) * n_axes,
                                vmem_limit_bytes=VMEM_LIMIT)


def _rms_scale(x):
    return lax.rsqrt(jnp.mean(x * x, axis=-1, keepdims=True) + EPS)


def _cumsum_lanes(x):
    n = x.shape[1]
    lane = lax.broadcasted_iota(jnp.int32, x.shape, 1)
    shift = 1
    while shift < n:
        x = x + jnp.where(lane >= shift, pltpu.roll(x, shift, axis=1), 0.0)
        shift *= 2
    return x


def _in_proj_kernel(h_ref, g_ref, w_ref, wf_ref, bf_ref, cw_ref, og_ref,
                    q_ref, k_ref, v_ref, yc_ref, ct_ref,
                    hn_ref, gb_ref, u_ref, carry_ref, *, tm, n_groups, scale):
    i = pl.program_id(1)
    j = pl.program_id(2)

    @pl.when(j == 0)
    def _norm_and_gates():
        x = h_ref[0]
        hn = (x * _rms_scale(x) * g_ref[...]).astype(BF16)
        hn_ref[...] = hn
        logit = lax.dot_general(wf_ref[...], hn, (((1,), (1,)), ((), ())),
                                preferred_element_type=F32)
        log_f = jax.nn.log_sigmoid(logit + bf_ref[...])

        @pl.when(i == 0)
        def _reset():
            carry_ref[...] = jnp.zeros_like(carry_ref)
            u_ref[pl.ds(0, HALO), :] = jnp.zeros((HALO, u_ref.shape[1]), F32)

        c = _cumsum_lanes(log_f) + carry_ref[...]
        ct_ref[0] = c
        carry_ref[...] = jnp.broadcast_to(c[:, tm - 1:tm], c.shape)

    z = jnp.dot(hn_ref[...], w_ref[...], preferred_element_type=F32)

    @pl.when(j == 0)
    def _q():
        q_ref[0] = (z * scale).astype(BF16)

    @pl.when(j == 1)
    def _k():
        k_ref[0] = z.astype(BF16)

    @pl.when(j == 2)
    def _v():
        v_ref[0] = z.astype(BF16)

    @pl.when(j == 3)
    def _gate_b():
        gb_ref[...] = z

    @pl.when(j == 4)
    def _gate_c():
        u_ref[pl.ds(HALO, tm), :] = z

    @pl.when(j == 5)
    def _conv():
        u = u_ref[pl.ds(HALO, tm), :] * z
        u_ref[pl.ds(HALO, tm), :] = u
        u1 = u_ref[pl.ds(HALO - 1, tm), :]
        u2 = u_ref[pl.ds(HALO - 2, tm), :]
        conv = cw_ref[0:1, :] * u2 + cw_ref[1:2, :] * u1 + cw_ref[2:3, :] * u
        y = gb_ref[...] * conv
        for g in range(n_groups):
            cols = slice(g * HEAD_DIM, (g + 1) * HEAD_DIM)
            yg = y[:, cols]
            yc_ref[0, :, cols] = (yg * _rms_scale(yg) * og_ref[:, cols]).astype(BF16)
        u_ref[pl.ds(0, HALO), :] = u_ref[pl.ds(tm, HALO), :]


def _in_proj(h, gain, w_main, wf_t, bf_col, conv_w, og_conv, *, scale):
    bsz, lp, d = h.shape
    tn = w_main.shape[1] // 6
    tm = SEQ_TILE
    kern = functools.partial(_in_proj_kernel, tm=tm, n_groups=tn // HEAD_DIM, scale=scale)
    row_blk = pl.BlockSpec((1, tm, tn), lambda b, i, j: (b, i, 0))
    return pl.pallas_call(
        kern,
        grid=(bsz, lp // tm, 6),
        in_specs=[
            pl.BlockSpec((1, tm, d), lambda b, i, j: (b, i, 0)),
            pl.BlockSpec((1, d), lambda b, i, j: (0, 0)),
            pl.BlockSpec((d, tn), lambda b, i, j: (0, j)),
            pl.BlockSpec((HEAD_PAD, d), lambda b, i, j: (0, 0)),
            pl.BlockSpec((HEAD_PAD, 1), lambda b, i, j: (0, 0)),
            pl.BlockSpec((HALO, tn), lambda b, i, j: (0, 0)),
            pl.BlockSpec((1, tn), lambda b, i, j: (0, 0)),
        ],
        out_specs=[row_blk, row_blk, row_blk, row_blk,
                   pl.BlockSpec((1, HEAD_PAD, tm), lambda b, i, j: (b, 0, i))],
        out_shape=[jax.ShapeDtypeStruct((bsz, lp, tn), BF16)] * 4
        + [jax.ShapeDtypeStruct((bsz, HEAD_PAD, lp), F32)],
        scratch_shapes=[
            pltpu.VMEM((tm, d), BF16),
            pltpu.VMEM((tm, tn), F32),
            pltpu.VMEM((tm + HALO, tn), F32),
            pltpu.VMEM((HEAD_PAD, tm), F32),
        ],
        compiler_params=_cparams(3),
        name="in_proj",
    )(h, gain, w_main, wf_t, bf_col, conv_w, og_conv)


def _flash_kernel(cref_ref, q_ref, k_ref, v_ref, ct_ref, g_ref, o_ref,
                  m_ref, l_ref, acc_ref, *, tq, n_q, n_pad):
    b = pl.program_id(0)
    h = pl.program_id(1)
    i = pl.program_id(2)
    q = q_ref[0]
    cref = cref_ref[(b * HEAD_PAD + h) * n_q + i]
    m_ref[...] = jnp.full_like(m_ref, NEG)
    l_ref[...] = jnp.zeros_like(l_ref)
    acc_ref[...] = jnp.zeros_like(acc_ref)

    def step(j, masked):
        start = pl.multiple_of(j * tq, tq)
        k = k_ref[0, pl.ds(start, tq), :]
        v = v_ref[0, pl.ds(start, tq), :]
        s = lax.dot_general(q, k, (((1,), (1,)), ((), ())), preferred_element_type=F32)
        c_keys = ct_ref[0, pl.ds(h, 1), pl.ds(start, tq)]
        key_pos = start + lax.broadcasted_iota(jnp.int32, (1, tq), 1)
        bias = jnp.where(key_pos >= n_pad, cref - c_keys, NEG)
        t = s + bias
        if masked:
            row = lax.broadcasted_iota(jnp.int32, (tq, tq), 0)
            col = lax.broadcasted_iota(jnp.int32, (tq, tq), 1)
            t = jnp.where(col <= row, t, NEG)
        m_old = m_ref[...]
        m_new = jnp.maximum(m_old, jnp.max(t, axis=1, keepdims=True))
        alpha = jnp.exp(m_old - m_new)
        p = jnp.exp(t - m_new)
        l_ref[...] = alpha * l_ref[...] + jnp.sum(p, axis=1, keepdims=True)
        acc_ref[...] = alpha * acc_ref[...] + jnp.dot(p.astype(BF16), v,
                                                      preferred_element_type=F32)
        m_ref[...] = m_new

    def body(j, carry):
        step(j, False)
        return carry

    lax.fori_loop(0, i, body, 0)
    step(i, True)
    o = acc_ref[...] / l_ref[...]
    o_ref[0] = (o * _rms_scale(o) * g_ref[0]).astype(BF16)


def _flash(q, k, v, ct, cref, og_heads, *, n_heads, n_pad):
    bsz, lp, _ = q.shape
    tq = SEQ_TILE
    n_q = lp // tq
    kern = functools.partial(_flash_kernel, tq=tq, n_q=n_q, n_pad=n_pad)
    kv_blk = pl.BlockSpec((1, lp, HEAD_DIM), lambda b, h, i: (b, 0, h))
    q_blk = pl.BlockSpec((1, tq, HEAD_DIM), lambda b, h, i: (b, i, h))
    return pl.pallas_call(
        kern,
        grid=(bsz, n_heads, n_q),
        in_specs=[
            pl.BlockSpec(memory_space=pltpu.SMEM),
            q_blk, kv_blk, kv_blk,
            pl.BlockSpec((1, HEAD_PAD, lp), lambda b, h, i: (b, 0, 0)),
            pl.BlockSpec((1, 1, HEAD_DIM), lambda b, h, i: (h, 0, 0)),
        ],
        out_specs=q_blk,
        out_shape=jax.ShapeDtypeStruct(q.shape, BF16),
        scratch_shapes=[
            pltpu.VMEM((tq, 1), F32),
            pltpu.VMEM((tq, 1), F32),
            pltpu.VMEM((tq, HEAD_DIM), F32),
        ],
        compiler_params=_cparams(3),
        name="flash",
    )(cref, q, k, v, ct, og_heads)


def _out_proj_kernel(ya_ref, yc_ref, h_ref, wa_ref, wc_ref, o_ref, *, tm, lp, n_pad):
    r = pl.program_id(0)
    out = (jnp.dot(ya_ref[...], wa_ref[...], preferred_element_type=F32)
           + jnp.dot(yc_ref[...], wc_ref[...], preferred_element_type=F32))
    out = h_ref[...] + out
    pos = (r * tm + lax.broadcasted_iota(jnp.int32, (tm, 1), 0)) % lp
    o_ref[...] = jnp.where(pos >= n_pad, out, 0.0)


def _out_proj(ya, yc, h, w_a, w_c, *, lp, n_pad):
    rows, d = h.shape
    da = ya.shape[1]
    tm = 512
    kern = functools.partial(_out_proj_kernel, tm=tm, lp=lp, n_pad=n_pad)
    return pl.pallas_call(
        kern,
        grid=(rows // tm,),
        in_specs=[
            pl.BlockSpec((tm, da), lambda r: (r, 0)),
            pl.BlockSpec((tm, da), lambda r: (r, 0)),
            pl.BlockSpec((tm, d), lambda r: (r, 0)),
            pl.BlockSpec((da, d), lambda r: (0, 0)),
            pl.BlockSpec((da, d), lambda r: (0, 0)),
        ],
        out_specs=pl.BlockSpec((tm, d), lambda r: (r, 0)),
        out_shape=jax.ShapeDtypeStruct(h.shape, F32),
        compiler_params=_cparams(1),
        name="out_proj",
    )(ya, yc, h, w_a, w_c)


def _ffn_kernel(h_ref, g_ref, wg_ref, wu_ref, wd_ref, o_ref, hn_ref):
    f = pl.program_id(1)

    @pl.when(f == 0)
    def _norm():
        x = h_ref[...]
        hn_ref[...] = (x * _rms_scale(x) * g_ref[...]).astype(BF16)
        o_ref[...] = x

    hn = hn_ref[...]
    gate = jnp.dot(hn, wg_ref[...], preferred_element_type=F32)
    up = jnp.dot(hn, wu_ref[...], preferred_element_type=F32)
    act = (jax.nn.silu(gate) * up).astype(BF16)
    o_ref[...] += jnp.dot(act, wd_ref[...], preferred_element_type=F32)


def _ffn(h, gain, w_gate, w_up, w_down):
    rows, d = h.shape
    d_ff = w_gate.shape[1]
    tm = 1024 if rows % 1024 == 0 else 512
    tf = 512 if d_ff % 512 == 0 else 256
    return pl.pallas_call(
        _ffn_kernel,
        grid=(rows // tm, d_ff // tf),
        in_specs=[
            pl.BlockSpec((tm, d), lambda r, f: (r, 0)),
            pl.BlockSpec((1, d), lambda r, f: (0, 0)),
            pl.BlockSpec((d, tf), lambda r, f: (0, f)),
            pl.BlockSpec((d, tf), lambda r, f: (0, f)),
            pl.BlockSpec((tf, d), lambda r, f: (f, 0)),
        ],
        out_specs=pl.BlockSpec((tm, d), lambda r, f: (r, 0)),
        out_shape=jax.ShapeDtypeStruct(h.shape, F32),
        scratch_shapes=[pltpu.VMEM((tm, d), BF16)],
        compiler_params=_cparams(2),
        name="ffn",
    )(h, gain, w_gate, w_up, w_down)


def _final_norm_kernel(h_ref, g_ref, o_ref):
    x = h_ref[0]
    o_ref[0] = x * _rms_scale(x) * g_ref[...]


def _final_norm(h, gain, *, seq):
    bsz, lp, d = h.shape
    tm = SEQ_TILE
    skip = (lp - seq) // tm
    return pl.pallas_call(
        _final_norm_kernel,
        grid=(bsz, seq // tm),
        in_specs=[
            pl.BlockSpec((1, tm, d), lambda b, i: (b, i + skip, 0)),
            pl.BlockSpec((1, d), lambda b, i: (0, 0)),
        ],
        out_specs=pl.BlockSpec((1, tm, d), lambda b, i: (b, i, 0)),
        out_shape=jax.ShapeDtypeStruct((bsz, seq, d), F32),
        compiler_params=_cparams(2),
        name="final_norm",
    )(h, gain)


def kernel(x, meta, norm_mix, w_in, b_f, conv_w, out_gain, w_out, norm_ffn, w_gate, w_up, w_down, final_norm):
    bsz, seq, d = x.shape
    n_meta = meta.shape[0]
    depth = w_in.shape[0]
    d_attn = d // 2
    n_heads = d_attn // HEAD_DIM
    assert n_heads <= HEAD_PAD and seq % SEQ_TILE == 0 and n_meta <= SEQ_TILE
    assert w_in.shape[2] == 6 * d_attn + n_heads
    lp = seq + SEQ_TILE
    n_pad = lp - seq - n_meta
    n_q = lp // SEQ_TILE
    scale = HEAD_DIM ** -0.5

    m = jnp.broadcast_to(meta.astype(x.dtype)[None], (bsz, n_meta, d))
    h = jnp.concatenate([jnp.zeros((bsz, n_pad, d), x.dtype), m, x], axis=1)

    for l in range(depth):
        w_main = w_in[l, :, :6 * d_attn].astype(BF16)
        wf_t = jnp.zeros((HEAD_PAD, d), BF16).at[:n_heads].set(w_in[l, :, 6 * d_attn:].T.astype(BF16))
        bf_col = jnp.zeros((HEAD_PAD, 1), F32).at[:n_heads, 0].set(b_f[l])
        cw = jnp.zeros((HALO, d_attn), F32).at[:CONV_WIDTH].set(conv_w[l])
        og_heads = out_gain[l, :d_attn].reshape(n_heads, 1, HEAD_DIM)
        og_conv = out_gain[l, d_attn:].reshape(1, d_attn)

        q, k, v, yc, ct = _in_proj(h, norm_mix[l].reshape(1, d), w_main, wf_t, bf_col, cw, og_conv,
                                   scale=scale)
        cref = ct[:, :, ::SEQ_TILE].reshape(-1)
        ya = _flash(q, k, v, ct, cref, og_heads, n_heads=n_heads, n_pad=n_pad)

        w_o = w_out[l].astype(BF16)
        h2 = _out_proj(ya.reshape(bsz * lp, d_attn), yc.reshape(bsz * lp, d_attn),
                       h.reshape(bsz * lp, d), w_o[:d_attn], w_o[d_attn:], lp=lp, n_pad=n_pad)
        h2 = _ffn(h2, norm_ffn[l].reshape(1, d), w_gate[l].astype(BF16), w_up[l].astype(BF16),
                  w_down[l].astype(BF16))
        h = h2.reshape(bsz, lp, d)

    return _final_norm(h, final_norm.reshape(1, d), seq=seq)
```

```python
import functools
import math

import jax
import jax.numpy as jnp
from jax import lax
from jax.experimental import pallas as pl
from jax.experimental.pallas import tpu as pltpu

F32 = jnp.float32
BF16 = jnp.bfloat16

HEAD_DIM = 128
AUG_DIM = 2 * HEAD_DIM
N_SPLIT = 3
CONV_WIDTH = 3
EPS = 1e-6
NEG = -1e30
LOG2E = math.log2(math.e)
SEQ_TILE = 512
HEAD_PAD = 16
LANES = 128
HALO = 8
VMEM_LIMIT = 56 * 1024 * 1024

_NT = (((1,), (1,)), ((), ()))


def _cparams(n_axes):
    return pltpu.CompilerParams(dimension_semantics=("arbitrary",) * n_axes,
                                vmem_limit_bytes=VMEM_LIMIT)


def _rms_scale(x, axis=-1):
    return lax.rsqrt(jnp.mean(x * x, axis=axis, keepdims=True) + EPS)


def _cumsum_lanes(x):
    n = x.shape[1]
    lane = lax.broadcasted_iota(jnp.int32, x.shape, 1)
    shift = 1
    while shift < n:
        x = x + jnp.where(lane >= shift, pltpu.roll(x, shift, axis=1), 0.0)
        shift *= 2
    return x


def _round_bf16(x):
    return x.astype(BF16).astype(F32)


def _in_proj_kernel(h_ref, g_ref, w_ref, wvt_ref, wf_ref, bf_ref, cw_ref, og_ref,
                    q_ref, k_ref, vt_ref, yc_ref, ctile_ref,
                    hn_ref, gb_ref, u_ref, carry_ref, bias_ref,
                    *, tm, n_heads, n_pad, scale):
    i = pl.program_id(1)
    j = pl.program_id(2)

    @pl.when(j == 0)
    def _norm_and_gates():
        x = h_ref[0]
        hn = (x * _rms_scale(x) * g_ref[...]).astype(BF16)
        hn_ref[...] = hn
        logit = lax.dot_general(wf_ref[...], hn, _NT, preferred_element_type=F32)
        log_f = jax.nn.log_sigmoid(logit + bf_ref[...])

        @pl.when(i == 0)
        def _reset():
            carry_ref[...] = jnp.zeros_like(carry_ref)
            u_ref[pl.ds(0, HALO), :] = jnp.zeros((HALO, u_ref.shape[1]), F32)

        r = _cumsum_lanes(log_f)
        ctile_ref[0, 0] = carry_ref[...] * LOG2E
        carry_ref[...] = carry_ref[...] + jnp.broadcast_to(r[:, tm - 1:tm], carry_ref.shape)
        pos = i * tm + lax.broadcasted_iota(jnp.int32, r.shape, 1)
        bias = jnp.where(pos >= n_pad, -LOG2E * r, NEG)
        bias = jnp.concatenate([bias, jnp.zeros((LANES - HEAD_PAD, tm), F32)], axis=0)
        bias_ref[...] = bias.T

    @pl.when(j != 2)
    def _main():
        z = jnp.dot(hn_ref[...], w_ref[...], preferred_element_type=F32)

        @pl.when(j == 0)
        def _q():
            lane = lax.broadcasted_iota(jnp.int32, (tm, HEAD_DIM), 1)
            ones = jnp.where(lane < N_SPLIT, 1.0, 0.0).astype(BF16)
            for h in range(n_heads):
                q_ref[0, :, h * AUG_DIM:h * AUG_DIM + HEAD_DIM] = (
                    z[:, h * HEAD_DIM:(h + 1) * HEAD_DIM] * (scale * LOG2E)).astype(BF16)
                q_ref[0, :, h * AUG_DIM + HEAD_DIM:(h + 1) * AUG_DIM] = ones

        @pl.when(j == 1)
        def _k():
            lane = lax.broadcasted_iota(jnp.int32, (tm, HEAD_DIM), 1)
            for h in range(n_heads):
                k_ref[0, :, h * AUG_DIM:h * AUG_DIM + HEAD_DIM] = (
                    z[:, h * HEAD_DIM:(h + 1) * HEAD_DIM].astype(BF16))
                x = jnp.broadcast_to(bias_ref[:, h:h + 1], (tm, HEAD_DIM))
                hi = _round_bf16(x)
                mid = _round_bf16(x - hi)
                lo = x - hi - mid
                aug = jnp.where(lane == 0, hi, jnp.where(lane == 1, mid,
                                                         jnp.where(lane == 2, lo, 0.0)))
                k_ref[0, :, h * AUG_DIM + HEAD_DIM:(h + 1) * AUG_DIM] = aug.astype(BF16)

        @pl.when(j == 3)
        def _gate_b():
            gb_ref[...] = z

        @pl.when(j == 4)
        def _gate_c():
            u_ref[pl.ds(HALO, tm), :] = z

        @pl.when(j == 5)
        def _conv():
            u = u_ref[pl.ds(HALO, tm), :] * z
            u_ref[pl.ds(HALO, tm), :] = u
            u1 = u_ref[pl.ds(HALO - 1, tm), :]
            u2 = u_ref[pl.ds(HALO - 2, tm), :]
            conv = cw_ref[0:1, :] * u2 + cw_ref[1:2, :] * u1 + cw_ref[2:3, :] * u
            y = gb_ref[...] * conv
            for g in range(n_heads):
                cols = slice(g * HEAD_DIM, (g + 1) * HEAD_DIM)
                yg = y[:, cols]
                yc_ref[0, :, cols] = (yg * _rms_scale(yg) * og_ref[:, cols]).astype(BF16)
            u_ref[pl.ds(0, HALO), :] = u_ref[pl.ds(tm, HALO), :]

    @pl.when(j == 2)
    def _v():
        zt = lax.dot_general(wvt_ref[...], hn_ref[...], _NT, preferred_element_type=F32)
        vt_ref[0] = zt.astype(BF16)


def _in_proj(h, gain, w_main, wv_t, wf_t, bf_col, conv_w, og_conv, *, n_pad, scale):
    bsz, lp, d = h.shape
    tn = w_main.shape[1] // 6
    n_heads = tn // HEAD_DIM
    tm = SEQ_TILE
    n_t = lp // tm
    kern = functools.partial(_in_proj_kernel, tm=tm, n_heads=n_heads, n_pad=n_pad, scale=scale)
    aug_blk = pl.BlockSpec((1, tm, n_heads * AUG_DIM), lambda b, i, j: (b, i, 0))
    return pl.pallas_call(
        kern,
        grid=(bsz, n_t, 6),
        in_specs=[
            pl.BlockSpec((1, tm, d), lambda b, i, j: (b, i, 0)),
            pl.BlockSpec((1, d), lambda b, i, j: (0, 0)),
            pl.BlockSpec((d, tn), lambda b, i, j: (0, j)),
            pl.BlockSpec((tn, d), lambda b, i, j: (0, 0)),
            pl.BlockSpec((HEAD_PAD, d), lambda b, i, j: (0, 0)),
            pl.BlockSpec((HEAD_PAD, 1), lambda b, i, j: (0, 0)),
            pl.BlockSpec((HALO, tn), lambda b, i, j: (0, 0)),
            pl.BlockSpec((1, tn), lambda b, i, j: (0, 0)),
        ],
        out_specs=[
            aug_blk, aug_blk,
            pl.BlockSpec((1, tn, tm), lambda b, i, j: (b, 0, i)),
            pl.BlockSpec((1, tm, tn), lambda b, i, j: (b, i, 0)),
            pl.BlockSpec((1, 1, HEAD_PAD, LANES), lambda b, i, j: (b, i, 0, 0)),
        ],
        out_shape=[
            jax.ShapeDtypeStruct((bsz, lp, n_heads * AUG_DIM), BF16),
            jax.ShapeDtypeStruct((bsz, lp, n_heads * AUG_DIM), BF16),
            jax.ShapeDtypeStruct((bsz, tn, lp), BF16),
            jax.ShapeDtypeStruct((bsz, lp, tn), BF16),
            jax.ShapeDtypeStruct((bsz, n_t, HEAD_PAD, LANES), F32),
        ],
        scratch_shapes=[
            pltpu.VMEM((tm, d), BF16),
            pltpu.VMEM((tm, tn), F32),
            pltpu.VMEM((tm + HALO, tn), F32),
            pltpu.VMEM((HEAD_PAD, LANES), F32),
            pltpu.VMEM((tm, LANES), F32),
        ],
        compiler_params=_cparams(3),
        name="in_proj",
    )(h, gain, w_main, wv_t, wf_t, bf_col, conv_w, og_conv)


def _flash_kernel(c_ref, q_ref, k_ref, vt_ref, g_ref, o_ref, m_ref, l_ref, acc_ref,
                  *, tq, n_q):
    b = pl.program_id(0)
    h = pl.program_id(1)
    i = pl.program_id(2)
    q = q_ref[0]
    c_base = (b * n_q) * HEAD_PAD + h
    c_q = c_ref[c_base + i * HEAD_PAD]
    m_ref[...] = jnp.full_like(m_ref, NEG)
    l_ref[...] = jnp.zeros_like(l_ref)
    acc_ref[...] = jnp.zeros_like(acc_ref)

    def step(j, masked):
        start = pl.multiple_of(j * tq, tq)
        k = k_ref[0, pl.ds(start, tq), :]
        vt = vt_ref[0, :, pl.ds(start, tq)]
        st = lax.dot_general(k, q, _NT, preferred_element_type=F32)
        delta = c_q - c_ref[c_base + j * HEAD_PAD]
        if masked:
            key = lax.broadcasted_iota(jnp.int32, st.shape, 0)
            qry = lax.broadcasted_iota(jnp.int32, st.shape, 1)
            st = jnp.where(key <= qry, st, NEG)
        m_old = m_ref[...]
        m_new = jnp.maximum(m_old, jnp.max(st, axis=0, keepdims=True) + delta)
        alpha = jnp.exp2(m_old - m_new)
        p = jnp.exp2(st - (m_new - delta))
        l_ref[...] = alpha * l_ref[...] + jnp.sum(p, axis=0, keepdims=True)
        acc_ref[...] = alpha * acc_ref[...] + jnp.dot(vt, p.astype(BF16),
                                                      preferred_element_type=F32)
        m_ref[...] = m_new

    def body(j, carry):
        step(j, False)
        return carry

    lax.fori_loop(0, i, body, 0)
    step(i, True)
    ot = acc_ref[...] / l_ref[...]
    ot = ot * _rms_scale(ot, axis=0)
    o_ref[0] = (ot.T * g_ref[0]).astype(BF16)


def _flash(q, k, vt, c_tiles, og_heads):
    bsz, lp, _ = q.shape
    n_heads = vt.shape[1] // HEAD_DIM
    tq = SEQ_TILE
    n_q = lp // tq
    kern = functools.partial(_flash_kernel, tq=tq, n_q=n_q)
    return pl.pallas_call(
        kern,
        grid=(bsz, n_heads, n_q),
        in_specs=[
            pl.BlockSpec(memory_space=pltpu.SMEM),
            pl.BlockSpec((1, tq, AUG_DIM), lambda b, h, i: (b, i, h)),
            pl.BlockSpec((1, lp, AUG_DIM), lambda b, h, i: (b, 0, h)),
            pl.BlockSpec((1, HEAD_DIM, lp), lambda b, h, i: (b, h, 0)),
            pl.BlockSpec((1, 1, HEAD_DIM), lambda b, h, i: (h, 0, 0)),
        ],
        out_specs=pl.BlockSpec((1, tq, HEAD_DIM), lambda b, h, i: (b, i, h)),
        out_shape=jax.ShapeDtypeStruct((bsz, lp, n_heads * HEAD_DIM), BF16),
        scratch_shapes=[
            pltpu.VMEM((1, tq), F32),
            pltpu.VMEM((1, tq), F32),
            pltpu.VMEM((HEAD_DIM, tq), F32),
        ],
        compiler_params=_cparams(3),
        name="flash",
    )(c_tiles, q, k, vt, og_heads)


def _out_proj_kernel(ya_ref, yc_ref, h_ref, wa_ref, wc_ref, o_ref, *, tm, lp, n_pad):
    r = pl.program_id(0)
    out = (jnp.dot(ya_ref[...], wa_ref[...], preferred_element_type=F32)
           + jnp.dot(yc_ref[...], wc_ref[...], preferred_element_type=F32))
    out = h_ref[...] + out
    pos = (r * tm + lax.broadcasted_iota(jnp.int32, (tm, 1), 0)) % lp
    o_ref[...] = jnp.where(pos >= n_pad, out, 0.0)


def _out_proj(ya, yc, h, w_a, w_c, *, lp, n_pad):
    rows, d = h.shape
    da = ya.shape[1]
    tm = 512
    kern = functools.partial(_out_proj_kernel, tm=tm, lp=lp, n_pad=n_pad)
    return pl.pallas_call(
        kern,
        grid=(rows // tm,),
        in_specs=[
            pl.BlockSpec((tm, da), lambda r: (r, 0)),
            pl.BlockSpec((tm, da), lambda r: (r, 0)),
            pl.BlockSpec((tm, d), lambda r: (r, 0)),
            pl.BlockSpec((da, d), lambda r: (0, 0)),
            pl.BlockSpec((da, d), lambda r: (0, 0)),
        ],
        out_specs=pl.BlockSpec((tm, d), lambda r: (r, 0)),
        out_shape=jax.ShapeDtypeStruct(h.shape, F32),
        compiler_params=_cparams(1),
        name="out_proj",
    )(ya, yc, h, w_a, w_c)


def _ffn_kernel(h_ref, g_ref, wg_ref, wu_ref, wd_ref, o_ref, hn_ref):
    f = pl.program_id(1)

    @pl.when(f == 0)
    def _norm():
        x = h_ref[...]
        hn_ref[...] = (x * _rms_scale(x) * g_ref[...]).astype(BF16)
        o_ref[...] = x

    hn = hn_ref[...]
    gate = jnp.dot(hn, wg_ref[...], preferred_element_type=F32)
    up = jnp.dot(hn, wu_ref[...], preferred_element_type=F32)
    act = (jax.nn.silu(gate) * up).astype(BF16)
    o_ref[...] += jnp.dot(act, wd_ref[...], preferred_element_type=F32)


def _ffn(h, gain, w_gate, w_up, w_down):
    rows, d = h.shape
    d_ff = w_gate.shape[1]
    tm = 1024 if rows % 1024 == 0 else 512
    tf = 512 if d_ff % 512 == 0 else 256
    return pl.pallas_call(
        _ffn_kernel,
        grid=(rows // tm, d_ff // tf),
        in_specs=[
            pl.BlockSpec((tm, d), lambda r, f: (r, 0)),
            pl.BlockSpec((1, d), lambda r, f: (0, 0)),
            pl.BlockSpec((d, tf), lambda r, f: (0, f)),
            pl.BlockSpec((d, tf), lambda r, f: (0, f)),
            pl.BlockSpec((tf, d), lambda r, f: (f, 0)),
        ],
        out_specs=pl.BlockSpec((tm, d), lambda r, f: (r, 0)),
        out_shape=jax.ShapeDtypeStruct(h.shape, F32),
        scratch_shapes=[pltpu.VMEM((tm, d), BF16)],
        compiler_params=_cparams(2),
        name="ffn",
    )(h, gain, w_gate, w_up, w_down)


def _final_norm_kernel(h_ref, g_ref, o_ref):
    x = h_ref[0]
    o_ref[0] = x * _rms_scale(x) * g_ref[...]


def _final_norm(h, gain, *, seq):
    bsz, lp, d = h.shape
    tm = SEQ_TILE
    skip = (lp - seq) // tm
    return pl.pallas_call(
        _final_norm_kernel,
        grid=(bsz, seq // tm),
        in_specs=[
            pl.BlockSpec((1, tm, d), lambda b, i: (b, i + skip, 0)),
            pl.BlockSpec((1, d), lambda b, i: (0, 0)),
        ],
        out_specs=pl.BlockSpec((1, tm, d), lambda b, i: (b, i, 0)),
        out_shape=jax.ShapeDtypeStruct((bsz, seq, d), F32),
        compiler_params=_cparams(2),
        name="final_norm",
    )(h, gain)


def kernel(x, meta, norm_mix, w_in, b_f, conv_w, out_gain, w_out, norm_ffn, w_gate, w_up, w_down, final_norm):
    bsz, seq, d = x.shape
    n_meta = meta.shape[0]
    depth = w_in.shape[0]
    d_attn = d // 2
    n_heads = d_attn // HEAD_DIM
    assert n_heads <= HEAD_PAD and seq % SEQ_TILE == 0 and n_meta <= SEQ_TILE
    assert w_in.shape[2] == 6 * d_attn + n_heads
    lp = seq + SEQ_TILE
    n_pad = lp - seq - n_meta
    scale = HEAD_DIM ** -0.5

    m = jnp.broadcast_to(meta.astype(x.dtype)[None], (bsz, n_meta, d))
    h = jnp.concatenate([jnp.zeros((bsz, n_pad, d), x.dtype), m, x], axis=1)

    for l in range(depth):
        w_main = w_in[l, :, :6 * d_attn].astype(BF16)
        wv_t = w_in[l, :, 2 * d_attn:3 * d_attn].T.astype(BF16)
        wf_t = jnp.zeros((HEAD_PAD, d), BF16).at[:n_heads].set(w_in[l, :, 6 * d_attn:].T.astype(BF16))
        bf_col = jnp.zeros((HEAD_PAD, 1), F32).at[:n_heads, 0].set(b_f[l])
        cw = jnp.zeros((HALO, d_attn), F32).at[:CONV_WIDTH].set(conv_w[l])
        og_heads = out_gain[l, :d_attn].reshape(n_heads, 1, HEAD_DIM)
        og_conv = out_gain[l, d_attn:].reshape(1, d_attn)

        q, k, vt, yc, ctile = _in_proj(h, norm_mix[l].reshape(1, d), w_main, wv_t, wf_t, bf_col, cw,
                                       og_conv, n_pad=n_pad, scale=scale)
        ya = _flash(q, k, vt, ctile[:, :, :, 0].reshape(-1), og_heads)

        w_o = w_out[l].astype(BF16)
        h2 = _out_proj(ya.reshape(bsz * lp, d_attn), yc.reshape(bsz * lp, d_attn),
                       h.reshape(bsz * lp, d), w_o[:d_attn], w_o[d_attn:], lp=lp, n_pad=n_pad)
        h2 = _ffn(h2, norm_ffn[l].reshape(1, d), w_gate[l].astype(BF16), w_up[l].astype(BF16),
                  w_down[l].astype(BF16))
        h = h2.reshape(bsz, lp, d)

    return _final_norm(h, final_norm.reshape(1, d), seq=seq)
```

```python
import functools
import math

import jax
import jax.numpy as jnp
from jax import lax
from jax.experimental import pallas as pl
from jax.experimental.pallas import tpu as pltpu

F32 = jnp.float32
BF16 = jnp.bfloat16

HEAD_DIM = 128
AUG_DIM = 2 * HEAD_DIM
N_SPLIT = 3
CONV_WIDTH = 3
EPS = 1e-6
NEG = -1e30
LOG2E = math.log2(math.e)
SEQ_TILE = 512
HEAD_PAD = 16
LANES = 128
HALO = 8
VMEM_LIMIT = 56 * 1024 * 1024

_NT = (((1,), (1,)), ((), ()))


def _cparams(n_axes):
    return pltpu.CompilerParams(dimension_semantics=("arbitrary",) * n_axes,
                                vmem_limit_bytes=VMEM_LIMIT)


def _rms_scale(x, axis=-1):
    return lax.rsqrt(jnp.mean(x * x, axis=axis, keepdims=True) + EPS)


def _cumsum_lanes(x):
    n = x.shape[1]
    lane = lax.broadcasted_iota(jnp.int32, x.shape, 1)
    shift = 1
    while shift < n:
        x = x + jnp.where(lane >= shift, pltpu.roll(x, shift, axis=1), 0.0)
        shift *= 2
    return x


def _round_bf16(x):
    return x.astype(BF16).astype(F32)


def _in_proj_kernel(h_ref, g_ref, w_ref, wvt_ref, wf_ref, bf_ref, cw_ref, og_ref,
                    q_ref, k_ref, vt_ref, yc_ref, ctile_ref,
                    hn_ref, gb_ref, u_ref, carry_ref, bias_ref,
                    *, tm, n_heads, n_pad, scale):
    i = pl.program_id(1)
    j = pl.program_id(2)

    @pl.when(j == 0)
    def _norm_and_gates():
        x = h_ref[0]
        hn = (x * _rms_scale(x) * g_ref[...]).astype(BF16)
        hn_ref[...] = hn
        logit = lax.dot_general(wf_ref[...], hn, _NT, preferred_element_type=F32)
        log_f = jax.nn.log_sigmoid(logit + bf_ref[...])

        @pl.when(i == 0)
        def _reset():
            carry_ref[...] = jnp.zeros_like(carry_ref)
            u_ref[pl.ds(0, HALO), :] = jnp.zeros((HALO, u_ref.shape[1]), F32)

        r = _cumsum_lanes(log_f)
        ctile_ref[0, 0] = carry_ref[...] * LOG2E
        carry_ref[...] = carry_ref[...] + jnp.broadcast_to(r[:, tm - 1:tm], carry_ref.shape)
        pos = i * tm + lax.broadcasted_iota(jnp.int32, r.shape, 1)
        bias = jnp.where(pos >= n_pad, -LOG2E * r, NEG)
        bias = jnp.concatenate([bias, jnp.zeros((LANES - HEAD_PAD, tm), F32)], axis=0)
        bias_ref[...] = bias.T

    @pl.when(j != 2)
    def _main():
        z = jnp.dot(hn_ref[...], w_ref[...], preferred_element_type=F32)

        @pl.when(j == 0)
        def _q():
            lane = lax.broadcasted_iota(jnp.int32, (tm, HEAD_DIM), 1)
            ones = jnp.where(lane < N_SPLIT, 1.0, 0.0).astype(BF16)
            for h in range(n_heads):
                q_ref[0, :, h * AUG_DIM:h * AUG_DIM + HEAD_DIM] = (
                    z[:, h * HEAD_DIM:(h + 1) * HEAD_DIM] * (scale * LOG2E)).astype(BF16)
                q_ref[0, :, h * AUG_DIM + HEAD_DIM:(h + 1) * AUG_DIM] = ones

        @pl.when(j == 1)
        def _k():
            lane = lax.broadcasted_iota(jnp.int32, (tm, HEAD_DIM), 1)
            for h in range(n_heads):
                k_ref[0, :, h * AUG_DIM:h * AUG_DIM + HEAD_DIM] = (
                    z[:, h * HEAD_DIM:(h + 1) * HEAD_DIM].astype(BF16))
                x = jnp.broadcast_to(bias_ref[:, h:h + 1], (tm, HEAD_DIM))
                hi = _round_bf16(x)
                mid = _round_bf16(x - hi)
                lo = x - hi - mid
                aug = jnp.where(lane == 0, hi, jnp.where(lane == 1, mid,
                                                         jnp.where(lane == 2, lo, 0.0)))
                k_ref[0, :, h * AUG_DIM + HEAD_DIM:(h + 1) * AUG_DIM] = aug.astype(BF16)

        @pl.when(j == 3)
        def _gate_b():
            gb_ref[...] = z

        @pl.when(j == 4)
        def _gate_c():
            u_ref[pl.ds(HALO, tm), :] = z

        @pl.when(j == 5)
        def _conv():
            u = u_ref[pl.ds(HALO, tm), :] * z
            u_ref[pl.ds(HALO, tm), :] = u
            u1 = u_ref[pl.ds(HALO - 1, tm), :]
            u2 = u_ref[pl.ds(HALO - 2, tm), :]
            conv = cw_ref[0:1, :] * u2 + cw_ref[1:2, :] * u1 + cw_ref[2:3, :] * u
            y = gb_ref[...] * conv
            for g in range(n_heads):
                cols = slice(g * HEAD_DIM, (g + 1) * HEAD_DIM)
                yg = y[:, cols]
                yc_ref[0, :, cols] = (yg * _rms_scale(yg) * og_ref[:, cols]).astype(BF16)
            u_ref[pl.ds(0, HALO), :] = u_ref[pl.ds(tm, HALO), :]

    @pl.when(j == 2)
    def _v():
        zt = lax.dot_general(wvt_ref[...], hn_ref[...], _NT, preferred_element_type=F32)
        vt_ref[0] = zt.astype(BF16)


def _in_proj(h, gain, w_main, wv_t, wf_t, bf_col, conv_w, og_conv, *, n_pad, scale):
    bsz, lp, d = h.shape
    tn = w_main.shape[1] // 6
    n_heads = tn // HEAD_DIM
    tm = SEQ_TILE
    n_t = lp // tm
    kern = functools.partial(_in_proj_kernel, tm=tm, n_heads=n_heads, n_pad=n_pad, scale=scale)
    aug_blk = pl.BlockSpec((1, tm, n_heads * AUG_DIM), lambda b, i, j: (b, i, 0))
    return pl.pallas_call(
        kern,
        grid=(bsz, n_t, 6),
        in_specs=[
            pl.BlockSpec((1, tm, d), lambda b, i, j: (b, i, 0)),
            pl.BlockSpec((1, d), lambda b, i, j: (0, 0)),
            pl.BlockSpec((d, tn), lambda b, i, j: (0, j)),
            pl.BlockSpec((tn, d), lambda b, i, j: (0, 0)),
            pl.BlockSpec((HEAD_PAD, d), lambda b, i, j: (0, 0)),
            pl.BlockSpec((HEAD_PAD, 1), lambda b, i, j: (0, 0)),
            pl.BlockSpec((HALO, tn), lambda b, i, j: (0, 0)),
            pl.BlockSpec((1, tn), lambda b, i, j: (0, 0)),
        ],
        out_specs=[
            aug_blk, aug_blk,
            pl.BlockSpec((1, tn, tm), lambda b, i, j: (b, 0, i)),
            pl.BlockSpec((1, tm, tn), lambda b, i, j: (b, i, 0)),
            pl.BlockSpec((1, 1, HEAD_PAD, LANES), lambda b, i, j: (b, i, 0, 0)),
        ],
        out_shape=[
            jax.ShapeDtypeStruct((bsz, lp, n_heads * AUG_DIM), BF16),
            jax.ShapeDtypeStruct((bsz, lp, n_heads * AUG_DIM), BF16),
            jax.ShapeDtypeStruct((bsz, tn, lp), BF16),
            jax.ShapeDtypeStruct((bsz, lp, tn), BF16),
            jax.ShapeDtypeStruct((bsz, n_t, HEAD_PAD, LANES), F32),
        ],
        scratch_shapes=[
            pltpu.VMEM((tm, d), BF16),
            pltpu.VMEM((tm, tn), F32),
            pltpu.VMEM((tm + HALO, tn), F32),
            pltpu.VMEM((HEAD_PAD, LANES), F32),
            pltpu.VMEM((tm, LANES), F32),
        ],
        compiler_params=_cparams(3),
        name="in_proj",
    )(h, gain, w_main, wv_t, wf_t, bf_col, conv_w, og_conv)


def _flash_kernel(c_ref, q_ref, k_ref, vt_ref, g_ref, o_ref, acc_ref, sa_ref, sb_ref,
                  *, tq, n_q):
    b = pl.program_id(0)
    h = pl.program_id(1)
    i = pl.program_id(2)
    c_base = (b * n_q) * HEAD_PAD + h
    c_q = c_ref[c_base + i * HEAD_PAD]
    last = jnp.maximum(i - 1, 0)

    def scores(j):
        start = pl.multiple_of(j * tq, tq)
        return lax.dot_general(k_ref[0, pl.ds(start, tq), :], q_ref[0], _NT,
                               preferred_element_type=F32)

    def delta_of(j):
        c_j = c_ref[c_base + jnp.minimum(j, last) * HEAD_PAD]
        return jnp.where(j < i, c_q - c_j, NEG)

    def accumulate(st, j, delta, m_old, l_old):
        start = pl.multiple_of(j * tq, tq)
        vt = vt_ref[0, :, pl.ds(start, tq)]
        m_new = jnp.maximum(m_old, jnp.max(st, axis=0, keepdims=True) + delta)
        alpha = jnp.exp2(m_old - m_new)
        p = jnp.exp2(st - (m_new - delta))
        l_new = alpha * l_old + jnp.sum(p, axis=0, keepdims=True)
        acc_ref[...] = alpha * acc_ref[...] + jnp.dot(vt, p.astype(BF16),
                                                      preferred_element_type=F32)
        return m_new, l_new

    acc_ref[...] = jnp.zeros_like(acc_ref)
    st = scores(i)
    key = lax.broadcasted_iota(jnp.int32, st.shape, 0)
    qry = lax.broadcasted_iota(jnp.int32, st.shape, 1)
    st = jnp.where(key <= qry, st, NEG)
    sa_ref[...] = scores(jnp.minimum(0, last))
    m, l = accumulate(st, i, 0.0, jnp.full((1, tq), NEG, F32), jnp.zeros((1, tq), F32))

    n_pairs = jnp.maximum((i + 1) // 2, 1)

    def pair(jj, carry):
        m, l = carry
        j0 = 2 * jj
        sb_ref[...] = scores(jnp.minimum(j0 + 1, last))
        m, l = accumulate(sa_ref[...], jnp.minimum(j0, last), delta_of(j0), m, l)
        sa_ref[...] = scores(jnp.minimum(j0 + 2, last))
        m, l = accumulate(sb_ref[...], jnp.minimum(j0 + 1, last), delta_of(j0 + 1), m, l)
        return m, l

    m, l = lax.fori_loop(0, n_pairs - 1, pair, (m, l))
    j0 = 2 * (n_pairs - 1)
    sb_ref[...] = scores(jnp.minimum(j0 + 1, last))
    m, l = accumulate(sa_ref[...], jnp.minimum(j0, last), delta_of(j0), m, l)
    m, l = accumulate(sb_ref[...], jnp.minimum(j0 + 1, last), delta_of(j0 + 1), m, l)

    ot = acc_ref[...] / l
    ot = ot * _rms_scale(ot, axis=0)
    o_ref[0] = (ot.T * g_ref[0]).astype(BF16)


def _flash(q, k, vt, c_tiles, og_heads):
    bsz, lp, _ = q.shape
    n_heads = vt.shape[1] // HEAD_DIM
    tq = SEQ_TILE
    n_q = lp // tq
    kern = functools.partial(_flash_kernel, tq=tq, n_q=n_q)
    return pl.pallas_call(
        kern,
        grid=(bsz, n_heads, n_q),
        in_specs=[
            pl.BlockSpec(memory_space=pltpu.SMEM),
            pl.BlockSpec((1, tq, AUG_DIM), lambda b, h, i: (b, i, h)),
            pl.BlockSpec((1, lp, AUG_DIM), lambda b, h, i: (b, 0, h)),
            pl.BlockSpec((1, HEAD_DIM, lp), lambda b, h, i: (b, h, 0)),
            pl.BlockSpec((1, 1, HEAD_DIM), lambda b, h, i: (h, 0, 0)),
        ],
        out_specs=pl.BlockSpec((1, tq, HEAD_DIM), lambda b, h, i: (b, i, h)),
        out_shape=jax.ShapeDtypeStruct((bsz, lp, n_heads * HEAD_DIM), BF16),
        scratch_shapes=[
            pltpu.VMEM((HEAD_DIM, tq), F32),
            pltpu.VMEM((tq, tq), F32),
            pltpu.VMEM((tq, tq), F32),
        ],
        compiler_params=_cparams(3),
        name="flash",
    )(c_tiles, q, k, vt, og_heads)


def _out_proj_kernel(ya_ref, yc_ref, h_ref, wa_ref, wc_ref, o_ref, *, tm, lp, n_pad):
    r = pl.program_id(0)
    out = (jnp.dot(ya_ref[...], wa_ref[...], preferred_element_type=F32)
           + jnp.dot(yc_ref[...], wc_ref[...], preferred_element_type=F32))
    out = h_ref[...] + out
    pos = (r * tm + lax.broadcasted_iota(jnp.int32, (tm, 1), 0)) % lp
    o_ref[...] = jnp.where(pos >= n_pad, out, 0.0)


def _out_proj(ya, yc, h, w_a, w_c, *, lp, n_pad):
    rows, d = h.shape
    da = ya.shape[1]
    tm = 512
    kern = functools.partial(_out_proj_kernel, tm=tm, lp=lp, n_pad=n_pad)
    return pl.pallas_call(
        kern,
        grid=(rows // tm,),
        in_specs=[
            pl.BlockSpec((tm, da), lambda r: (r, 0)),
            pl.BlockSpec((tm, da), lambda r: (r, 0)),
            pl.BlockSpec((tm, d), lambda r: (r, 0)),
            pl.BlockSpec((da, d), lambda r: (0, 0)),
            pl.BlockSpec((da, d), lambda r: (0, 0)),
        ],
        out_specs=pl.BlockSpec((tm, d), lambda r: (r, 0)),
        out_shape=jax.ShapeDtypeStruct(h.shape, F32),
        compiler_params=_cparams(1),
        name="out_proj",
    )(ya, yc, h, w_a, w_c)


def _ffn_kernel(h_ref, g_ref, wg_ref, wu_ref, wd_ref, o_ref, hn_ref):
    f = pl.program_id(1)

    @pl.when(f == 0)
    def _norm():
        x = h_ref[...]
        hn_ref[...] = (x * _rms_scale(x) * g_ref[...]).astype(BF16)
        o_ref[...] = x

    hn = hn_ref[...]
    gate = jnp.dot(hn, wg_ref[...], preferred_element_type=F32)
    up = jnp.dot(hn, wu_ref[...], preferred_element_type=F32)
    act = (jax.nn.silu(gate) * up).astype(BF16)
    o_ref[...] += jnp.dot(act, wd_ref[...], preferred_element_type=F32)


def _ffn(h, gain, w_gate, w_up, w_down):
    rows, d = h.shape
    d_ff = w_gate.shape[1]
    tm = 1024 if rows % 1024 == 0 else 512
    tf = 512 if d_ff % 512 == 0 else 256
    return pl.pallas_call(
        _ffn_kernel,
        grid=(rows // tm, d_ff // tf),
        in_specs=[
            pl.BlockSpec((tm, d), lambda r, f: (r, 0)),
            pl.BlockSpec((1, d), lambda r, f: (0, 0)),
            pl.BlockSpec((d, tf), lambda r, f: (0, f)),
            pl.BlockSpec((d, tf), lambda r, f: (0, f)),
            pl.BlockSpec((tf, d), lambda r, f: (f, 0)),
        ],
        out_specs=pl.BlockSpec((tm, d), lambda r, f: (r, 0)),
        out_shape=jax.ShapeDtypeStruct(h.shape, F32),
        scratch_shapes=[pltpu.VMEM((tm, d), BF16)],
        compiler_params=_cparams(2),
        name="ffn",
    )(h, gain, w_gate, w_up, w_down)


def _final_norm_kernel(h_ref, g_ref, o_ref):
    x = h_ref[0]
    o_ref[0] = x * _rms_scale(x) * g_ref[...]


def _final_norm(h, gain, *, seq):
    bsz, lp, d = h.shape
    tm = SEQ_TILE
    skip = (lp - seq) // tm
    return pl.pallas_call(
        _final_norm_kernel,
        grid=(bsz, seq // tm),
        in_specs=[
            pl.BlockSpec((1, tm, d), lambda b, i: (b, i + skip, 0)),
            pl.BlockSpec((1, d), lambda b, i: (0, 0)),
        ],
        out_specs=pl.BlockSpec((1, tm, d), lambda b, i: (b, i, 0)),
        out_shape=jax.ShapeDtypeStruct((bsz, seq, d), F32),
        compiler_params=_cparams(2),
        name="final_norm",
    )(h, gain)


def kernel(x, meta, norm_mix, w_in, b_f, conv_w, out_gain, w_out, norm_ffn, w_gate, w_up, w_down, final_norm):
    bsz, seq, d = x.shape
    n_meta = meta.shape[0]
    depth = w_in.shape[0]
    d_attn = d // 2
    n_heads = d_attn // HEAD_DIM
    assert n_heads <= HEAD_PAD and seq % SEQ_TILE == 0 and n_meta <= SEQ_TILE
    assert w_in.shape[2] == 6 * d_attn + n_heads
    lp = seq + SEQ_TILE
    n_pad = lp - seq - n_meta
    scale = HEAD_DIM ** -0.5

    m = jnp.broadcast_to(meta.astype(x.dtype)[None], (bsz, n_meta, d))
    h = jnp.concatenate([jnp.zeros((bsz, n_pad, d), x.dtype), m, x], axis=1)

    for l in range(depth):
        w_main = w_in[l, :, :6 * d_attn].astype(BF16)
        wv_t = w_in[l, :, 2 * d_attn:3 * d_attn].T.astype(BF16)
        wf_t = jnp.zeros((HEAD_PAD, d), BF16).at[:n_heads].set(w_in[l, :, 6 * d_attn:].T.astype(BF16))
        bf_col = jnp.zeros((HEAD_PAD, 1), F32).at[:n_heads, 0].set(b_f[l])
        cw = jnp.zeros((HALO, d_attn), F32).at[:CONV_WIDTH].set(conv_w[l])
        og_heads = out_gain[l, :d_attn].reshape(n_heads, 1, HEAD_DIM)
        og_conv = out_gain[l, d_attn:].reshape(1, d_attn)

        q, k, vt, yc, ctile = _in_proj(h, norm_mix[l].reshape(1, d), w_main, wv_t, wf_t, bf_col, cw,
                                       og_conv, n_pad=n_pad, scale=scale)
        ya = _flash(q, k, vt, ctile[:, :, :, 0].reshape(-1), og_heads)

        w_o = w_out[l].astype(BF16)
        h2 = _out_proj(ya.reshape(bsz * lp, d_attn), yc.reshape(bsz * lp, d_attn),
                       h.reshape(bsz * lp, d), w_o[:d_attn], w_o[d_attn:], lp=lp, n_pad=n_pad)
        h2 = _ffn(h2, norm_ffn[l].reshape(1, d), w_gate[l].astype(BF16), w_up[l].astype(BF16),
                  w_down[l].astype(BF16))
        h = h2.reshape(bsz, lp, d)

    return _final_norm(h, final_norm.reshape(1, d), seq=seq)
```

```python
import functools
import math

import jax
import jax.numpy as jnp
from jax import lax
from jax.experimental import pallas as pl
from jax.experimental.pallas import tpu as pltpu

F32 = jnp.float32
BF16 = jnp.bfloat16

HEAD_DIM = 128
AUG_DIM = 2 * HEAD_DIM
N_SPLIT = 3
CONV_WIDTH = 3
EPS = 1e-6
NEG = -1e30
LOG2E = math.log2(math.e)
SEQ_TILE = 512
HEAD_PAD = 16
ONES_ROWS = 16
LANES = 128
HALO = 8
VMEM_LIMIT = 56 * 1024 * 1024

_NT = (((1,), (1,)), ((), ()))


def _cparams(n_axes):
    return pltpu.CompilerParams(dimension_semantics=("arbitrary",) * n_axes,
                                vmem_limit_bytes=VMEM_LIMIT)


def _rms_scale(x, axis=-1):
    return lax.rsqrt(jnp.mean(x * x, axis=axis, keepdims=True) + EPS)


def _cumsum_lanes(x):
    n = x.shape[1]
    lane = lax.broadcasted_iota(jnp.int32, x.shape, 1)
    shift = 1
    while shift < n:
        x = x + jnp.where(lane >= shift, pltpu.roll(x, shift, axis=1), 0.0)
        shift *= 2
    return x


def _round_bf16(x):
    return x.astype(BF16).astype(F32)


def _in_proj_kernel(h_ref, g_ref, w_ref, wvt_ref, wf_ref, bf_ref, cw_ref, og_ref,
                    q_ref, k_ref, vt_ref, yc_ref, ctile_ref,
                    hn_ref, gb_ref, u_ref, carry_ref, bias_ref,
                    *, tm, n_heads, n_pad, scale):
    i = pl.program_id(1)
    j = pl.program_id(2)

    @pl.when(j == 0)
    def _norm_and_gates():
        x = h_ref[0]
        hn = (x * _rms_scale(x) * g_ref[...]).astype(BF16)
        hn_ref[...] = hn
        logit = lax.dot_general(wf_ref[...], hn, _NT, preferred_element_type=F32)
        log_f = jax.nn.log_sigmoid(logit + bf_ref[...])

        @pl.when(i == 0)
        def _reset():
            carry_ref[...] = jnp.zeros_like(carry_ref)
            u_ref[pl.ds(0, HALO), :] = jnp.zeros((HALO, u_ref.shape[1]), F32)

        r = _cumsum_lanes(log_f)
        ctile_ref[0, 0] = carry_ref[...] * LOG2E
        carry_ref[...] = carry_ref[...] + jnp.broadcast_to(r[:, tm - 1:tm], carry_ref.shape)
        pos = i * tm + lax.broadcasted_iota(jnp.int32, r.shape, 1)
        bias = jnp.where(pos >= n_pad, -LOG2E * r, NEG)
        bias = jnp.concatenate([bias, jnp.zeros((LANES - HEAD_PAD, tm), F32)], axis=0)
        bias_ref[...] = bias.T

    @pl.when(j != 2)
    def _main():
        z = jnp.dot(hn_ref[...], w_ref[...], preferred_element_type=F32)

        @pl.when(j == 0)
        def _q():
            lane = lax.broadcasted_iota(jnp.int32, (tm, HEAD_DIM), 1)
            ones = jnp.where(lane < N_SPLIT, 1.0, 0.0).astype(BF16)
            for h in range(n_heads):
                q_ref[0, :, h * AUG_DIM:h * AUG_DIM + HEAD_DIM] = (
                    z[:, h * HEAD_DIM:(h + 1) * HEAD_DIM] * (scale * LOG2E)).astype(BF16)
                q_ref[0, :, h * AUG_DIM + HEAD_DIM:(h + 1) * AUG_DIM] = ones

        @pl.when(j == 1)
        def _k():
            lane = lax.broadcasted_iota(jnp.int32, (tm, HEAD_DIM), 1)
            for h in range(n_heads):
                k_ref[0, :, h * AUG_DIM:h * AUG_DIM + HEAD_DIM] = (
                    z[:, h * HEAD_DIM:(h + 1) * HEAD_DIM].astype(BF16))
                x = jnp.broadcast_to(bias_ref[:, h:h + 1], (tm, HEAD_DIM))
                hi = _round_bf16(x)
                mid = _round_bf16(x - hi)
                lo = x - hi - mid
                aug = jnp.where(lane == 0, hi, jnp.where(lane == 1, mid,
                                                         jnp.where(lane == 2, lo, 0.0)))
                k_ref[0, :, h * AUG_DIM + HEAD_DIM:(h + 1) * AUG_DIM] = aug.astype(BF16)

        @pl.when(j == 3)
        def _gate_b():
            gb_ref[...] = z

        @pl.when(j == 4)
        def _gate_c():
            u_ref[pl.ds(HALO, tm), :] = z

        @pl.when(j == 5)
        def _conv():
            u = u_ref[pl.ds(HALO, tm), :] * z
            u_ref[pl.ds(HALO, tm), :] = u
            u1 = u_ref[pl.ds(HALO - 1, tm), :]
            u2 = u_ref[pl.ds(HALO - 2, tm), :]
            conv = cw_ref[0:1, :] * u2 + cw_ref[1:2, :] * u1 + cw_ref[2:3, :] * u
            y = gb_ref[...] * conv
            for g in range(n_heads):
                cols = slice(g * HEAD_DIM, (g + 1) * HEAD_DIM)
                yg = y[:, cols]
                yc_ref[0, :, cols] = (yg * _rms_scale(yg) * og_ref[:, cols]).astype(BF16)
            u_ref[pl.ds(0, HALO), :] = u_ref[pl.ds(tm, HALO), :]

    @pl.when(j == 2)
    def _v():
        zt = lax.dot_general(wvt_ref[...], hn_ref[...], _NT, preferred_element_type=F32)
        vt_ref[0] = zt.astype(BF16)


def _in_proj(h, gain, w_main, wv_t, wf_t, bf_col, conv_w, og_conv, *, n_pad, scale):
    bsz, lp, d = h.shape
    tn = w_main.shape[1] // 6
    n_heads = tn // HEAD_DIM
    tm = SEQ_TILE
    n_t = lp // tm
    kern = functools.partial(_in_proj_kernel, tm=tm, n_heads=n_heads, n_pad=n_pad, scale=scale)
    aug_blk = pl.BlockSpec((1, tm, n_heads * AUG_DIM), lambda b, i, j: (b, i, 0))
    return pl.pallas_call(
        kern,
        grid=(bsz, n_t, 6),
        in_specs=[
            pl.BlockSpec((1, tm, d), lambda b, i, j: (b, i, 0)),
            pl.BlockSpec((1, d), lambda b, i, j: (0, 0)),
            pl.BlockSpec((d, tn), lambda b, i, j: (0, j)),
            pl.BlockSpec((tn, d), lambda b, i, j: (0, 0)),
            pl.BlockSpec((HEAD_PAD, d), lambda b, i, j: (0, 0)),
            pl.BlockSpec((HEAD_PAD, 1), lambda b, i, j: (0, 0)),
            pl.BlockSpec((HALO, tn), lambda b, i, j: (0, 0)),
            pl.BlockSpec((1, tn), lambda b, i, j: (0, 0)),
        ],
        out_specs=[
            aug_blk, aug_blk,
            pl.BlockSpec((1, tn, tm), lambda b, i, j: (b, 0, i)),
            pl.BlockSpec((1, tm, tn), lambda b, i, j: (b, i, 0)),
            pl.BlockSpec((1, 1, HEAD_PAD, LANES), lambda b, i, j: (b, i, 0, 0)),
        ],
        out_shape=[
            jax.ShapeDtypeStruct((bsz, lp, n_heads * AUG_DIM), BF16),
            jax.ShapeDtypeStruct((bsz, lp, n_heads * AUG_DIM), BF16),
            jax.ShapeDtypeStruct((bsz, tn, lp), BF16),
            jax.ShapeDtypeStruct((bsz, lp, tn), BF16),
            jax.ShapeDtypeStruct((bsz, n_t, HEAD_PAD, LANES), F32),
        ],
        scratch_shapes=[
            pltpu.VMEM((tm, d), BF16),
            pltpu.VMEM((tm, tn), F32),
            pltpu.VMEM((tm + HALO, tn), F32),
            pltpu.VMEM((HEAD_PAD, LANES), F32),
            pltpu.VMEM((tm, LANES), F32),
        ],
        compiler_params=_cparams(3),
        name="in_proj",
    )(h, gain, w_main, wv_t, wf_t, bf_col, conv_w, og_conv)


def _flash_kernel(c_ref, q_ref, k_ref, vt_ref, g_ref, o_ref, acc_ref, sa_ref, sb_ref,
                  *, tq, n_q, hpb):
    b = pl.program_id(0)
    hp = pl.program_id(1)
    i = pl.program_id(2)
    heads = range(hpb)
    c_base = [(b * n_q) * HEAD_PAD + hp * hpb + hh for hh in heads]
    c_q = [c_ref[c_base[hh] + i * HEAD_PAD] for hh in heads]
    last = jnp.maximum(i - 1, 0)

    def scores(hh, j):
        start = pl.multiple_of(j * tq, tq)
        cols = slice(hh * AUG_DIM, (hh + 1) * AUG_DIM)
        return lax.dot_general(k_ref[0, pl.ds(start, tq), cols], q_ref[0, :, cols], _NT,
                               preferred_element_type=F32)

    def delta_of(hh, j):
        c_j = c_ref[c_base[hh] + jnp.minimum(j, last) * HEAD_PAD]
        return jnp.where(j < i, c_q[hh] - c_j, NEG)

    ones_rows = jnp.ones((ONES_ROWS, tq), BF16)

    def accumulate(hh, st, j, delta, m_old):
        start = pl.multiple_of(j * tq, tq)
        vt = vt_ref[0, hh * HEAD_DIM:(hh + 1) * HEAD_DIM, pl.ds(start, tq)]
        m_new = jnp.maximum(m_old, jnp.max(st, axis=0, keepdims=True) + delta)
        alpha = jnp.exp2(m_old - m_new)
        p = jnp.exp2(st - (m_new - delta)).astype(BF16)
        pv = jnp.dot(jnp.concatenate([vt, ones_rows], axis=0), p, preferred_element_type=F32)
        acc_ref[hh] = alpha * acc_ref[hh] + pv
        return m_new

    def accumulate_all(s_ref, j, m_all):
        return tuple(accumulate(hh, s_ref[hh], jnp.minimum(j, last), delta_of(hh, j), m_all[hh])
                     for hh in heads)

    acc_ref[...] = jnp.zeros_like(acc_ref)
    key = lax.broadcasted_iota(jnp.int32, (tq, tq), 0)
    qry = lax.broadcasted_iota(jnp.int32, (tq, tq), 1)
    diag = [jnp.where(key <= qry, scores(hh, i), NEG) for hh in heads]
    for hh in heads:
        sa_ref[hh] = scores(hh, jnp.minimum(0, last))
    stats = tuple(accumulate(hh, diag[hh], i, 0.0, jnp.full((1, tq), NEG, F32)) for hh in heads)

    n_pairs = jnp.maximum((i + 1) // 2, 1)

    def pair(jj, stats):
        j0 = 2 * jj
        for hh in heads:
            sb_ref[hh] = scores(hh, jnp.minimum(j0 + 1, last))
        stats = accumulate_all(sa_ref, j0, stats)
        for hh in heads:
            sa_ref[hh] = scores(hh, jnp.minimum(j0 + 2, last))
        return accumulate_all(sb_ref, j0 + 1, stats)

    stats = lax.fori_loop(0, n_pairs - 1, pair, stats)
    j0 = 2 * (n_pairs - 1)
    for hh in heads:
        sb_ref[hh] = scores(hh, jnp.minimum(j0 + 1, last))
    stats = accumulate_all(sa_ref, j0, stats)
    stats = accumulate_all(sb_ref, j0 + 1, stats)

    for hh in heads:
        ot = acc_ref[hh, :HEAD_DIM, :] / acc_ref[hh, HEAD_DIM:HEAD_DIM + 1, :]
        ot = ot * _rms_scale(ot, axis=0)
        o_ref[0, :, hh * HEAD_DIM:(hh + 1) * HEAD_DIM] = (ot.T * g_ref[hh]).astype(BF16)


def _flash(q, k, vt, c_tiles, og_heads):
    bsz, lp, _ = q.shape
    n_heads = vt.shape[1] // HEAD_DIM
    hpb = 2 if n_heads % 2 == 0 else 1
    tq = SEQ_TILE
    n_q = lp // tq
    kern = functools.partial(_flash_kernel, tq=tq, n_q=n_q, hpb=hpb)
    resident = pl.Buffered(1)
    return pl.pallas_call(
        kern,
        grid=(bsz, n_heads // hpb, n_q),
        in_specs=[
            pl.BlockSpec(memory_space=pltpu.SMEM),
            pl.BlockSpec((1, tq, hpb * AUG_DIM), lambda b, h, i: (b, i, h)),
            pl.BlockSpec((1, lp, hpb * AUG_DIM), lambda b, h, i: (b, 0, h),
                         pipeline_mode=resident),
            pl.BlockSpec((1, hpb * HEAD_DIM, lp), lambda b, h, i: (b, h, 0),
                         pipeline_mode=resident),
            pl.BlockSpec((hpb, 1, HEAD_DIM), lambda b, h, i: (h, 0, 0)),
        ],
        out_specs=pl.BlockSpec((1, tq, hpb * HEAD_DIM), lambda b, h, i: (b, i, h)),
        out_shape=jax.ShapeDtypeStruct((bsz, lp, n_heads * HEAD_DIM), BF16),
        scratch_shapes=[
            pltpu.VMEM((hpb, HEAD_DIM + ONES_ROWS, tq), F32),
            pltpu.VMEM((hpb, tq, tq), F32),
            pltpu.VMEM((hpb, tq, tq), F32),
        ],
        compiler_params=_cparams(3),
        name="flash",
    )(c_tiles, q, k, vt, og_heads)


def _out_proj_kernel(ya_ref, yc_ref, h_ref, wa_ref, wc_ref, o_ref, *, tm, lp, n_pad):
    r = pl.program_id(0)
    out = (jnp.dot(ya_ref[...], wa_ref[...], preferred_element_type=F32)
           + jnp.dot(yc_ref[...], wc_ref[...], preferred_element_type=F32))
    out = h_ref[...] + out
    pos = (r * tm + lax.broadcasted_iota(jnp.int32, (tm, 1), 0)) % lp
    o_ref[...] = jnp.where(pos >= n_pad, out, 0.0)


def _out_proj(ya, yc, h, w_a, w_c, *, lp, n_pad):
    rows, d = h.shape
    da = ya.shape[1]
    tm = 512
    kern = functools.partial(_out_proj_kernel, tm=tm, lp=lp, n_pad=n_pad)
    return pl.pallas_call(
        kern,
        grid=(rows // tm,),
        in_specs=[
            pl.BlockSpec((tm, da), lambda r: (r, 0)),
            pl.BlockSpec((tm, da), lambda r: (r, 0)),
            pl.BlockSpec((tm, d), lambda r: (r, 0)),
            pl.BlockSpec((da, d), lambda r: (0, 0)),
            pl.BlockSpec((da, d), lambda r: (0, 0)),
        ],
        out_specs=pl.BlockSpec((tm, d), lambda r: (r, 0)),
        out_shape=jax.ShapeDtypeStruct(h.shape, F32),
        compiler_params=_cparams(1),
        name="out_proj",
    )(ya, yc, h, w_a, w_c)


def _ffn_kernel(h_ref, g_ref, wg_ref, wu_ref, wd_ref, o_ref, hn_ref):
    f = pl.program_id(1)

    @pl.when(f == 0)
    def _norm():
        x = h_ref[...]
        hn_ref[...] = (x * _rms_scale(x) * g_ref[...]).astype(BF16)
        o_ref[...] = x

    hn = hn_ref[...]
    gate = jnp.dot(hn, wg_ref[...], preferred_element_type=F32)
    up = jnp.dot(hn, wu_ref[...], preferred_element_type=F32)
    act = (jax.nn.silu(gate) * up).astype(BF16)
    o_ref[...] += jnp.dot(act, wd_ref[...], preferred_element_type=F32)


def _ffn(h, gain, w_gate, w_up, w_down):
    rows, d = h.shape
    d_ff = w_gate.shape[1]
    tm = 1024 if rows % 1024 == 0 else 512
    tf = 512 if d_ff % 512 == 0 else 256
    return pl.pallas_call(
        _ffn_kernel,
        grid=(rows // tm, d_ff // tf),
        in_specs=[
            pl.BlockSpec((tm, d), lambda r, f: (r, 0)),
            pl.BlockSpec((1, d), lambda r, f: (0, 0)),
            pl.BlockSpec((d, tf), lambda r, f: (0, f)),
            pl.BlockSpec((d, tf), lambda r, f: (0, f)),
            pl.BlockSpec((tf, d), lambda r, f: (f, 0)),
        ],
        out_specs=pl.BlockSpec((tm, d), lambda r, f: (r, 0)),
        out_shape=jax.ShapeDtypeStruct(h.shape, F32),
        scratch_shapes=[pltpu.VMEM((tm, d), BF16)],
        compiler_params=_cparams(2),
        name="ffn",
    )(h, gain, w_gate, w_up, w_down)


def _final_norm_kernel(h_ref, g_ref, o_ref):
    x = h_ref[0]
    o_ref[0] = x * _rms_scale(x) * g_ref[...]


def _final_norm(h, gain, *, seq):
    bsz, lp, d = h.shape
    tm = SEQ_TILE
    skip = (lp - seq) // tm
    return pl.pallas_call(
        _final_norm_kernel,
        grid=(bsz, seq // tm),
        in_specs=[
            pl.BlockSpec((1, tm, d), lambda b, i: (b, i + skip, 0)),
            pl.BlockSpec((1, d), lambda b, i: (0, 0)),
        ],
        out_specs=pl.BlockSpec((1, tm, d), lambda b, i: (b, i, 0)),
        out_shape=jax.ShapeDtypeStruct((bsz, seq, d), F32),
        compiler_params=_cparams(2),
        name="final_norm",
    )(h, gain)


def kernel(x, meta, norm_mix, w_in, b_f, conv_w, out_gain, w_out, norm_ffn, w_gate, w_up, w_down, final_norm):
    bsz, seq, d = x.shape
    n_meta = meta.shape[0]
    depth = w_in.shape[0]
    d_attn = d // 2
    n_heads = d_attn // HEAD_DIM
    assert n_heads <= HEAD_PAD and seq % SEQ_TILE == 0 and n_meta <= SEQ_TILE
    assert w_in.shape[2] == 6 * d_attn + n_heads
    lp = seq + SEQ_TILE
    n_pad = lp - seq - n_meta
    scale = HEAD_DIM ** -0.5

    m = jnp.broadcast_to(meta.astype(x.dtype)[None], (bsz, n_meta, d))
    h = jnp.concatenate([jnp.zeros((bsz, n_pad, d), x.dtype), m, x], axis=1)

    for l in range(depth):
        w_main = w_in[l, :, :6 * d_attn].astype(BF16)
        wv_t = w_in[l, :, 2 * d_attn:3 * d_attn].T.astype(BF16)
        wf_t = jnp.zeros((HEAD_PAD, d), BF16).at[:n_heads].set(w_in[l, :, 6 * d_attn:].T.astype(BF16))
        bf_col = jnp.zeros((HEAD_PAD, 1), F32).at[:n_heads, 0].set(b_f[l])
        cw = jnp.zeros((HALO, d_attn), F32).at[:CONV_WIDTH].set(conv_w[l])
        og_heads = out_gain[l, :d_attn].reshape(n_heads, 1, HEAD_DIM)
        og_conv = out_gain[l, d_attn:].reshape(1, d_attn)

        q, k, vt, yc, ctile = _in_proj(h, norm_mix[l].reshape(1, d), w_main, wv_t, wf_t, bf_col, cw,
                                       og_conv, n_pad=n_pad, scale=scale)
        ya = _flash(q, k, vt, ctile[:, :, :, 0].reshape(-1), og_heads)

        w_o = w_out[l].astype(BF16)
        h2 = _out_proj(ya.reshape(bsz * lp, d_attn), yc.reshape(bsz * lp, d_attn),
                       h.reshape(bsz * lp, d), w_o[:d_attn], w_o[d_attn:], lp=lp, n_pad=n_pad)
        h2 = _ffn(h2, norm_ffn[l].reshape(1, d), w_gate[l].astype(BF16), w_up[l].astype(BF16),
                  w_down[l].astype(BF16))
        h = h2.reshape(bsz, lp, d)

    return _final_norm(h, final_norm.reshape(1, d), seq=seq)
```

```python
import functools
import math

import jax
import jax.numpy as jnp
from jax import lax
from jax.experimental import pallas as pl
from jax.experimental.pallas import tpu as pltpu

F32 = jnp.float32
BF16 = jnp.bfloat16

HEAD_DIM = 128
AUG_DIM = 2 * HEAD_DIM
N_SPLIT = 3
CONV_WIDTH = 3
EPS = 1e-6
NEG = -1e30
FAKE = 2 * NEG
LOG2E = math.log2(math.e)
SEQ_TILE = 512
HEAD_PAD = 16
ONES_ROWS = 16
LANES = 128
HALO = 8
VMEM_LIMIT = 56 * 1024 * 1024

_NT = (((1,), (1,)), ((), ()))


def _cparams(n_axes):
    return pltpu.CompilerParams(dimension_semantics=("arbitrary",) * n_axes,
                                vmem_limit_bytes=VMEM_LIMIT)


def _rms_scale(x, axis=-1):
    return lax.rsqrt(jnp.mean(x * x, axis=axis, keepdims=True) + EPS)


def _cumsum_lanes(x):
    n = x.shape[1]
    lane = lax.broadcasted_iota(jnp.int32, x.shape, 1)
    shift = 1
    while shift < n:
        x = x + jnp.where(lane >= shift, pltpu.roll(x, shift, axis=1), 0.0)
        shift *= 2
    return x


def _round_bf16(x):
    return x.astype(BF16).astype(F32)


def _in_proj_kernel(h_ref, g_ref, w_ref, wvt_ref, wf_ref, bf_ref, cw_ref, og_ref,
                    q_ref, k_ref, vt_ref, yc_ref, ctile_ref,
                    hn_ref, gb_ref, u_ref, carry_ref, bias_ref,
                    *, tm, n_heads, n_pad, scale):
    i = pl.program_id(1)
    j = pl.program_id(2)

    @pl.when(j == 0)
    def _norm_and_gates():
        x = h_ref[0]
        hn = (x * _rms_scale(x) * g_ref[...]).astype(BF16)
        hn_ref[...] = hn
        logit = lax.dot_general(wf_ref[...], hn, _NT, preferred_element_type=F32)
        log_f = jax.nn.log_sigmoid(logit + bf_ref[...])

        @pl.when(i == 0)
        def _reset():
            carry_ref[...] = jnp.zeros_like(carry_ref)
            u_ref[pl.ds(0, HALO), :] = jnp.zeros((HALO, u_ref.shape[1]), F32)

        r = _cumsum_lanes(log_f)
        ctile_ref[0, 0] = carry_ref[...] * LOG2E
        carry_ref[...] = carry_ref[...] + jnp.broadcast_to(r[:, tm - 1:tm], carry_ref.shape)
        pos = i * tm + lax.broadcasted_iota(jnp.int32, r.shape, 1)
        bias = jnp.where(pos >= n_pad, -LOG2E * r, NEG)
        bias = jnp.concatenate([bias, jnp.zeros((LANES - HEAD_PAD, tm), F32)], axis=0)
        bias_ref[...] = bias.T

    @pl.when(j != 2)
    def _main():
        z = jnp.dot(hn_ref[...], w_ref[...], preferred_element_type=F32)

        @pl.when(j == 0)
        def _q():
            lane = lax.broadcasted_iota(jnp.int32, (tm, HEAD_DIM), 1)
            ones = jnp.where(lane < N_SPLIT, 1.0, 0.0).astype(BF16)
            for h in range(n_heads):
                q_ref[0, :, h * AUG_DIM:h * AUG_DIM + HEAD_DIM] = (
                    z[:, h * HEAD_DIM:(h + 1) * HEAD_DIM] * (scale * LOG2E)).astype(BF16)
                q_ref[0, :, h * AUG_DIM + HEAD_DIM:(h + 1) * AUG_DIM] = ones

        @pl.when(j == 1)
        def _k():
            lane = lax.broadcasted_iota(jnp.int32, (tm, HEAD_DIM), 1)
            for h in range(n_heads):
                k_ref[0, :, h * AUG_DIM:h * AUG_DIM + HEAD_DIM] = (
                    z[:, h * HEAD_DIM:(h + 1) * HEAD_DIM].astype(BF16))
                x = jnp.broadcast_to(bias_ref[:, h:h + 1], (tm, HEAD_DIM))
                hi = _round_bf16(x)
                mid = _round_bf16(x - hi)
                lo = x - hi - mid
                aug = jnp.where(lane == 0, hi, jnp.where(lane == 1, mid,
                                                         jnp.where(lane == 2, lo, 0.0)))
                k_ref[0, :, h * AUG_DIM + HEAD_DIM:(h + 1) * AUG_DIM] = aug.astype(BF16)

        @pl.when(j == 3)
        def _gate_b():
            gb_ref[...] = z

        @pl.when(j == 4)
        def _gate_c():
            u_ref[pl.ds(HALO, tm), :] = z

        @pl.when(j == 5)
        def _conv():
            u = u_ref[pl.ds(HALO, tm), :] * z
            u_ref[pl.ds(HALO, tm), :] = u
            u1 = u_ref[pl.ds(HALO - 1, tm), :]
            u2 = u_ref[pl.ds(HALO - 2, tm), :]
            conv = cw_ref[0:1, :] * u2 + cw_ref[1:2, :] * u1 + cw_ref[2:3, :] * u
            y = gb_ref[...] * conv
            for g in range(n_heads):
                cols = slice(g * HEAD_DIM, (g + 1) * HEAD_DIM)
                yg = y[:, cols]
                yc_ref[0, :, cols] = (yg * _rms_scale(yg) * og_ref[:, cols]).astype(BF16)
            u_ref[pl.ds(0, HALO), :] = u_ref[pl.ds(tm, HALO), :]

    @pl.when(j == 2)
    def _v():
        zt = lax.dot_general(wvt_ref[...], hn_ref[...], _NT, preferred_element_type=F32)
        vt_ref[0] = zt.astype(BF16)


def _in_proj(h, gain, w_main, wv_t, wf_t, bf_col, conv_w, og_conv, *, n_pad, scale):
    bsz, lp, d = h.shape
    tn = w_main.shape[1] // 6
    n_heads = tn // HEAD_DIM
    tm = SEQ_TILE
    n_t = lp // tm
    kern = functools.partial(_in_proj_kernel, tm=tm, n_heads=n_heads, n_pad=n_pad, scale=scale)
    aug_blk = pl.BlockSpec((1, tm, n_heads * AUG_DIM), lambda b, i, j: (b, i, 0))
    return pl.pallas_call(
        kern,
        grid=(bsz, n_t, 6),
        in_specs=[
            pl.BlockSpec((1, tm, d), lambda b, i, j: (b, i, 0)),
            pl.BlockSpec((1, d), lambda b, i, j: (0, 0)),
            pl.BlockSpec((d, tn), lambda b, i, j: (0, j)),
            pl.BlockSpec((tn, d), lambda b, i, j: (0, 0)),
            pl.BlockSpec((HEAD_PAD, d), lambda b, i, j: (0, 0)),
            pl.BlockSpec((HEAD_PAD, 1), lambda b, i, j: (0, 0)),
            pl.BlockSpec((HALO, tn), lambda b, i, j: (0, 0)),
            pl.BlockSpec((1, tn), lambda b, i, j: (0, 0)),
        ],
        out_specs=[
            aug_blk, aug_blk,
            pl.BlockSpec((1, tn, tm), lambda b, i, j: (b, 0, i)),
            pl.BlockSpec((1, tm, tn), lambda b, i, j: (b, i, 0)),
            pl.BlockSpec((1, 1, HEAD_PAD, LANES), lambda b, i, j: (b, i, 0, 0)),
        ],
        out_shape=[
            jax.ShapeDtypeStruct((bsz, lp, n_heads * AUG_DIM), BF16),
            jax.ShapeDtypeStruct((bsz, lp, n_heads * AUG_DIM), BF16),
            jax.ShapeDtypeStruct((bsz, tn, lp), BF16),
            jax.ShapeDtypeStruct((bsz, lp, tn), BF16),
            jax.ShapeDtypeStruct((bsz, n_t, HEAD_PAD, LANES), F32),
        ],
        scratch_shapes=[
            pltpu.VMEM((tm, d), BF16),
            pltpu.VMEM((tm, tn), F32),
            pltpu.VMEM((tm + HALO, tn), F32),
            pltpu.VMEM((HEAD_PAD, LANES), F32),
            pltpu.VMEM((tm, LANES), F32),
        ],
        compiler_params=_cparams(3),
        name="in_proj",
    )(h, gain, w_main, wv_t, wf_t, bf_col, conv_w, og_conv)


def _flash_kernel(c_ref, q_ref, k_ref, vt_ref, g_ref, o_ref, acc_ref, sa_ref, sb_ref,
                  *, tq, n_q, hpb):
    b = pl.program_id(0)
    hp = pl.program_id(1)
    i = pl.program_id(2)
    heads = range(hpb)
    c_base = [(b * n_q) * HEAD_PAD + hp * hpb + hh for hh in heads]
    c_q = [c_ref[c_base[hh] + i * HEAD_PAD] for hh in heads]
    last = jnp.maximum(i - 1, 0)
    key = lax.broadcasted_iota(jnp.int32, (tq, tq), 0)
    qry = lax.broadcasted_iota(jnp.int32, (tq, tq), 1)
    ones_rows = jnp.ones((ONES_ROWS, tq), BF16)

    def scores_into(s_ref, j, masked=False):
        start = pl.multiple_of(j * tq, tq)
        col_max = []
        for hh in heads:
            cols = slice(hh * AUG_DIM, (hh + 1) * AUG_DIM)
            st = lax.dot_general(k_ref[0, pl.ds(start, tq), cols], q_ref[0, :, cols], _NT,
                                 preferred_element_type=F32)
            if masked:
                st = jnp.where(key <= qry, st, NEG)
            s_ref[hh] = st
            col_max.append(jnp.max(st, axis=0, keepdims=True))
        return tuple(col_max)

    def accumulate(s_ref, col_max, j, delta, m_all):
        start = pl.multiple_of(j * tq, tq)
        m_out = []
        for hh in heads:
            vt = vt_ref[0, hh * HEAD_DIM:(hh + 1) * HEAD_DIM, pl.ds(start, tq)]
            m_new = jnp.maximum(m_all[hh], col_max[hh] + delta[hh])
            alpha = jnp.exp2(m_all[hh] - m_new)
            p = jnp.exp2(s_ref[hh] - (m_new - delta[hh])).astype(BF16)
            pv = jnp.dot(jnp.concatenate([vt, ones_rows], axis=0), p,
                         preferred_element_type=F32)
            acc_ref[hh] = alpha * acc_ref[hh] + pv
            m_out.append(m_new)
        return tuple(m_out)

    def delta_of(j):
        return [c_q[hh] - c_ref[c_base[hh] + j * HEAD_PAD] for hh in heads]

    def tail_delta_of(j):
        real = delta_of(jnp.minimum(j, last))
        return [jnp.where(j < i, real[hh], FAKE) for hh in heads]

    acc_ref[...] = jnp.zeros_like(acc_ref)
    max_a = scores_into(sa_ref, jnp.minimum(0, last))
    m_all = tuple(jnp.full((1, tq), NEG, F32) for _ in heads)
    n_trips = jnp.maximum((i - 1) // 2, 0)

    def trip(jj, carry):
        m_all, max_a = carry
        j0 = 2 * jj
        max_b = scores_into(sb_ref, j0 + 1)
        m_all = accumulate(sa_ref, max_a, j0, delta_of(j0), m_all)
        max_a = scores_into(sa_ref, j0 + 2)
        m_all = accumulate(sb_ref, max_b, j0 + 1, delta_of(j0 + 1), m_all)
        return m_all, max_a

    m_all, max_a = lax.fori_loop(0, n_trips, trip, (m_all, max_a))
    j0 = 2 * n_trips
    j1 = jnp.minimum(j0 + 1, last)
    max_b = scores_into(sb_ref, j1)
    m_all = accumulate(sa_ref, max_a, jnp.minimum(j0, last), tail_delta_of(j0), m_all)
    max_d = scores_into(sa_ref, i, masked=True)
    m_all = accumulate(sb_ref, max_b, j1, tail_delta_of(j0 + 1), m_all)
    accumulate(sa_ref, max_d, i, [0.0 for _ in heads], m_all)

    for hh in heads:
        ot = acc_ref[hh, :HEAD_DIM, :] / acc_ref[hh, HEAD_DIM:HEAD_DIM + 1, :]
        ot = ot * _rms_scale(ot, axis=0)
        o_ref[0, :, hh * HEAD_DIM:(hh + 1) * HEAD_DIM] = (ot.T * g_ref[hh]).astype(BF16)


def _flash(q, k, vt, c_tiles, og_heads):
    bsz, lp, _ = q.shape
    n_heads = vt.shape[1] // HEAD_DIM
    hpb = 2 if n_heads % 2 == 0 else 1
    tq = SEQ_TILE
    n_q = lp // tq
    kern = functools.partial(_flash_kernel, tq=tq, n_q=n_q, hpb=hpb)
    resident = pl.Buffered(1)
    return pl.pallas_call(
        kern,
        grid=(bsz, n_heads // hpb, n_q),
        in_specs=[
            pl.BlockSpec(memory_space=pltpu.SMEM),
            pl.BlockSpec((1, tq, hpb * AUG_DIM), lambda b, h, i: (b, i, h)),
            pl.BlockSpec((1, lp, hpb * AUG_DIM), lambda b, h, i: (b, 0, h),
                         pipeline_mode=resident),
            pl.BlockSpec((1, hpb * HEAD_DIM, lp), lambda b, h, i: (b, h, 0),
                         pipeline_mode=resident),
            pl.BlockSpec((hpb, 1, HEAD_DIM), lambda b, h, i: (h, 0, 0)),
        ],
        out_specs=pl.BlockSpec((1, tq, hpb * HEAD_DIM), lambda b, h, i: (b, i, h)),
        out_shape=jax.ShapeDtypeStruct((bsz, lp, n_heads * HEAD_DIM), BF16),
        scratch_shapes=[
            pltpu.VMEM((hpb, HEAD_DIM + ONES_ROWS, tq), F32),
            pltpu.VMEM((hpb, tq, tq), F32),
            pltpu.VMEM((hpb, tq, tq), F32),
        ],
        compiler_params=_cparams(3),
        name="flash",
    )(c_tiles, q, k, vt, og_heads)


def _out_proj_kernel(ya_ref, yc_ref, h_ref, wa_ref, wc_ref, o_ref, *, tm, lp, n_pad):
    r = pl.program_id(0)
    out = (jnp.dot(ya_ref[...], wa_ref[...], preferred_element_type=F32)
           + jnp.dot(yc_ref[...], wc_ref[...], preferred_element_type=F32))
    out = h_ref[...] + out
    pos = (r * tm + lax.broadcasted_iota(jnp.int32, (tm, 1), 0)) % lp
    o_ref[...] = jnp.where(pos >= n_pad, out, 0.0)


def _out_proj(ya, yc, h, w_a, w_c, *, lp, n_pad):
    rows, d = h.shape
    da = ya.shape[1]
    tm = 512
    kern = functools.partial(_out_proj_kernel, tm=tm, lp=lp, n_pad=n_pad)
    return pl.pallas_call(
        kern,
        grid=(rows // tm,),
        in_specs=[
            pl.BlockSpec((tm, da), lambda r: (r, 0)),
            pl.BlockSpec((tm, da), lambda r: (r, 0)),
            pl.BlockSpec((tm, d), lambda r: (r, 0)),
            pl.BlockSpec((da, d), lambda r: (0, 0)),
            pl.BlockSpec((da, d), lambda r: (0, 0)),
        ],
        out_specs=pl.BlockSpec((tm, d), lambda r: (r, 0)),
        out_shape=jax.ShapeDtypeStruct(h.shape, F32),
        compiler_params=_cparams(1),
        name="out_proj",
    )(ya, yc, h, w_a, w_c)


def _ffn_kernel(h_ref, g_ref, wg_ref, wu_ref, wd_ref, o_ref, hn_ref):
    f = pl.program_id(1)

    @pl.when(f == 0)
    def _norm():
        x = h_ref[...]
        hn_ref[...] = (x * _rms_scale(x) * g_ref[...]).astype(BF16)
        o_ref[...] = x

    hn = hn_ref[...]
    gate = jnp.dot(hn, wg_ref[...], preferred_element_type=F32)
    up = jnp.dot(hn, wu_ref[...], preferred_element_type=F32)
    act = (jax.nn.silu(gate) * up).astype(BF16)
    o_ref[...] += jnp.dot(act, wd_ref[...], preferred_element_type=F32)


def _ffn(h, gain, w_gate, w_up, w_down):
    rows, d = h.shape
    d_ff = w_gate.shape[1]
    tm = 1024 if rows % 1024 == 0 else 512
    tf = 512 if d_ff % 512 == 0 else 256
    return pl.pallas_call(
        _ffn_kernel,
        grid=(rows // tm, d_ff // tf),
        in_specs=[
            pl.BlockSpec((tm, d), lambda r, f: (r, 0)),
            pl.BlockSpec((1, d), lambda r, f: (0, 0)),
            pl.BlockSpec((d, tf), lambda r, f: (0, f)),
            pl.BlockSpec((d, tf), lambda r, f: (0, f)),
            pl.BlockSpec((tf, d), lambda r, f: (f, 0)),
        ],
        out_specs=pl.BlockSpec((tm, d), lambda r, f: (r, 0)),
        out_shape=jax.ShapeDtypeStruct(h.shape, F32),
        scratch_shapes=[pltpu.VMEM((tm, d), BF16)],
        compiler_params=_cparams(2),
        name="ffn",
    )(h, gain, w_gate, w_up, w_down)


def _final_norm_kernel(h_ref, g_ref, o_ref):
    x = h_ref[0]
    o_ref[0] = x * _rms_scale(x) * g_ref[...]


def _final_norm(h, gain, *, seq):
    bsz, lp, d = h.shape
    tm = SEQ_TILE
    skip = (lp - seq) // tm
    return pl.pallas_call(
        _final_norm_kernel,
        grid=(bsz, seq // tm),
        in_specs=[
            pl.BlockSpec((1, tm, d), lambda b, i: (b, i + skip, 0)),
            pl.BlockSpec((1, d), lambda b, i: (0, 0)),
        ],
        out_specs=pl.BlockSpec((1, tm, d), lambda b, i: (b, i, 0)),
        out_shape=jax.ShapeDtypeStruct((bsz, seq, d), F32),
        compiler_params=_cparams(2),
        name="final_norm",
    )(h, gain)


def kernel(x, meta, norm_mix, w_in, b_f, conv_w, out_gain, w_out, norm_ffn, w_gate, w_up, w_down, final_norm):
    bsz, seq, d = x.shape
    n_meta = meta.shape[0]
    depth = w_in.shape[0]
    d_attn = d // 2
    n_heads = d_attn // HEAD_DIM
    assert n_heads <= HEAD_PAD and seq % SEQ_TILE == 0 and n_meta <= SEQ_TILE
    assert w_in.shape[2] == 6 * d_attn + n_heads
    lp = seq + SEQ_TILE
    n_pad = lp - seq - n_meta
    scale = HEAD_DIM ** -0.5

    m = jnp.broadcast_to(meta.astype(x.dtype)[None], (bsz, n_meta, d))
    h = jnp.concatenate([jnp.zeros((bsz, n_pad, d), x.dtype), m, x], axis=1)

    for l in range(depth):
        w_main = w_in[l, :, :6 * d_attn].astype(BF16)
        wv_t = w_in[l, :, 2 * d_attn:3 * d_attn].T.astype(BF16)
        wf_t = jnp.zeros((HEAD_PAD, d), BF16).at[:n_heads].set(w_in[l, :, 6 * d_attn:].T.astype(BF16))
        bf_col = jnp.zeros((HEAD_PAD, 1), F32).at[:n_heads, 0].set(b_f[l])
        cw = jnp.zeros((HALO, d_attn), F32).at[:CONV_WIDTH].set(conv_w[l])
        og_heads = out_gain[l, :d_attn].reshape(n_heads, 1, HEAD_DIM)
        og_conv = out_gain[l, d_attn:].reshape(1, d_attn)

        q, k, vt, yc, ctile = _in_proj(h, norm_mix[l].reshape(1, d), w_main, wv_t, wf_t, bf_col, cw,
                                       og_conv, n_pad=n_pad, scale=scale)
        ya = _flash(q, k, vt, ctile[:, :, :, 0].reshape(-1), og_heads)

        w_o = w_out[l].astype(BF16)
        h2 = _out_proj(ya.reshape(bsz * lp, d_attn), yc.reshape(bsz * lp, d_attn),
                       h.reshape(bsz * lp, d), w_o[:d_attn], w_o[d_attn:], lp=lp, n_pad=n_pad)
        h2 = _ffn(h2, norm_ffn[l].reshape(1, d), w_gate[l].astype(BF16), w_up[l].astype(BF16),
                  w_down[l].astype(BF16))
        h = h2.reshape(bsz, lp, d)

    return _final_norm(h, final_norm.reshape(1, d), seq=seq)
```

```python
import functools
import math

import jax
import jax.numpy as jnp
from jax import lax
from jax.experimental import pallas as pl
from jax.experimental.pallas import tpu as pltpu

F32 = jnp.float32
BF16 = jnp.bfloat16

HEAD_DIM = 128
AUG_DIM = 2 * HEAD_DIM
N_SPLIT = 3
CONV_WIDTH = 3
EPS = 1e-6
NEG = -1e30
FAKE = 2 * NEG
LOG2E = math.log2(math.e)
SEQ_TILE = 512
HEAD_PAD = 16
ONES_ROWS = 16
LANES = 128
STAT_C, STAT_BMAX, STAT_QN, STAT_KN, N_STAT = 0, 1, 2, 3, 4
SKIP_LOG2 = 140.0
HALO = 8
VMEM_LIMIT = 56 * 1024 * 1024

_NT = (((1,), (1,)), ((), ()))


def _cparams(n_axes):
    return pltpu.CompilerParams(dimension_semantics=("arbitrary",) * n_axes,
                                vmem_limit_bytes=VMEM_LIMIT)


def _rms_scale(x, axis=-1):
    return lax.rsqrt(jnp.mean(x * x, axis=axis, keepdims=True) + EPS)


def _cumsum_lanes(x):
    n = x.shape[1]
    lane = lax.broadcasted_iota(jnp.int32, x.shape, 1)
    shift = 1
    while shift < n:
        x = x + jnp.where(lane >= shift, pltpu.roll(x, shift, axis=1), 0.0)
        shift *= 2
    return x


def _round_bf16(x):
    return x.astype(BF16).astype(F32)


def _max_row_norm(x):
    xf = x.astype(F32)
    n2 = jnp.max(jnp.sum(xf * xf, axis=1, keepdims=True), axis=0, keepdims=True)
    return jnp.broadcast_to(jnp.sqrt(n2), (1, LANES))


def _in_proj_kernel(h_ref, g_ref, w_ref, wvt_ref, wf_ref, bf_ref, cw_ref, og_ref,
                    q_ref, k_ref, vt_ref, yc_ref, stat_ref,
                    hn_ref, gb_ref, u_ref, carry_ref, bias_ref,
                    *, tm, n_heads, n_pad, scale):
    i = pl.program_id(1)
    j = pl.program_id(2)

    @pl.when(j == 0)
    def _norm_and_gates():
        x = h_ref[0]
        hn = (x * _rms_scale(x) * g_ref[...]).astype(BF16)
        hn_ref[...] = hn
        logit = lax.dot_general(wf_ref[...], hn, _NT, preferred_element_type=F32)
        log_f = jax.nn.log_sigmoid(logit + bf_ref[...])

        @pl.when(i == 0)
        def _reset():
            carry_ref[...] = jnp.zeros_like(carry_ref)
            u_ref[pl.ds(0, HALO), :] = jnp.zeros((HALO, u_ref.shape[1]), F32)

        r = _cumsum_lanes(log_f)
        stat_ref[0, 0] = jnp.zeros(stat_ref.shape[2:], F32)
        stat_ref[0, 0, STAT_C] = carry_ref[...] * LOG2E
        carry_ref[...] = carry_ref[...] + jnp.broadcast_to(r[:, tm - 1:tm], carry_ref.shape)
        pos = i * tm + lax.broadcasted_iota(jnp.int32, r.shape, 1)
        bias = jnp.where(pos >= n_pad, -LOG2E * r, NEG)
        stat_ref[0, 0, STAT_BMAX] = jnp.broadcast_to(jnp.max(bias, axis=1, keepdims=True),
                                                     (HEAD_PAD, LANES))
        bias = jnp.concatenate([bias, jnp.zeros((LANES - HEAD_PAD, tm), F32)], axis=0)
        bias_ref[...] = bias.T

    @pl.when(j != 2)
    def _main():
        z = jnp.dot(hn_ref[...], w_ref[...], preferred_element_type=F32)

        @pl.when(j == 0)
        def _q():
            lane = lax.broadcasted_iota(jnp.int32, (tm, HEAD_DIM), 1)
            ones = jnp.where(lane < N_SPLIT, 1.0, 0.0).astype(BF16)
            for h in range(n_heads):
                qh = (z[:, h * HEAD_DIM:(h + 1) * HEAD_DIM] * (scale * LOG2E)).astype(BF16)
                q_ref[0, :, h * AUG_DIM:h * AUG_DIM + HEAD_DIM] = qh
                q_ref[0, :, h * AUG_DIM + HEAD_DIM:(h + 1) * AUG_DIM] = ones
                stat_ref[0, 0, STAT_QN, h:h + 1, :] = _max_row_norm(qh)

        @pl.when(j == 1)
        def _k():
            lane = lax.broadcasted_iota(jnp.int32, (tm, HEAD_DIM), 1)
            for h in range(n_heads):
                kh = z[:, h * HEAD_DIM:(h + 1) * HEAD_DIM].astype(BF16)
                k_ref[0, :, h * AUG_DIM:h * AUG_DIM + HEAD_DIM] = kh
                stat_ref[0, 0, STAT_KN, h:h + 1, :] = _max_row_norm(kh)
                x = jnp.broadcast_to(bias_ref[:, h:h + 1], (tm, HEAD_DIM))
                hi = _round_bf16(x)
                mid = _round_bf16(x - hi)
                lo = x - hi - mid
                aug = jnp.where(lane == 0, hi, jnp.where(lane == 1, mid,
                                                         jnp.where(lane == 2, lo, 0.0)))
                k_ref[0, :, h * AUG_DIM + HEAD_DIM:(h + 1) * AUG_DIM] = aug.astype(BF16)

        @pl.when(j == 3)
        def _gate_b():
            gb_ref[...] = z

        @pl.when(j == 4)
        def _gate_c():
            u_ref[pl.ds(HALO, tm), :] = z

        @pl.when(j == 5)
        def _conv():
            u = u_ref[pl.ds(HALO, tm), :] * z
            u_ref[pl.ds(HALO, tm), :] = u
            u1 = u_ref[pl.ds(HALO - 1, tm), :]
            u2 = u_ref[pl.ds(HALO - 2, tm), :]
            conv = cw_ref[0:1, :] * u2 + cw_ref[1:2, :] * u1 + cw_ref[2:3, :] * u
            y = gb_ref[...] * conv
            for g in range(n_heads):
                cols = slice(g * HEAD_DIM, (g + 1) * HEAD_DIM)
                yg = y[:, cols]
                yc_ref[0, :, cols] = (yg * _rms_scale(yg) * og_ref[:, cols]).astype(BF16)
            u_ref[pl.ds(0, HALO), :] = u_ref[pl.ds(tm, HALO), :]

    @pl.when(j == 2)
    def _v():
        zt = lax.dot_general(wvt_ref[...], hn_ref[...], _NT, preferred_element_type=F32)
        vt_ref[0] = zt.astype(BF16)


def _in_proj(h, gain, w_main, wv_t, wf_t, bf_col, conv_w, og_conv, *, n_pad, scale):
    bsz, lp, d = h.shape
    tn = w_main.shape[1] // 6
    n_heads = tn // HEAD_DIM
    tm = SEQ_TILE
    n_t = lp // tm
    kern = functools.partial(_in_proj_kernel, tm=tm, n_heads=n_heads, n_pad=n_pad, scale=scale)
    aug_blk = pl.BlockSpec((1, tm, n_heads * AUG_DIM), lambda b, i, j: (b, i, 0))
    return pl.pallas_call(
        kern,
        grid=(bsz, n_t, 6),
        in_specs=[
            pl.BlockSpec((1, tm, d), lambda b, i, j: (b, i, 0)),
            pl.BlockSpec((1, d), lambda b, i, j: (0, 0)),
            pl.BlockSpec((d, tn), lambda b, i, j: (0, j)),
            pl.BlockSpec((tn, d), lambda b, i, j: (0, 0)),
            pl.BlockSpec((HEAD_PAD, d), lambda b, i, j: (0, 0)),
            pl.BlockSpec((HEAD_PAD, 1), lambda b, i, j: (0, 0)),
            pl.BlockSpec((HALO, tn), lambda b, i, j: (0, 0)),
            pl.BlockSpec((1, tn), lambda b, i, j: (0, 0)),
        ],
        out_specs=[
            aug_blk, aug_blk,
            pl.BlockSpec((1, tn, tm), lambda b, i, j: (b, 0, i)),
            pl.BlockSpec((1, tm, tn), lambda b, i, j: (b, i, 0)),
            pl.BlockSpec((1, 1, N_STAT, HEAD_PAD, LANES), lambda b, i, j: (b, i, 0, 0, 0)),
        ],
        out_shape=[
            jax.ShapeDtypeStruct((bsz, lp, n_heads * AUG_DIM), BF16),
            jax.ShapeDtypeStruct((bsz, lp, n_heads * AUG_DIM), BF16),
            jax.ShapeDtypeStruct((bsz, tn, lp), BF16),
            jax.ShapeDtypeStruct((bsz, lp, tn), BF16),
            jax.ShapeDtypeStruct((bsz, n_t, N_STAT, HEAD_PAD, LANES), F32),
        ],
        scratch_shapes=[
            pltpu.VMEM((tm, d), BF16),
            pltpu.VMEM((tm, tn), F32),
            pltpu.VMEM((tm + HALO, tn), F32),
            pltpu.VMEM((HEAD_PAD, LANES), F32),
            pltpu.VMEM((tm, LANES), F32),
        ],
        compiler_params=_cparams(3),
        name="in_proj",
    )(h, gain, w_main, wv_t, wf_t, bf_col, conv_w, og_conv)


def _flash_kernel(c_ref, first_ref, q_ref, k_ref, vt_ref, g_ref, o_ref, acc_ref, sa_ref, sb_ref,
                  *, tq, n_q, hpb):
    b = pl.program_id(0)
    hp = pl.program_id(1)
    i = pl.program_id(2)
    heads = range(hpb)
    c_base = [(b * n_q) * HEAD_PAD + hp * hpb + hh for hh in heads]
    c_q = [c_ref[c_base[hh] + i * HEAD_PAD] for hh in heads]
    first = first_ref[(b * pl.num_programs(1) + hp) * n_q + i]
    n_unmasked = i - first
    last = first + jnp.maximum(n_unmasked - 1, 0)
    key = lax.broadcasted_iota(jnp.int32, (tq, tq), 0)
    qry = lax.broadcasted_iota(jnp.int32, (tq, tq), 1)
    ones_rows = jnp.ones((ONES_ROWS, tq), BF16)

    def scores_into(s_ref, j, masked=False):
        start = pl.multiple_of(j * tq, tq)
        col_max = []
        for hh in heads:
            cols = slice(hh * AUG_DIM, (hh + 1) * AUG_DIM)
            st = lax.dot_general(k_ref[0, pl.ds(start, tq), cols], q_ref[0, :, cols], _NT,
                                 preferred_element_type=F32)
            if masked:
                st = jnp.where(key <= qry, st, NEG)
            s_ref[hh] = st
            col_max.append(jnp.max(st, axis=0, keepdims=True))
        return tuple(col_max)

    def accumulate(s_ref, col_max, j, delta, m_all):
        start = pl.multiple_of(j * tq, tq)
        m_out = []
        for hh in heads:
            vt = vt_ref[0, hh * HEAD_DIM:(hh + 1) * HEAD_DIM, pl.ds(start, tq)]
            m_new = jnp.maximum(m_all[hh], col_max[hh] + delta[hh])
            alpha = jnp.exp2(m_all[hh] - m_new)
            p = jnp.exp2(s_ref[hh] - (m_new - delta[hh])).astype(BF16)
            pv = jnp.dot(jnp.concatenate([vt, ones_rows], axis=0), p,
                         preferred_element_type=F32)
            acc_ref[hh] = alpha * acc_ref[hh] + pv
            m_out.append(m_new)
        return tuple(m_out)

    def delta_of(j):
        return [c_q[hh] - c_ref[c_base[hh] + j * HEAD_PAD] for hh in heads]

    def tail_delta_of(j):
        real = delta_of(jnp.minimum(j, last))
        return [jnp.where(j < i, real[hh], FAKE) for hh in heads]

    acc_ref[...] = jnp.zeros_like(acc_ref)
    max_a = scores_into(sa_ref, first)
    m_all = tuple(jnp.full((1, tq), NEG, F32) for _ in heads)
    n_trips = jnp.maximum((n_unmasked - 1) // 2, 0)

    def trip(jj, carry):
        m_all, max_a = carry
        j0 = first + 2 * jj
        max_b = scores_into(sb_ref, j0 + 1)
        m_all = accumulate(sa_ref, max_a, j0, delta_of(j0), m_all)
        max_a = scores_into(sa_ref, j0 + 2)
        m_all = accumulate(sb_ref, max_b, j0 + 1, delta_of(j0 + 1), m_all)
        return m_all, max_a

    m_all, max_a = lax.fori_loop(0, n_trips, trip, (m_all, max_a))
    j0 = first + 2 * n_trips
    j1 = jnp.minimum(j0 + 1, last)
    max_b = scores_into(sb_ref, j1)
    m_all = accumulate(sa_ref, max_a, jnp.minimum(j0, last), tail_delta_of(j0), m_all)
    max_d = scores_into(sa_ref, i, masked=True)
    m_all = accumulate(sb_ref, max_b, j1, tail_delta_of(j0 + 1), m_all)
    accumulate(sa_ref, max_d, i, [0.0 for _ in heads], m_all)

    for hh in heads:
        ot = acc_ref[hh, :HEAD_DIM, :] / acc_ref[hh, HEAD_DIM:HEAD_DIM + 1, :]
        ot = ot * _rms_scale(ot, axis=0)
        o_ref[0, :, hh * HEAD_DIM:(hh + 1) * HEAD_DIM] = (ot.T * g_ref[hh]).astype(BF16)


def _heads_per_block(n_heads):
    return 2 if n_heads % 2 == 0 else 1


def _first_key_tile(stats, n_heads):
    hpb = _heads_per_block(n_heads)
    st = stats[:, :, :, :n_heads, 0]
    c, bmax, qn, kn = (jnp.moveaxis(st[:, :, s], 1, 2) for s in
                       (STAT_C, STAT_BMAX, STAT_QN, STAT_KN))
    upper = (qn[..., :, None] * (kn[..., None, :] + kn[..., :, None]) * 1.001 + 1.0
             + (c[..., :, None] - c[..., None, :]) + bmax[..., None, :])
    n_t = c.shape[-1]
    earlier = jnp.arange(n_t)[None, :] < jnp.arange(n_t)[:, None]
    negligible = (upper < -SKIP_LOG2) & earlier
    first = jnp.sum(jnp.cumprod(negligible.astype(jnp.int32), axis=-1), axis=-1)
    first = first.reshape(first.shape[0], n_heads // hpb, hpb, n_t).min(axis=2)
    return first.astype(jnp.int32).reshape(-1)


def _flash(q, k, vt, c_tiles, first_tile, og_heads):
    bsz, lp, _ = q.shape
    n_heads = vt.shape[1] // HEAD_DIM
    hpb = _heads_per_block(n_heads)
    tq = SEQ_TILE
    n_q = lp // tq
    kern = functools.partial(_flash_kernel, tq=tq, n_q=n_q, hpb=hpb)
    resident = pl.Buffered(1)
    return pl.pallas_call(
        kern,
        grid=(bsz, n_heads // hpb, n_q),
        in_specs=[
            pl.BlockSpec(memory_space=pltpu.SMEM),
            pl.BlockSpec(memory_space=pltpu.SMEM),
            pl.BlockSpec((1, tq, hpb * AUG_DIM), lambda b, h, i: (b, i, h)),
            pl.BlockSpec((1, lp, hpb * AUG_DIM), lambda b, h, i: (b, 0, h),
                         pipeline_mode=resident),
            pl.BlockSpec((1, hpb * HEAD_DIM, lp), lambda b, h, i: (b, h, 0),
                         pipeline_mode=resident),
            pl.BlockSpec((hpb, 1, HEAD_DIM), lambda b, h, i: (h, 0, 0)),
        ],
        out_specs=pl.BlockSpec((1, tq, hpb * HEAD_DIM), lambda b, h, i: (b, i, h)),
        out_shape=jax.ShapeDtypeStruct((bsz, lp, n_heads * HEAD_DIM), BF16),
        scratch_shapes=[
            pltpu.VMEM((hpb, HEAD_DIM + ONES_ROWS, tq), F32),
            pltpu.VMEM((hpb, tq, tq), F32),
            pltpu.VMEM((hpb, tq, tq), F32),
        ],
        compiler_params=_cparams(3),
        name="flash",
    )(c_tiles, first_tile, q, k, vt, og_heads)


def _out_proj_kernel(ya_ref, yc_ref, h_ref, wa_ref, wc_ref, o_ref, *, tm, lp, n_pad):
    r = pl.program_id(0)
    out = (jnp.dot(ya_ref[...], wa_ref[...], preferred_element_type=F32)
           + jnp.dot(yc_ref[...], wc_ref[...], preferred_element_type=F32))
    out = h_ref[...] + out
    pos = (r * tm + lax.broadcasted_iota(jnp.int32, (tm, 1), 0)) % lp
    o_ref[...] = jnp.where(pos >= n_pad, out, 0.0)


def _out_proj(ya, yc, h, w_a, w_c, *, lp, n_pad):
    rows, d = h.shape
    da = ya.shape[1]
    tm = 512
    kern = functools.partial(_out_proj_kernel, tm=tm, lp=lp, n_pad=n_pad)
    return pl.pallas_call(
        kern,
        grid=(rows // tm,),
        in_specs=[
            pl.BlockSpec((tm, da), lambda r: (r, 0)),
            pl.BlockSpec((tm, da), lambda r: (r, 0)),
            pl.BlockSpec((tm, d), lambda r: (r, 0)),
            pl.BlockSpec((da, d), lambda r: (0, 0)),
            pl.BlockSpec((da, d), lambda r: (0, 0)),
        ],
        out_specs=pl.BlockSpec((tm, d), lambda r: (r, 0)),
        out_shape=jax.ShapeDtypeStruct(h.shape, F32),
        compiler_params=_cparams(1),
        name="out_proj",
    )(ya, yc, h, w_a, w_c)


def _ffn_kernel(h_ref, g_ref, wg_ref, wu_ref, wd_ref, o_ref, hn_ref):
    f = pl.program_id(1)

    @pl.when(f == 0)
    def _norm():
        x = h_ref[...]
        hn_ref[...] = (x * _rms_scale(x) * g_ref[...]).astype(BF16)
        o_ref[...] = x

    hn = hn_ref[...]
    gate = jnp.dot(hn, wg_ref[...], preferred_element_type=F32)
    up = jnp.dot(hn, wu_ref[...], preferred_element_type=F32)
    act = (jax.nn.silu(gate) * up).astype(BF16)
    o_ref[...] += jnp.dot(act, wd_ref[...], preferred_element_type=F32)


def _ffn(h, gain, w_gate, w_up, w_down):
    rows, d = h.shape
    d_ff = w_gate.shape[1]
    tm = 1024 if rows % 1024 == 0 else 512
    tf = 512 if d_ff % 512 == 0 else 256
    return pl.pallas_call(
        _ffn_kernel,
        grid=(rows // tm, d_ff // tf),
        in_specs=[
            pl.BlockSpec((tm, d), lambda r, f: (r, 0)),
            pl.BlockSpec((1, d), lambda r, f: (0, 0)),
            pl.BlockSpec((d, tf), lambda r, f: (0, f)),
            pl.BlockSpec((d, tf), lambda r, f: (0, f)),
            pl.BlockSpec((tf, d), lambda r, f: (f, 0)),
        ],
        out_specs=pl.BlockSpec((tm, d), lambda r, f: (r, 0)),
        out_shape=jax.ShapeDtypeStruct(h.shape, F32),
        scratch_shapes=[pltpu.VMEM((tm, d), BF16)],
        compiler_params=_cparams(2),
        name="ffn",
    )(h, gain, w_gate, w_up, w_down)


def _final_norm_kernel(h_ref, g_ref, o_ref):
    x = h_ref[0]
    o_ref[0] = x * _rms_scale(x) * g_ref[...]


def _final_norm(h, gain, *, seq):
    bsz, lp, d = h.shape
    tm = SEQ_TILE
    skip = (lp - seq) // tm
    return pl.pallas_call(
        _final_norm_kernel,
        grid=(bsz, seq // tm),
        in_specs=[
            pl.BlockSpec((1, tm, d), lambda b, i: (b, i + skip, 0)),
            pl.BlockSpec((1, d), lambda b, i: (0, 0)),
        ],
        out_specs=pl.BlockSpec((1, tm, d), lambda b, i: (b, i, 0)),
        out_shape=jax.ShapeDtypeStruct((bsz, seq, d), F32),
        compiler_params=_cparams(2),
        name="final_norm",
    )(h, gain)


def kernel(x, meta, norm_mix, w_in, b_f, conv_w, out_gain, w_out, norm_ffn, w_gate, w_up, w_down, final_norm):
    bsz, seq, d = x.shape
    n_meta = meta.shape[0]
    depth = w_in.shape[0]
    d_attn = d // 2
    n_heads = d_attn // HEAD_DIM
    assert n_heads <= HEAD_PAD and seq % SEQ_TILE == 0 and n_meta <= SEQ_TILE
    assert w_in.shape[2] == 6 * d_attn + n_heads
    lp = seq + SEQ_TILE
    n_pad = lp - seq - n_meta
    scale = HEAD_DIM ** -0.5

    m = jnp.broadcast_to(meta.astype(x.dtype)[None], (bsz, n_meta, d))
    h = jnp.concatenate([jnp.zeros((bsz, n_pad, d), x.dtype), m, x], axis=1)

    for l in range(depth):
        w_main = w_in[l, :, :6 * d_attn].astype(BF16)
        wv_t = w_in[l, :, 2 * d_attn:3 * d_attn].T.astype(BF16)
        wf_t = jnp.zeros((HEAD_PAD, d), BF16).at[:n_heads].set(w_in[l, :, 6 * d_attn:].T.astype(BF16))
        bf_col = jnp.zeros((HEAD_PAD, 1), F32).at[:n_heads, 0].set(b_f[l])
        cw = jnp.zeros((HALO, d_attn), F32).at[:CONV_WIDTH].set(conv_w[l])
        og_heads = out_gain[l, :d_attn].reshape(n_heads, 1, HEAD_DIM)
        og_conv = out_gain[l, d_attn:].reshape(1, d_attn)

        q, k, vt, yc, stats = _in_proj(h, norm_mix[l].reshape(1, d), w_main, wv_t, wf_t, bf_col, cw,
                                       og_conv, n_pad=n_pad, scale=scale)
        ya = _flash(q, k, vt, stats[:, :, STAT_C, :, 0].reshape(-1),
                    _first_key_tile(stats, n_heads), og_heads)

        w_o = w_out[l].astype(BF16)
        h2 = _out_proj(ya.reshape(bsz * lp, d_attn), yc.reshape(bsz * lp, d_attn),
                       h.reshape(bsz * lp, d), w_o[:d_attn], w_o[d_attn:], lp=lp, n_pad=n_pad)
        h2 = _ffn(h2, norm_ffn[l].reshape(1, d), w_gate[l].astype(BF16), w_up[l].astype(BF16),
                  w_down[l].astype(BF16))
        h = h2.reshape(bsz, lp, d)

    return _final_norm(h, final_norm.reshape(1, d), seq=seq)
```

```python
import functools
import math

import jax
import jax.numpy as jnp
from jax import lax
from jax.experimental import pallas as pl
from jax.experimental.pallas import tpu as pltpu

F32 = jnp.float32
BF16 = jnp.bfloat16

HEAD_DIM = 128
AUG_DIM = 2 * HEAD_DIM
N_SPLIT = 3
CONV_WIDTH = 3
EPS = 1e-6
NEG = -1e30
FAKE = 2 * NEG
LOG2E = math.log2(math.e)
SEQ_TILE = 512
HEAD_PAD = 16
ONES_ROWS = 16
LANES = 128
STAT_C, STAT_BMAX, STAT_QN, STAT_KN, N_STAT = 0, 1, 2, 3, 4
SKIP_LOG2 = 140.0
HALO = 8
VMEM_LIMIT = 56 * 1024 * 1024

_NT = (((1,), (1,)), ((), ()))


def _cparams(n_axes):
    return pltpu.CompilerParams(dimension_semantics=("arbitrary",) * n_axes,
                                vmem_limit_bytes=VMEM_LIMIT)


def _rms_scale(x, axis=-1):
    return lax.rsqrt(jnp.mean(x * x, axis=axis, keepdims=True) + EPS)


def _cumsum_lanes(x):
    n = x.shape[1]
    lane = lax.broadcasted_iota(jnp.int32, x.shape, 1)
    shift = 1
    while shift < n:
        x = x + jnp.where(lane >= shift, pltpu.roll(x, shift, axis=1), 0.0)
        shift *= 2
    return x


def _round_bf16(x):
    return x.astype(BF16).astype(F32)


def _max_row_norm(x):
    xf = x.astype(F32)
    n2 = jnp.max(jnp.sum(xf * xf, axis=1, keepdims=True), axis=0, keepdims=True)
    return jnp.broadcast_to(jnp.sqrt(n2), (1, LANES))


def _attn_proj_kernel(h_ref, g_ref, wqk_ref, wvt_ref, wf_ref, bf_ref,
                      q_ref, k_ref, vt_ref, stat_ref, carry_ref,
                      *, tm, n_heads, n_pad, scale):
    i = pl.program_id(1)
    d_attn = n_heads * HEAD_DIM
    x = h_ref[0]
    hn = (x * _rms_scale(x) * g_ref[...]).astype(BF16)

    @pl.when(i == 0)
    def _reset():
        carry_ref[...] = jnp.zeros_like(carry_ref)

    logit = lax.dot_general(wf_ref[...], hn, _NT, preferred_element_type=F32)
    log_f = jax.nn.log_sigmoid(logit + bf_ref[...])
    r = _cumsum_lanes(log_f)
    stat_ref[0, 0] = jnp.zeros(stat_ref.shape[2:], F32)
    stat_ref[0, 0, STAT_C] = carry_ref[...] * LOG2E
    carry_ref[...] = carry_ref[...] + jnp.broadcast_to(r[:, tm - 1:tm], carry_ref.shape)
    pos = i * tm + lax.broadcasted_iota(jnp.int32, r.shape, 1)
    bias = jnp.where(pos >= n_pad, -LOG2E * r, NEG)
    stat_ref[0, 0, STAT_BMAX] = jnp.broadcast_to(jnp.max(bias, axis=1, keepdims=True),
                                                 (HEAD_PAD, LANES))
    bias_col = jnp.concatenate([bias, jnp.zeros((LANES - HEAD_PAD, tm), F32)], axis=0).T

    lane = lax.broadcasted_iota(jnp.int32, (tm, HEAD_DIM), 1)
    ones = jnp.where(lane < N_SPLIT, 1.0, 0.0).astype(BF16)
    zq = jnp.dot(hn, wqk_ref[:, :d_attn], preferred_element_type=F32)
    for h in range(n_heads):
        qh = (zq[:, h * HEAD_DIM:(h + 1) * HEAD_DIM] * (scale * LOG2E)).astype(BF16)
        q_ref[0, :, h * AUG_DIM:h * AUG_DIM + HEAD_DIM] = qh
        q_ref[0, :, h * AUG_DIM + HEAD_DIM:(h + 1) * AUG_DIM] = ones
        stat_ref[0, 0, STAT_QN, h:h + 1, :] = _max_row_norm(qh)

    zk = jnp.dot(hn, wqk_ref[:, d_attn:], preferred_element_type=F32)
    for h in range(n_heads):
        kh = zk[:, h * HEAD_DIM:(h + 1) * HEAD_DIM].astype(BF16)
        k_ref[0, :, h * AUG_DIM:h * AUG_DIM + HEAD_DIM] = kh
        stat_ref[0, 0, STAT_KN, h:h + 1, :] = _max_row_norm(kh)
        xb = jnp.broadcast_to(bias_col[:, h:h + 1], (tm, HEAD_DIM))
        hi = _round_bf16(xb)
        mid = _round_bf16(xb - hi)
        lo = xb - hi - mid
        aug = jnp.where(lane == 0, hi, jnp.where(lane == 1, mid, jnp.where(lane == 2, lo, 0.0)))
        k_ref[0, :, h * AUG_DIM + HEAD_DIM:(h + 1) * AUG_DIM] = aug.astype(BF16)

    zvt = lax.dot_general(wvt_ref[...], hn, _NT, preferred_element_type=F32)
    vt_ref[0] = zvt.astype(BF16)


def _attn_proj(h, gain, w_qk, wv_t, wf_t, bf_col, *, n_pad, scale):
    bsz, lp, d = h.shape
    d_attn = wv_t.shape[0]
    n_heads = d_attn // HEAD_DIM
    tm = SEQ_TILE
    n_t = lp // tm
    kern = functools.partial(_attn_proj_kernel, tm=tm, n_heads=n_heads, n_pad=n_pad, scale=scale)
    aug_blk = pl.BlockSpec((1, tm, n_heads * AUG_DIM), lambda b, i: (b, i, 0))
    const = dict(pipeline_mode=pl.Buffered(1))
    return pl.pallas_call(
        kern,
        grid=(bsz, n_t),
        in_specs=[
            pl.BlockSpec((1, tm, d), lambda b, i: (b, i, 0)),
            pl.BlockSpec((1, d), lambda b, i: (0, 0)),
            pl.BlockSpec((d, 2 * d_attn), lambda b, i: (0, 0), **const),
            pl.BlockSpec((d_attn, d), lambda b, i: (0, 0), **const),
            pl.BlockSpec((HEAD_PAD, d), lambda b, i: (0, 0)),
            pl.BlockSpec((HEAD_PAD, 1), lambda b, i: (0, 0)),
        ],
        out_specs=[
            aug_blk, aug_blk,
            pl.BlockSpec((1, d_attn, tm), lambda b, i: (b, 0, i)),
            pl.BlockSpec((1, 1, N_STAT, HEAD_PAD, LANES), lambda b, i: (b, i, 0, 0, 0)),
        ],
        out_shape=[
            jax.ShapeDtypeStruct((bsz, lp, n_heads * AUG_DIM), BF16),
            jax.ShapeDtypeStruct((bsz, lp, n_heads * AUG_DIM), BF16),
            jax.ShapeDtypeStruct((bsz, d_attn, lp), BF16),
            jax.ShapeDtypeStruct((bsz, n_t, N_STAT, HEAD_PAD, LANES), F32),
        ],
        scratch_shapes=[pltpu.VMEM((HEAD_PAD, LANES), F32)],
        compiler_params=_cparams(2),
        name="attn_proj",
    )(h, gain, w_qk, wv_t, wf_t, bf_col)


def _flash_kernel(c_ref, first_ref, q_ref, k_ref, vt_ref, g_ref, o_ref, acc_ref, sa_ref, sb_ref,
                  *, tq, n_q, hpb):
    b = pl.program_id(0)
    hp = pl.program_id(1)
    i = pl.program_id(2)
    heads = range(hpb)
    c_base = [(b * n_q) * HEAD_PAD + hp * hpb + hh for hh in heads]
    c_q = [c_ref[c_base[hh] + i * HEAD_PAD] for hh in heads]
    first = first_ref[(b * pl.num_programs(1) + hp) * n_q + i]
    n_unmasked = i - first
    last = first + jnp.maximum(n_unmasked - 1, 0)
    key = lax.broadcasted_iota(jnp.int32, (tq, tq), 0)
    qry = lax.broadcasted_iota(jnp.int32, (tq, tq), 1)
    ones_rows = jnp.ones((ONES_ROWS, tq), BF16)

    def scores_into(s_ref, j, masked=False):
        start = pl.multiple_of(j * tq, tq)
        col_max = []
        for hh in heads:
            cols = slice(hh * AUG_DIM, (hh + 1) * AUG_DIM)
            st = lax.dot_general(k_ref[0, pl.ds(start, tq), cols], q_ref[0, :, cols], _NT,
                                 preferred_element_type=F32)
            if masked:
                st = jnp.where(key <= qry, st, NEG)
            s_ref[hh] = st
            col_max.append(jnp.max(st, axis=0, keepdims=True))
        return tuple(col_max)

    def accumulate(s_ref, col_max, j, delta, m_all):
        start = pl.multiple_of(j * tq, tq)
        m_out = []
        for hh in heads:
            vt = vt_ref[0, hh * HEAD_DIM:(hh + 1) * HEAD_DIM, pl.ds(start, tq)]
            m_new = jnp.maximum(m_all[hh], col_max[hh] + delta[hh])
            alpha = jnp.exp2(m_all[hh] - m_new)
            p = jnp.exp2(s_ref[hh] - (m_new - delta[hh])).astype(BF16)
            pv = jnp.dot(jnp.concatenate([vt, ones_rows], axis=0), p,
                         preferred_element_type=F32)
            acc_ref[hh] = alpha * acc_ref[hh] + pv
            m_out.append(m_new)
        return tuple(m_out)

    def delta_of(j):
        return [c_q[hh] - c_ref[c_base[hh] + j * HEAD_PAD] for hh in heads]

    def tail_delta_of(j):
        real = delta_of(jnp.minimum(j, last))
        return [jnp.where(j < i, real[hh], FAKE) for hh in heads]

    acc_ref[...] = jnp.zeros_like(acc_ref)
    max_a = scores_into(sa_ref, first)
    m_all = tuple(jnp.full((1, tq), NEG, F32) for _ in heads)
    n_trips = jnp.maximum((n_unmasked - 1) // 2, 0)

    def trip(jj, carry):
        m_all, max_a = carry
        j0 = first + 2 * jj
        max_b = scores_into(sb_ref, j0 + 1)
        m_all = accumulate(sa_ref, max_a, j0, delta_of(j0), m_all)
        max_a = scores_into(sa_ref, j0 + 2)
        m_all = accumulate(sb_ref, max_b, j0 + 1, delta_of(j0 + 1), m_all)
        return m_all, max_a

    m_all, max_a = lax.fori_loop(0, n_trips, trip, (m_all, max_a))
    j0 = first + 2 * n_trips
    j1 = jnp.minimum(j0 + 1, last)
    max_b = scores_into(sb_ref, j1)
    m_all = accumulate(sa_ref, max_a, jnp.minimum(j0, last), tail_delta_of(j0), m_all)
    max_d = scores_into(sa_ref, i, masked=True)
    m_all = accumulate(sb_ref, max_b, j1, tail_delta_of(j0 + 1), m_all)
    accumulate(sa_ref, max_d, i, [0.0 for _ in heads], m_all)

    for hh in heads:
        ot = acc_ref[hh, :HEAD_DIM, :] / acc_ref[hh, HEAD_DIM:HEAD_DIM + 1, :]
        ot = ot * _rms_scale(ot, axis=0)
        o_ref[0, :, hh * HEAD_DIM:(hh + 1) * HEAD_DIM] = (ot.T * g_ref[hh]).astype(BF16)


def _heads_per_block(n_heads):
    return 2 if n_heads % 2 == 0 else 1


def _first_key_tile(stats, n_heads):
    hpb = _heads_per_block(n_heads)
    st = stats[:, :, :, :n_heads, 0]
    c, bmax, qn, kn = (jnp.moveaxis(st[:, :, s], 1, 2) for s in
                       (STAT_C, STAT_BMAX, STAT_QN, STAT_KN))
    upper = (qn[..., :, None] * (kn[..., None, :] + kn[..., :, None]) * 1.001 + 1.0
             + (c[..., :, None] - c[..., None, :]) + bmax[..., None, :])
    n_t = c.shape[-1]
    earlier = jnp.arange(n_t)[None, :] < jnp.arange(n_t)[:, None]
    negligible = (upper < -SKIP_LOG2) & earlier
    first = jnp.min(jnp.where(negligible, n_t, jnp.arange(n_t)), axis=-1)
    first = first.reshape(first.shape[0], n_heads // hpb, hpb, n_t).min(axis=2)
    return first.astype(jnp.int32).reshape(-1)


def _flash(q, k, vt, c_tiles, first_tile, og_heads):
    bsz, lp, _ = q.shape
    n_heads = vt.shape[1] // HEAD_DIM
    hpb = _heads_per_block(n_heads)
    tq = SEQ_TILE
    n_q = lp // tq
    kern = functools.partial(_flash_kernel, tq=tq, n_q=n_q, hpb=hpb)
    resident = pl.Buffered(1)
    return pl.pallas_call(
        kern,
        grid=(bsz, n_heads // hpb, n_q),
        in_specs=[
            pl.BlockSpec(memory_space=pltpu.SMEM),
            pl.BlockSpec(memory_space=pltpu.SMEM),
            pl.BlockSpec((1, tq, hpb * AUG_DIM), lambda b, h, i: (b, i, h)),
            pl.BlockSpec((1, lp, hpb * AUG_DIM), lambda b, h, i: (b, 0, h),
                         pipeline_mode=resident),
            pl.BlockSpec((1, hpb * HEAD_DIM, lp), lambda b, h, i: (b, h, 0),
                         pipeline_mode=resident),
            pl.BlockSpec((hpb, 1, HEAD_DIM), lambda b, h, i: (h, 0, 0)),
        ],
        out_specs=pl.BlockSpec((1, tq, hpb * HEAD_DIM), lambda b, h, i: (b, i, h)),
        out_shape=jax.ShapeDtypeStruct((bsz, lp, n_heads * HEAD_DIM), BF16),
        scratch_shapes=[
            pltpu.VMEM((hpb, HEAD_DIM + ONES_ROWS, tq), F32),
            pltpu.VMEM((hpb, tq, tq), F32),
            pltpu.VMEM((hpb, tq, tq), F32),
        ],
        compiler_params=_cparams(3),
        name="flash",
    )(c_tiles, first_tile, q, k, vt, og_heads)


def _mix_out_kernel(h_ref, ya_ref, g_ref, wc_ref, cw_ref, og_ref, wo_ref, o_ref, u_ref,
                    *, tm, n_groups, n_pad):
    i = pl.program_id(1)
    d_conv = n_groups * HEAD_DIM
    x = h_ref[0]
    hn = (x * _rms_scale(x) * g_ref[...]).astype(BF16)

    @pl.when(i == 0)
    def _reset():
        u_ref[pl.ds(0, HALO), :] = jnp.zeros((HALO, d_conv), F32)

    gate_b = jnp.dot(hn, wc_ref[:, :d_conv], preferred_element_type=F32)
    gate_c = jnp.dot(hn, wc_ref[:, d_conv:2 * d_conv], preferred_element_type=F32)
    hc = jnp.dot(hn, wc_ref[:, 2 * d_conv:], preferred_element_type=F32)
    u = gate_c * hc
    u_ref[pl.ds(HALO, tm), :] = u
    u1 = u_ref[pl.ds(HALO - 1, tm), :]
    u2 = u_ref[pl.ds(HALO - 2, tm), :]
    conv = cw_ref[0:1, :] * u2 + cw_ref[1:2, :] * u1 + cw_ref[2:3, :] * u
    y = gate_b * conv
    u_ref[pl.ds(0, HALO), :] = u_ref[pl.ds(tm, HALO), :]
    yc = jnp.concatenate(
        [(y[:, g * HEAD_DIM:(g + 1) * HEAD_DIM] * _rms_scale(y[:, g * HEAD_DIM:(g + 1) * HEAD_DIM])
          * og_ref[:, g * HEAD_DIM:(g + 1) * HEAD_DIM]).astype(BF16) for g in range(n_groups)],
        axis=1)
    d_attn = ya_ref.shape[2]
    out = (jnp.dot(ya_ref[0], wo_ref[:d_attn, :], preferred_element_type=F32)
           + jnp.dot(yc, wo_ref[d_attn:, :], preferred_element_type=F32))
    out = x + out
    pos = i * tm + lax.broadcasted_iota(jnp.int32, (tm, 1), 0)
    o_ref[0] = jnp.where(pos >= n_pad, out, 0.0)


def _mix_out(h, ya, gain, w_conv, conv_w, og_conv, w_o, *, n_pad):
    bsz, lp, d = h.shape
    d_attn = ya.shape[2]
    d_conv = w_conv.shape[1] // 3
    tm = SEQ_TILE
    kern = functools.partial(_mix_out_kernel, tm=tm, n_groups=d_conv // HEAD_DIM, n_pad=n_pad)
    const = dict(pipeline_mode=pl.Buffered(1))
    return pl.pallas_call(
        kern,
        grid=(bsz, lp // tm),
        in_specs=[
            pl.BlockSpec((1, tm, d), lambda b, i: (b, i, 0)),
            pl.BlockSpec((1, tm, d_attn), lambda b, i: (b, i, 0)),
            pl.BlockSpec((1, d), lambda b, i: (0, 0)),
            pl.BlockSpec((d, 3 * d_conv), lambda b, i: (0, 0), **const),
            pl.BlockSpec((HALO, d_conv), lambda b, i: (0, 0)),
            pl.BlockSpec((1, d_conv), lambda b, i: (0, 0)),
            pl.BlockSpec((d_attn + d_conv, d), lambda b, i: (0, 0), **const),
        ],
        out_specs=pl.BlockSpec((1, tm, d), lambda b, i: (b, i, 0)),
        out_shape=jax.ShapeDtypeStruct(h.shape, F32),
        scratch_shapes=[pltpu.VMEM((tm + HALO, d_conv), F32)],
        compiler_params=_cparams(2),
        name="mix_out",
    )(h, ya, gain, w_conv, conv_w, og_conv, w_o)


def _ffn_kernel(h_ref, g_ref, wg_ref, wu_ref, wd_ref, o_ref, hn_ref):
    f = pl.program_id(1)

    @pl.when(f == 0)
    def _norm():
        x = h_ref[...]
        hn_ref[...] = (x * _rms_scale(x) * g_ref[...]).astype(BF16)
        o_ref[...] = x

    hn = hn_ref[...]
    gate = jnp.dot(hn, wg_ref[...], preferred_element_type=F32)
    up = jnp.dot(hn, wu_ref[...], preferred_element_type=F32)
    act = (jax.nn.silu(gate) * up).astype(BF16)
    o_ref[...] += jnp.dot(act, wd_ref[...], preferred_element_type=F32)


def _ffn(h, gain, w_gate, w_up, w_down):
    rows, d = h.shape
    d_ff = w_gate.shape[1]
    tm = 1024 if rows % 1024 == 0 else 512
    tf = 512 if d_ff % 512 == 0 else 256
    return pl.pallas_call(
        _ffn_kernel,
        grid=(rows // tm, d_ff // tf),
        in_specs=[
            pl.BlockSpec((tm, d), lambda r, f: (r, 0)),
            pl.BlockSpec((1, d), lambda r, f: (0, 0)),
            pl.BlockSpec((d, tf), lambda r, f: (0, f)),
            pl.BlockSpec((d, tf), lambda r, f: (0, f)),
            pl.BlockSpec((tf, d), lambda r, f: (f, 0)),
        ],
        out_specs=pl.BlockSpec((tm, d), lambda r, f: (r, 0)),
        out_shape=jax.ShapeDtypeStruct(h.shape, F32),
        scratch_shapes=[pltpu.VMEM((tm, d), BF16)],
        compiler_params=_cparams(2),
        name="ffn",
    )(h, gain, w_gate, w_up, w_down)


def _final_norm_kernel(h_ref, g_ref, o_ref):
    x = h_ref[0]
    o_ref[0] = x * _rms_scale(x) * g_ref[...]


def _final_norm(h, gain, *, seq):
    bsz, lp, d = h.shape
    tm = SEQ_TILE
    skip = (lp - seq) // tm
    return pl.pallas_call(
        _final_norm_kernel,
        grid=(bsz, seq // tm),
        in_specs=[
            pl.BlockSpec((1, tm, d), lambda b, i: (b, i + skip, 0)),
            pl.BlockSpec((1, d), lambda b, i: (0, 0)),
        ],
        out_specs=pl.BlockSpec((1, tm, d), lambda b, i: (b, i, 0)),
        out_shape=jax.ShapeDtypeStruct((bsz, seq, d), F32),
        compiler_params=_cparams(2),
        name="final_norm",
    )(h, gain)


def kernel(x, meta, norm_mix, w_in, b_f, conv_w, out_gain, w_out, norm_ffn, w_gate, w_up, w_down, final_norm):
    bsz, seq, d = x.shape
    n_meta = meta.shape[0]
    depth = w_in.shape[0]
    d_attn = d // 2
    n_heads = d_attn // HEAD_DIM
    assert n_heads <= HEAD_PAD and seq % SEQ_TILE == 0 and n_meta <= SEQ_TILE
    assert w_in.shape[2] == 6 * d_attn + n_heads
    lp = seq + SEQ_TILE
    n_pad = lp - seq - n_meta
    scale = HEAD_DIM ** -0.5

    m = jnp.broadcast_to(meta.astype(x.dtype)[None], (bsz, n_meta, d))
    h = jnp.concatenate([jnp.zeros((bsz, n_pad, d), x.dtype), m, x], axis=1)

    for l in range(depth):
        w_qk = w_in[l, :, :2 * d_attn].astype(BF16)
        wv_t = w_in[l, :, 2 * d_attn:3 * d_attn].T.astype(BF16)
        w_conv = w_in[l, :, 3 * d_attn:6 * d_attn].astype(BF16)
        wf_t = jnp.zeros((HEAD_PAD, d), BF16).at[:n_heads].set(w_in[l, :, 6 * d_attn:].T.astype(BF16))
        bf_col = jnp.zeros((HEAD_PAD, 1), F32).at[:n_heads, 0].set(b_f[l])
        cw = jnp.zeros((HALO, d_attn), F32).at[:CONV_WIDTH].set(conv_w[l])
        og_heads = out_gain[l, :d_attn].reshape(n_heads, 1, HEAD_DIM)
        og_conv = out_gain[l, d_attn:].reshape(1, d_attn)

        gain = norm_mix[l].reshape(1, d)
        q, k, vt, stats = _attn_proj(h, gain, w_qk, wv_t, wf_t, bf_col, n_pad=n_pad, scale=scale)
        ya = _flash(q, k, vt, stats[:, :, STAT_C, :, 0].reshape(-1),
                    _first_key_tile(stats, n_heads), og_heads)

        h2 = _mix_out(h, ya, gain, w_conv, cw, og_conv, w_out[l].astype(BF16), n_pad=n_pad)
        h2 = _ffn(h2.reshape(bsz * lp, d), norm_ffn[l].reshape(1, d), w_gate[l].astype(BF16),
                  w_up[l].astype(BF16), w_down[l].astype(BF16))
        h = h2.reshape(bsz, lp, d)

    return _final_norm(h, final_norm.reshape(1, d), seq=seq)
```

```python
import functools
import math

import jax
import jax.numpy as jnp
from jax import lax
from jax.experimental import pallas as pl
from jax.experimental.pallas import tpu as pltpu

F32 = jnp.float32
BF16 = jnp.bfloat16

HEAD_DIM = 128
AUG_DIM = 2 * HEAD_DIM
N_SPLIT = 3
CONV_WIDTH = 3
EPS = 1e-6
NEG = -1e30
FAKE = 2 * NEG
LOG2E = math.log2(math.e)
SEQ_TILE = 512
HEAD_PAD = 16
ONES_ROWS = 16
LANES = 128
STAT_C, STAT_BMAX, STAT_QN, STAT_KN, N_STAT = 0, 1, 2, 3, 4
SKIP_LOG2 = 140.0
HALO = 8
VMEM_LIMIT = 56 * 1024 * 1024

_NT = (((1,), (1,)), ((), ()))


def _cparams(n_axes):
    return pltpu.CompilerParams(dimension_semantics=("arbitrary",) * n_axes,
                                vmem_limit_bytes=VMEM_LIMIT)


def _rms_scale(x, axis=-1):
    return lax.rsqrt(jnp.mean(x * x, axis=axis, keepdims=True) + EPS)


def _cumsum_lanes(x):
    n = x.shape[1]
    lane = lax.broadcasted_iota(jnp.int32, x.shape, 1)
    shift = 1
    while shift < n:
        x = x + jnp.where(lane >= shift, pltpu.roll(x, shift, axis=1), 0.0)
        shift *= 2
    return x


def _round_bf16(x):
    return x.astype(BF16).astype(F32)


def _max_row_norm(x):
    xf = x.astype(F32)
    n2 = jnp.max(jnp.sum(xf * xf, axis=1, keepdims=True), axis=0, keepdims=True)
    return jnp.broadcast_to(jnp.sqrt(n2), (1, LANES))


def _attn_proj_kernel(h_ref, g_ref, wqk_ref, wvt_ref, wf_ref, bf_ref,
                      q_ref, k_ref, vt_ref, stat_ref, carry_ref,
                      *, tm, n_heads, n_pad, scale):
    i = pl.program_id(1)
    d_attn = n_heads * HEAD_DIM
    x = h_ref[...]
    hn = (x * _rms_scale(x) * g_ref[...]).astype(BF16)

    @pl.when(i == 0)
    def _reset():
        carry_ref[...] = jnp.zeros_like(carry_ref)

    logit = lax.dot_general(wf_ref[...], hn, _NT, preferred_element_type=F32)
    log_f = jax.nn.log_sigmoid(logit + bf_ref[...])
    r = _cumsum_lanes(log_f)
    stat_ref[0, 0] = jnp.zeros(stat_ref.shape[2:], F32)
    stat_ref[0, 0, STAT_C] = carry_ref[...] * LOG2E
    carry_ref[...] = carry_ref[...] + jnp.broadcast_to(r[:, tm - 1:tm], carry_ref.shape)
    pos = i * tm + lax.broadcasted_iota(jnp.int32, r.shape, 1)
    bias = jnp.where(pos >= n_pad, -LOG2E * r, NEG)
    stat_ref[0, 0, STAT_BMAX] = jnp.broadcast_to(jnp.max(bias, axis=1, keepdims=True),
                                                 (HEAD_PAD, LANES))
    bias_col = jnp.concatenate([bias, jnp.zeros((LANES - HEAD_PAD, tm), F32)], axis=0).T

    lane = lax.broadcasted_iota(jnp.int32, (tm, HEAD_DIM), 1)
    ones = jnp.where(lane < N_SPLIT, 1.0, 0.0).astype(BF16)
    zq = jnp.dot(hn, wqk_ref[:, :d_attn], preferred_element_type=F32)
    for h in range(n_heads):
        qh = (zq[:, h * HEAD_DIM:(h + 1) * HEAD_DIM] * (scale * LOG2E)).astype(BF16)
        q_ref[0, :, h * AUG_DIM:h * AUG_DIM + HEAD_DIM] = qh
        q_ref[0, :, h * AUG_DIM + HEAD_DIM:(h + 1) * AUG_DIM] = ones
        stat_ref[0, 0, STAT_QN, h:h + 1, :] = _max_row_norm(qh)

    zk = jnp.dot(hn, wqk_ref[:, d_attn:], preferred_element_type=F32)
    for h in range(n_heads):
        kh = zk[:, h * HEAD_DIM:(h + 1) * HEAD_DIM].astype(BF16)
        k_ref[0, :, h * AUG_DIM:h * AUG_DIM + HEAD_DIM] = kh
        stat_ref[0, 0, STAT_KN, h:h + 1, :] = _max_row_norm(kh)
        xb = jnp.broadcast_to(bias_col[:, h:h + 1], (tm, HEAD_DIM))
        hi = _round_bf16(xb)
        mid = _round_bf16(xb - hi)
        lo = xb - hi - mid
        aug = jnp.where(lane == 0, hi, jnp.where(lane == 1, mid, jnp.where(lane == 2, lo, 0.0)))
        k_ref[0, :, h * AUG_DIM + HEAD_DIM:(h + 1) * AUG_DIM] = aug.astype(BF16)

    zvt = lax.dot_general(wvt_ref[...], hn, _NT, preferred_element_type=F32)
    vt_ref[0] = zvt.astype(BF16)


def _row_tile_of(bsz, n_t):
    n_x = n_t - 1
    return lambda b, i: (jnp.where(i == 0, bsz * n_x + b, b * n_x + i - 1), 0)


def _attn_proj(h, gain, w_qk, wv_t, wf_t, bf_col, *, bsz, n_pad, scale):
    d = h.shape[1]
    d_attn = wv_t.shape[0]
    n_heads = d_attn // HEAD_DIM
    tm = SEQ_TILE
    n_t = h.shape[0] // (bsz * tm)
    lp = n_t * tm
    kern = functools.partial(_attn_proj_kernel, tm=tm, n_heads=n_heads, n_pad=n_pad, scale=scale)
    aug_blk = pl.BlockSpec((1, tm, n_heads * AUG_DIM), lambda b, i: (b, i, 0))
    const = dict(pipeline_mode=pl.Buffered(1))
    return pl.pallas_call(
        kern,
        grid=(bsz, n_t),
        in_specs=[
            pl.BlockSpec((tm, d), _row_tile_of(bsz, n_t)),
            pl.BlockSpec((1, d), lambda b, i: (0, 0)),
            pl.BlockSpec((d, 2 * d_attn), lambda b, i: (0, 0), **const),
            pl.BlockSpec((d_attn, d), lambda b, i: (0, 0), **const),
            pl.BlockSpec((HEAD_PAD, d), lambda b, i: (0, 0)),
            pl.BlockSpec((HEAD_PAD, 1), lambda b, i: (0, 0)),
        ],
        out_specs=[
            aug_blk, aug_blk,
            pl.BlockSpec((1, d_attn, tm), lambda b, i: (b, 0, i)),
            pl.BlockSpec((1, 1, N_STAT, HEAD_PAD, LANES), lambda b, i: (b, i, 0, 0, 0)),
        ],
        out_shape=[
            jax.ShapeDtypeStruct((bsz, lp, n_heads * AUG_DIM), BF16),
            jax.ShapeDtypeStruct((bsz, lp, n_heads * AUG_DIM), BF16),
            jax.ShapeDtypeStruct((bsz, d_attn, lp), BF16),
            jax.ShapeDtypeStruct((bsz, n_t, N_STAT, HEAD_PAD, LANES), F32),
        ],
        scratch_shapes=[pltpu.VMEM((HEAD_PAD, LANES), F32)],
        compiler_params=_cparams(2),
        name="attn_proj",
    )(h, gain, w_qk, wv_t, wf_t, bf_col)


def _flash_kernel(c_ref, first_ref, q_ref, k_ref, vt_ref, g_ref, o_ref, acc_ref, sa_ref, sb_ref,
                  *, tq, n_q, hpb):
    b = pl.program_id(0)
    hp = pl.program_id(1)
    i = pl.program_id(2)
    heads = range(hpb)
    c_base = [(b * n_q) * HEAD_PAD + hp * hpb + hh for hh in heads]
    c_q = [c_ref[c_base[hh] + i * HEAD_PAD] for hh in heads]
    first = first_ref[(b * pl.num_programs(1) + hp) * n_q + i]
    n_unmasked = i - first
    last = first + jnp.maximum(n_unmasked - 1, 0)
    key = lax.broadcasted_iota(jnp.int32, (tq, tq), 0)
    qry = lax.broadcasted_iota(jnp.int32, (tq, tq), 1)
    ones_rows = jnp.ones((ONES_ROWS, tq), BF16)

    def scores_into(s_ref, j, masked=False):
        start = pl.multiple_of(j * tq, tq)
        col_max = []
        for hh in heads:
            cols = slice(hh * AUG_DIM, (hh + 1) * AUG_DIM)
            st = lax.dot_general(k_ref[0, pl.ds(start, tq), cols], q_ref[0, :, cols], _NT,
                                 preferred_element_type=F32)
            if masked:
                st = jnp.where(key <= qry, st, NEG)
            s_ref[hh] = st
            col_max.append(jnp.max(st, axis=0, keepdims=True))
        return tuple(col_max)

    def accumulate(s_ref, col_max, j, delta, m_all):
        start = pl.multiple_of(j * tq, tq)
        m_out = []
        for hh in heads:
            vt = vt_ref[0, hh * HEAD_DIM:(hh + 1) * HEAD_DIM, pl.ds(start, tq)]
            m_new = jnp.maximum(m_all[hh], col_max[hh] + delta[hh])
            alpha = jnp.exp2(m_all[hh] - m_new)
            p = jnp.exp2(s_ref[hh] - (m_new - delta[hh])).astype(BF16)
            pv = jnp.dot(jnp.concatenate([vt, ones_rows], axis=0), p,
                         preferred_element_type=F32)
            acc_ref[hh] = alpha * acc_ref[hh] + pv
            m_out.append(m_new)
        return tuple(m_out)

    def delta_of(j):
        return [c_q[hh] - c_ref[c_base[hh] + j * HEAD_PAD] for hh in heads]

    def tail_delta_of(j):
        real = delta_of(jnp.minimum(j, last))
        return [jnp.where(j < i, real[hh], FAKE) for hh in heads]

    acc_ref[...] = jnp.zeros_like(acc_ref)
    max_a = scores_into(sa_ref, first)
    m_all = tuple(jnp.full((1, tq), NEG, F32) for _ in heads)
    n_trips = jnp.maximum((n_unmasked - 1) // 2, 0)

    def trip(jj, carry):
        m_all, max_a = carry
        j0 = first + 2 * jj
        max_b = scores_into(sb_ref, j0 + 1)
        m_all = accumulate(sa_ref, max_a, j0, delta_of(j0), m_all)
        max_a = scores_into(sa_ref, j0 + 2)
        m_all = accumulate(sb_ref, max_b, j0 + 1, delta_of(j0 + 1), m_all)
        return m_all, max_a

    m_all, max_a = lax.fori_loop(0, n_trips, trip, (m_all, max_a))
    j0 = first + 2 * n_trips
    j1 = jnp.minimum(j0 + 1, last)
    max_b = scores_into(sb_ref, j1)
    m_all = accumulate(sa_ref, max_a, jnp.minimum(j0, last), tail_delta_of(j0), m_all)
    max_d = scores_into(sa_ref, i, masked=True)
    m_all = accumulate(sb_ref, max_b, j1, tail_delta_of(j0 + 1), m_all)
    accumulate(sa_ref, max_d, i, [0.0 for _ in heads], m_all)

    for hh in heads:
        ot = acc_ref[hh, :HEAD_DIM, :] / acc_ref[hh, HEAD_DIM:HEAD_DIM + 1, :]
        ot = ot * _rms_scale(ot, axis=0)
        o_ref[0, :, hh * HEAD_DIM:(hh + 1) * HEAD_DIM] = (ot.T * g_ref[hh]).astype(BF16)


def _heads_per_block(n_heads):
    return 2 if n_heads % 2 == 0 else 1


def _first_key_tile(stats, n_heads):
    hpb = _heads_per_block(n_heads)
    st = stats[:, :, :, :n_heads, 0]
    c, bmax, qn, kn = (jnp.moveaxis(st[:, :, s], 1, 2) for s in
                       (STAT_C, STAT_BMAX, STAT_QN, STAT_KN))
    upper = (qn[..., :, None] * (kn[..., None, :] + kn[..., :, None]) * 1.001 + 1.0
             + (c[..., :, None] - c[..., None, :]) + bmax[..., None, :])
    n_t = c.shape[-1]
    earlier = jnp.arange(n_t)[None, :] < jnp.arange(n_t)[:, None]
    negligible = (upper < -SKIP_LOG2) & earlier
    first = jnp.min(jnp.where(negligible, n_t, jnp.arange(n_t)), axis=-1)
    first = first.reshape(first.shape[0], n_heads // hpb, hpb, n_t).min(axis=2)
    return first.astype(jnp.int32).reshape(-1)


def _flash(q, k, vt, c_tiles, first_tile, og_heads):
    bsz, lp, _ = q.shape
    n_heads = vt.shape[1] // HEAD_DIM
    hpb = _heads_per_block(n_heads)
    tq = SEQ_TILE
    n_q = lp // tq
    kern = functools.partial(_flash_kernel, tq=tq, n_q=n_q, hpb=hpb)
    resident = pl.Buffered(1)
    return pl.pallas_call(
        kern,
        grid=(bsz, n_heads // hpb, n_q),
        in_specs=[
            pl.BlockSpec(memory_space=pltpu.SMEM),
            pl.BlockSpec(memory_space=pltpu.SMEM),
            pl.BlockSpec((1, tq, hpb * AUG_DIM), lambda b, h, i: (b, i, h)),
            pl.BlockSpec((1, lp, hpb * AUG_DIM), lambda b, h, i: (b, 0, h),
                         pipeline_mode=resident),
            pl.BlockSpec((1, hpb * HEAD_DIM, lp), lambda b, h, i: (b, h, 0),
                         pipeline_mode=resident),
            pl.BlockSpec((hpb, 1, HEAD_DIM), lambda b, h, i: (h, 0, 0)),
        ],
        out_specs=pl.BlockSpec((1, tq, hpb * HEAD_DIM), lambda b, h, i: (b, i, h)),
        out_shape=jax.ShapeDtypeStruct((bsz, lp, n_heads * HEAD_DIM), BF16),
        scratch_shapes=[
            pltpu.VMEM((hpb, HEAD_DIM + ONES_ROWS, tq), F32),
            pltpu.VMEM((hpb, tq, tq), F32),
            pltpu.VMEM((hpb, tq, tq), F32),
        ],
        compiler_params=_cparams(3),
        name="flash",
    )(c_tiles, first_tile, q, k, vt, og_heads)


def _mix_out_kernel(h_ref, ya_ref, g_ref, wc_ref, cw_ref, og_ref, wo_ref, o_ref, u_ref,
                    *, tm, n_groups, n_pad):
    i = pl.program_id(1)
    d_conv = n_groups * HEAD_DIM
    x = h_ref[...]
    hn = (x * _rms_scale(x) * g_ref[...]).astype(BF16)

    @pl.when(i == 0)
    def _reset():
        u_ref[pl.ds(0, HALO), :] = jnp.zeros((HALO, d_conv), F32)

    gate_b = jnp.dot(hn, wc_ref[:, :d_conv], preferred_element_type=F32)
    gate_c = jnp.dot(hn, wc_ref[:, d_conv:2 * d_conv], preferred_element_type=F32)
    hc = jnp.dot(hn, wc_ref[:, 2 * d_conv:], preferred_element_type=F32)
    u = gate_c * hc
    u_ref[pl.ds(HALO, tm), :] = u
    u1 = u_ref[pl.ds(HALO - 1, tm), :]
    u2 = u_ref[pl.ds(HALO - 2, tm), :]
    conv = cw_ref[0:1, :] * u2 + cw_ref[1:2, :] * u1 + cw_ref[2:3, :] * u
    y = gate_b * conv
    u_ref[pl.ds(0, HALO), :] = u_ref[pl.ds(tm, HALO), :]
    yc = jnp.concatenate(
        [(y[:, g * HEAD_DIM:(g + 1) * HEAD_DIM] * _rms_scale(y[:, g * HEAD_DIM:(g + 1) * HEAD_DIM])
          * og_ref[:, g * HEAD_DIM:(g + 1) * HEAD_DIM]).astype(BF16) for g in range(n_groups)],
        axis=1)
    d_attn = ya_ref.shape[2]
    out = (jnp.dot(ya_ref[0], wo_ref[:d_attn, :], preferred_element_type=F32)
           + jnp.dot(yc, wo_ref[d_attn:, :], preferred_element_type=F32))
    out = x + out
    pos = i * tm + lax.broadcasted_iota(jnp.int32, (tm, 1), 0)
    o_ref[...] = jnp.where(pos >= n_pad, out, 0.0)


def _mix_out(h, ya, gain, w_conv, conv_w, og_conv, w_o, *, n_pad):
    d = h.shape[1]
    bsz, lp, d_attn = ya.shape
    d_conv = w_conv.shape[1] // 3
    tm = SEQ_TILE
    h_blk = pl.BlockSpec((tm, d), _row_tile_of(bsz, lp // tm))
    kern = functools.partial(_mix_out_kernel, tm=tm, n_groups=d_conv // HEAD_DIM, n_pad=n_pad)
    const = dict(pipeline_mode=pl.Buffered(1))
    return pl.pallas_call(
        kern,
        grid=(bsz, lp // tm),
        in_specs=[
            h_blk,
            pl.BlockSpec((1, tm, d_attn), lambda b, i: (b, i, 0)),
            pl.BlockSpec((1, d), lambda b, i: (0, 0)),
            pl.BlockSpec((d, 3 * d_conv), lambda b, i: (0, 0), **const),
            pl.BlockSpec((HALO, d_conv), lambda b, i: (0, 0)),
            pl.BlockSpec((1, d_conv), lambda b, i: (0, 0)),
            pl.BlockSpec((d_attn + d_conv, d), lambda b, i: (0, 0), **const),
        ],
        out_specs=h_blk,
        out_shape=jax.ShapeDtypeStruct(h.shape, F32),
        scratch_shapes=[pltpu.VMEM((tm + HALO, d_conv), F32)],
        compiler_params=_cparams(2),
        name="mix_out",
    )(h, ya, gain, w_conv, conv_w, og_conv, w_o)


def _ffn_kernel(h_ref, g_ref, wg_ref, wu_ref, wd_ref, gf_ref, o_ref, hn_ref, *, final):
    f = pl.program_id(1)

    @pl.when(f == 0)
    def _norm():
        x = h_ref[...]
        hn_ref[...] = (x * _rms_scale(x) * g_ref[...]).astype(BF16)
        o_ref[...] = x

    hn = hn_ref[...]
    gate = jnp.dot(hn, wg_ref[...], preferred_element_type=F32)
    up = jnp.dot(hn, wu_ref[...], preferred_element_type=F32)
    act = (jax.nn.silu(gate) * up).astype(BF16)
    o_ref[...] += jnp.dot(act, wd_ref[...], preferred_element_type=F32)

    if final:
        @pl.when(f == pl.num_programs(1) - 1)
        def _final_norm():
            y = o_ref[...]
            o_ref[...] = y * _rms_scale(y) * gf_ref[...]


def _ffn(h, gain, w_gate, w_up, w_down, final_gain, *, rows, final):
    d = h.shape[1]
    d_ff = w_gate.shape[1]
    tm = 1024 if rows % 1024 == 0 else 512
    tf = 512 if d_ff % 512 == 0 else 256
    return pl.pallas_call(
        functools.partial(_ffn_kernel, final=final),
        grid=(rows // tm, d_ff // tf),
        in_specs=[
            pl.BlockSpec((tm, d), lambda r, f: (r, 0)),
            pl.BlockSpec((1, d), lambda r, f: (0, 0)),
            pl.BlockSpec((d, tf), lambda r, f: (0, f)),
            pl.BlockSpec((d, tf), lambda r, f: (0, f)),
            pl.BlockSpec((tf, d), lambda r, f: (f, 0)),
            pl.BlockSpec((1, d), lambda r, f: (0, 0)),
        ],
        out_specs=pl.BlockSpec((tm, d), lambda r, f: (r, 0)),
        out_shape=jax.ShapeDtypeStruct((rows, d), F32),
        scratch_shapes=[pltpu.VMEM((tm, d), BF16)],
        compiler_params=_cparams(2),
        name="ffn",
    )(h, gain, w_gate, w_up, w_down, final_gain)


def kernel(x, meta, norm_mix, w_in, b_f, conv_w, out_gain, w_out, norm_ffn, w_gate, w_up, w_down, final_norm):
    bsz, seq, d = x.shape
    n_meta = meta.shape[0]
    depth = w_in.shape[0]
    d_attn = d // 2
    n_heads = d_attn // HEAD_DIM
    assert n_heads <= HEAD_PAD and seq % SEQ_TILE == 0 and n_meta <= SEQ_TILE
    assert w_in.shape[2] == 6 * d_attn + n_heads
    lp = seq + SEQ_TILE
    n_pad = lp - seq - n_meta
    scale = HEAD_DIM ** -0.5

    m = jnp.broadcast_to(meta.astype(x.dtype)[None], (bsz, n_meta, d))
    head = jnp.concatenate([jnp.zeros((bsz, n_pad, d), x.dtype), m], axis=1)
    h = jnp.concatenate([x.reshape(bsz * seq, d), head.reshape(bsz * SEQ_TILE, d)], axis=0)
    final_gain = final_norm.reshape(1, d)

    for l in range(depth):
        w_qk = w_in[l, :, :2 * d_attn].astype(BF16)
        wv_t = w_in[l, :, 2 * d_attn:3 * d_attn].T.astype(BF16)
        w_conv = w_in[l, :, 3 * d_attn:6 * d_attn].astype(BF16)
        wf_t = jnp.zeros((HEAD_PAD, d), BF16).at[:n_heads].set(w_in[l, :, 6 * d_attn:].T.astype(BF16))
        bf_col = jnp.zeros((HEAD_PAD, 1), F32).at[:n_heads, 0].set(b_f[l])
        cw = jnp.zeros((HALO, d_attn), F32).at[:CONV_WIDTH].set(conv_w[l])
        og_heads = out_gain[l, :d_attn].reshape(n_heads, 1, HEAD_DIM)
        og_conv = out_gain[l, d_attn:].reshape(1, d_attn)

        gain = norm_mix[l].reshape(1, d)
        q, k, vt, stats = _attn_proj(h, gain, w_qk, wv_t, wf_t, bf_col, bsz=bsz, n_pad=n_pad,
                                     scale=scale)
        ya = _flash(q, k, vt, stats[:, :, STAT_C, :, 0].reshape(-1),
                    _first_key_tile(stats, n_heads), og_heads)

        h = _mix_out(h, ya, gain, w_conv, cw, og_conv, w_out[l].astype(BF16), n_pad=n_pad)
        final = l == depth - 1
        h = _ffn(h, norm_ffn[l].reshape(1, d), w_gate[l].astype(BF16), w_up[l].astype(BF16),
                 w_down[l].astype(BF16), final_gain,
                 rows=bsz * seq if final else h.shape[0], final=final)

    return h.reshape(bsz, seq, d)
```

```python
import functools
import math

import jax
import jax.numpy as jnp
from jax import lax
from jax.experimental import pallas as pl
from jax.experimental.pallas import tpu as pltpu

F32 = jnp.float32
BF16 = jnp.bfloat16

HEAD_DIM = 128
AUG_DIM = 2 * HEAD_DIM
N_SPLIT = 3
CONV_WIDTH = 3
EPS = 1e-6
NEG = -1e30
FAKE = 2 * NEG
LOG2E = math.log2(math.e)
SEQ_TILE = 512
HEAD_PAD = 16
ONES_ROWS = 16
LANES = 128
STAT_C, STAT_BMAX, STAT_QN, STAT_KN, N_STAT = 0, 1, 2, 3, 4
SKIP_LOG2 = 140.0
HALO = 8
VMEM_LIMIT = 56 * 1024 * 1024

_NT = (((1,), (1,)), ((), ()))


def _cparams(n_axes):
    return pltpu.CompilerParams(dimension_semantics=("arbitrary",) * n_axes,
                                vmem_limit_bytes=VMEM_LIMIT)


def _rms_scale(x, axis=-1):
    return lax.rsqrt(jnp.mean(x * x, axis=axis, keepdims=True) + EPS)


def _cumsum_lanes(x):
    n = x.shape[1]
    lane = lax.broadcasted_iota(jnp.int32, x.shape, 1)
    shift = 1
    while shift < n:
        x = x + jnp.where(lane >= shift, pltpu.roll(x, shift, axis=1), 0.0)
        shift *= 2
    return x


def _round_bf16(x):
    return x.astype(BF16).astype(F32)


def _max_row_norm(x):
    xf = x.astype(F32)
    n2 = jnp.max(jnp.sum(xf * xf, axis=1, keepdims=True), axis=0, keepdims=True)
    return jnp.broadcast_to(jnp.sqrt(n2), (1, LANES))


def _attn_proj_kernel(h_ref, g_ref, wqk_ref, wvt_ref, wf_ref, bf_ref,
                      q_ref, k_ref, vt_ref, stat_ref, hn_ref, carry_ref,
                      *, tm, n_heads, n_pad, scale):
    i = pl.program_id(1)
    d_attn = n_heads * HEAD_DIM
    x = h_ref[...]
    hn = (x * _rms_scale(x) * g_ref[...]).astype(BF16)
    hn_ref[...] = hn

    @pl.when(i == 0)
    def _reset():
        carry_ref[...] = jnp.zeros_like(carry_ref)

    logit = lax.dot_general(wf_ref[...], hn, _NT, preferred_element_type=F32)
    log_f = jax.nn.log_sigmoid(logit + bf_ref[...])
    r = _cumsum_lanes(log_f)
    stat_ref[0, 0] = jnp.zeros(stat_ref.shape[2:], F32)
    stat_ref[0, 0, STAT_C] = carry_ref[...] * LOG2E
    carry_ref[...] = carry_ref[...] + jnp.broadcast_to(r[:, tm - 1:tm], carry_ref.shape)
    pos = i * tm + lax.broadcasted_iota(jnp.int32, r.shape, 1)
    bias = jnp.where(pos >= n_pad, -LOG2E * r, NEG)
    stat_ref[0, 0, STAT_BMAX] = jnp.broadcast_to(jnp.max(bias, axis=1, keepdims=True),
                                                 (HEAD_PAD, LANES))
    bias_col = jnp.concatenate([bias, jnp.zeros((LANES - HEAD_PAD, tm), F32)], axis=0).T

    lane = lax.broadcasted_iota(jnp.int32, (tm, HEAD_DIM), 1)
    ones = jnp.where(lane < N_SPLIT, 1.0, 0.0).astype(BF16)
    zq = jnp.dot(hn, wqk_ref[:, :d_attn], preferred_element_type=F32)
    for h in range(n_heads):
        qh = (zq[:, h * HEAD_DIM:(h + 1) * HEAD_DIM] * (scale * LOG2E)).astype(BF16)
        q_ref[0, :, h * AUG_DIM:h * AUG_DIM + HEAD_DIM] = qh
        q_ref[0, :, h * AUG_DIM + HEAD_DIM:(h + 1) * AUG_DIM] = ones
        stat_ref[0, 0, STAT_QN, h:h + 1, :] = _max_row_norm(qh)

    zk = jnp.dot(hn, wqk_ref[:, d_attn:], preferred_element_type=F32)
    for h in range(n_heads):
        kh = zk[:, h * HEAD_DIM:(h + 1) * HEAD_DIM].astype(BF16)
        k_ref[0, :, h * AUG_DIM:h * AUG_DIM + HEAD_DIM] = kh
        stat_ref[0, 0, STAT_KN, h:h + 1, :] = _max_row_norm(kh)
        xb = jnp.broadcast_to(bias_col[:, h:h + 1], (tm, HEAD_DIM))
        hi = _round_bf16(xb)
        mid = _round_bf16(xb - hi)
        lo = xb - hi - mid
        aug = jnp.where(lane == 0, hi, jnp.where(lane == 1, mid, jnp.where(lane == 2, lo, 0.0)))
        k_ref[0, :, h * AUG_DIM + HEAD_DIM:(h + 1) * AUG_DIM] = aug.astype(BF16)

    zvt = lax.dot_general(wvt_ref[...], hn, _NT, preferred_element_type=F32)
    vt_ref[0] = zvt.astype(BF16)


def _row_tile_of(bsz, n_t):
    n_x = n_t - 1
    return lambda b, i: (jnp.where(i == 0, bsz * n_x + b, b * n_x + i - 1), 0)


def _attn_proj(h, gain, w_qk, wv_t, wf_t, bf_col, *, bsz, n_pad, scale):
    d = h.shape[1]
    d_attn = wv_t.shape[0]
    n_heads = d_attn // HEAD_DIM
    tm = SEQ_TILE
    n_t = h.shape[0] // (bsz * tm)
    lp = n_t * tm
    kern = functools.partial(_attn_proj_kernel, tm=tm, n_heads=n_heads, n_pad=n_pad, scale=scale)
    aug_blk = pl.BlockSpec((1, tm, n_heads * AUG_DIM), lambda b, i: (b, i, 0))
    const = dict(pipeline_mode=pl.Buffered(1))
    return pl.pallas_call(
        kern,
        grid=(bsz, n_t),
        in_specs=[
            pl.BlockSpec((tm, d), _row_tile_of(bsz, n_t)),
            pl.BlockSpec((1, d), lambda b, i: (0, 0)),
            pl.BlockSpec((d, 2 * d_attn), lambda b, i: (0, 0), **const),
            pl.BlockSpec((d_attn, d), lambda b, i: (0, 0), **const),
            pl.BlockSpec((HEAD_PAD, d), lambda b, i: (0, 0)),
            pl.BlockSpec((HEAD_PAD, 1), lambda b, i: (0, 0)),
        ],
        out_specs=[
            aug_blk, aug_blk,
            pl.BlockSpec((1, d_attn, tm), lambda b, i: (b, 0, i)),
            pl.BlockSpec((1, 1, N_STAT, HEAD_PAD, LANES), lambda b, i: (b, i, 0, 0, 0)),
            pl.BlockSpec((tm, d), _row_tile_of(bsz, n_t)),
        ],
        out_shape=[
            jax.ShapeDtypeStruct((bsz, lp, n_heads * AUG_DIM), BF16),
            jax.ShapeDtypeStruct((bsz, lp, n_heads * AUG_DIM), BF16),
            jax.ShapeDtypeStruct((bsz, d_attn, lp), BF16),
            jax.ShapeDtypeStruct((bsz, n_t, N_STAT, HEAD_PAD, LANES), F32),
            jax.ShapeDtypeStruct(h.shape, BF16),
        ],
        scratch_shapes=[pltpu.VMEM((HEAD_PAD, LANES), F32)],
        compiler_params=_cparams(2),
        name="attn_proj",
    )(h, gain, w_qk, wv_t, wf_t, bf_col)


def _flash_kernel(c_ref, first_ref, q_ref, k_ref, vt_ref, g_ref, o_ref, acc_ref, sa_ref, sb_ref,
                  *, tq, n_q, hpb):
    b = pl.program_id(0)
    hp = pl.program_id(1)
    i = pl.program_id(2)
    heads = range(hpb)
    c_base = [(b * n_q) * HEAD_PAD + hp * hpb + hh for hh in heads]
    c_q = [c_ref[c_base[hh] + i * HEAD_PAD] for hh in heads]
    first = first_ref[(b * pl.num_programs(1) + hp) * n_q + i]
    n_unmasked = i - first
    last = first + jnp.maximum(n_unmasked - 1, 0)
    key = lax.broadcasted_iota(jnp.int32, (tq, tq), 0)
    qry = lax.broadcasted_iota(jnp.int32, (tq, tq), 1)
    ones_rows = jnp.ones((ONES_ROWS, tq), BF16)

    def scores_into(s_ref, j, masked=False):
        start = pl.multiple_of(j * tq, tq)
        col_max = []
        for hh in heads:
            cols = slice(hh * AUG_DIM, (hh + 1) * AUG_DIM)
            st = lax.dot_general(k_ref[0, pl.ds(start, tq), cols], q_ref[0, :, cols], _NT,
                                 preferred_element_type=F32)
            if masked:
                st = jnp.where(key <= qry, st, NEG)
            s_ref[hh] = st
            col_max.append(jnp.max(st, axis=0, keepdims=True))
        return tuple(col_max)

    def accumulate(s_ref, col_max, j, delta, m_all):
        start = pl.multiple_of(j * tq, tq)
        m_out = []
        for hh in heads:
            vt = vt_ref[0, hh * HEAD_DIM:(hh + 1) * HEAD_DIM, pl.ds(start, tq)]
            m_new = jnp.maximum(m_all[hh], col_max[hh] + delta[hh])
            alpha = jnp.exp2(m_all[hh] - m_new)
            p = jnp.exp2(s_ref[hh] - (m_new - delta[hh])).astype(BF16)
            pv = jnp.dot(jnp.concatenate([vt, ones_rows], axis=0), p,
                         preferred_element_type=F32)
            acc_ref[hh] = alpha * acc_ref[hh] + pv
            m_out.append(m_new)
        return tuple(m_out)

    def delta_of(j):
        return [c_q[hh] - c_ref[c_base[hh] + j * HEAD_PAD] for hh in heads]

    def tail_delta_of(j):
        real = delta_of(jnp.minimum(j, last))
        return [jnp.where(j < i, real[hh], FAKE) for hh in heads]

    acc_ref[...] = jnp.zeros_like(acc_ref)
    max_a = scores_into(sa_ref, first)
    m_all = tuple(jnp.full((1, tq), NEG, F32) for _ in heads)
    n_trips = jnp.maximum((n_unmasked - 1) // 2, 0)

    def trip(jj, carry):
        m_all, max_a = carry
        j0 = first + 2 * jj
        max_b = scores_into(sb_ref, j0 + 1)
        m_all = accumulate(sa_ref, max_a, j0, delta_of(j0), m_all)
        max_a = scores_into(sa_ref, j0 + 2)
        m_all = accumulate(sb_ref, max_b, j0 + 1, delta_of(j0 + 1), m_all)
        return m_all, max_a

    m_all, max_a = lax.fori_loop(0, n_trips, trip, (m_all, max_a))
    j0 = first + 2 * n_trips
    no_delta = [0.0 for _ in heads]

    @pl.when(n_unmasked - 2 * n_trips == 2)
    def _tail_two_unmasked():
        max_b = scores_into(sb_ref, j0 + 1)
        m_1 = accumulate(sa_ref, max_a, j0, delta_of(j0), m_all)
        max_d = scores_into(sa_ref, i, masked=True)
        m_2 = accumulate(sb_ref, max_b, j0 + 1, delta_of(j0 + 1), m_1)
        accumulate(sa_ref, max_d, i, no_delta, m_2)

    @pl.when(n_unmasked - 2 * n_trips < 2)
    def _tail_one_unmasked():
        max_d = scores_into(sb_ref, i, masked=True)
        m_1 = accumulate(sa_ref, max_a, jnp.minimum(j0, last), tail_delta_of(j0), m_all)
        accumulate(sb_ref, max_d, i, no_delta, m_1)

    for hh in heads:
        ot = acc_ref[hh, :HEAD_DIM, :] / acc_ref[hh, HEAD_DIM:HEAD_DIM + 1, :]
        ot = ot * _rms_scale(ot, axis=0)
        o_ref[0, :, hh * HEAD_DIM:(hh + 1) * HEAD_DIM] = (ot.T * g_ref[hh]).astype(BF16)


def _heads_per_block(n_heads):
    return 2 if n_heads % 2 == 0 else 1


def _first_key_tile(stats, n_heads):
    hpb = _heads_per_block(n_heads)
    st = stats[:, :, :, :n_heads, 0]
    c, bmax, qn, kn = (jnp.moveaxis(st[:, :, s], 1, 2) for s in
                       (STAT_C, STAT_BMAX, STAT_QN, STAT_KN))
    upper = (qn[..., :, None] * (kn[..., None, :] + kn[..., :, None]) * 1.001 + 1.0
             + (c[..., :, None] - c[..., None, :]) + bmax[..., None, :])
    n_t = c.shape[-1]
    earlier = jnp.arange(n_t)[None, :] < jnp.arange(n_t)[:, None]
    negligible = (upper < -SKIP_LOG2) & earlier
    first = jnp.min(jnp.where(negligible, n_t, jnp.arange(n_t)), axis=-1)
    first = first.reshape(first.shape[0], n_heads // hpb, hpb, n_t).min(axis=2)
    return first.astype(jnp.int32).reshape(-1)


def _flash(q, k, vt, c_tiles, first_tile, og_heads):
    bsz, lp, _ = q.shape
    n_heads = vt.shape[1] // HEAD_DIM
    hpb = _heads_per_block(n_heads)
    tq = SEQ_TILE
    n_q = lp // tq
    kern = functools.partial(_flash_kernel, tq=tq, n_q=n_q, hpb=hpb)
    resident = pl.Buffered(1)
    return pl.pallas_call(
        kern,
        grid=(bsz, n_heads // hpb, n_q),
        in_specs=[
            pl.BlockSpec(memory_space=pltpu.SMEM),
            pl.BlockSpec(memory_space=pltpu.SMEM),
            pl.BlockSpec((1, tq, hpb * AUG_DIM), lambda b, h, i: (b, i, h)),
            pl.BlockSpec((1, lp, hpb * AUG_DIM), lambda b, h, i: (b, 0, h),
                         pipeline_mode=resident),
            pl.BlockSpec((1, hpb * HEAD_DIM, lp), lambda b, h, i: (b, h, 0),
                         pipeline_mode=resident),
            pl.BlockSpec((hpb, 1, HEAD_DIM), lambda b, h, i: (h, 0, 0)),
        ],
        out_specs=pl.BlockSpec((1, tq, hpb * HEAD_DIM), lambda b, h, i: (b, i, h)),
        out_shape=jax.ShapeDtypeStruct((bsz, lp, n_heads * HEAD_DIM), BF16),
        scratch_shapes=[
            pltpu.VMEM((hpb, HEAD_DIM + ONES_ROWS, tq), F32),
            pltpu.VMEM((hpb, tq, tq), F32),
            pltpu.VMEM((hpb, tq, tq), F32),
        ],
        compiler_params=_cparams(3),
        name="flash",
    )(c_tiles, first_tile, q, k, vt, og_heads)


def _mix_out_kernel(h_ref, hn_ref, ya_ref, wc_ref, cw_ref, og_ref, wo_ref, o_ref, u_ref,
                    *, tm, n_groups, n_pad):
    i = pl.program_id(1)
    d_conv = n_groups * HEAD_DIM
    hn = hn_ref[...]

    @pl.when(i == 0)
    def _reset():
        u_ref[pl.ds(0, HALO), :] = jnp.zeros((HALO, d_conv), F32)

    gate_b = jnp.dot(hn, wc_ref[:, :d_conv], preferred_element_type=F32)
    gate_c = jnp.dot(hn, wc_ref[:, d_conv:2 * d_conv], preferred_element_type=F32)
    hc = jnp.dot(hn, wc_ref[:, 2 * d_conv:], preferred_element_type=F32)
    u = gate_c * hc
    u_ref[pl.ds(HALO, tm), :] = u
    u1 = u_ref[pl.ds(HALO - 1, tm), :]
    u2 = u_ref[pl.ds(HALO - 2, tm), :]
    conv = cw_ref[0:1, :] * u2 + cw_ref[1:2, :] * u1 + cw_ref[2:3, :] * u
    y = gate_b * conv
    u_ref[pl.ds(0, HALO), :] = u_ref[pl.ds(tm, HALO), :]
    yc = jnp.concatenate(
        [(y[:, g * HEAD_DIM:(g + 1) * HEAD_DIM] * _rms_scale(y[:, g * HEAD_DIM:(g + 1) * HEAD_DIM])
          * og_ref[:, g * HEAD_DIM:(g + 1) * HEAD_DIM]).astype(BF16) for g in range(n_groups)],
        axis=1)
    d_attn = ya_ref.shape[2]
    out = (jnp.dot(ya_ref[0], wo_ref[:d_attn, :], preferred_element_type=F32)
           + jnp.dot(yc, wo_ref[d_attn:, :], preferred_element_type=F32))
    out = h_ref[...] + out
    pos = i * tm + lax.broadcasted_iota(jnp.int32, (tm, 1), 0)
    o_ref[...] = jnp.where(pos >= n_pad, out, 0.0)


def _mix_out(h, hn, ya, w_conv, conv_w, og_conv, w_o, *, n_pad):
    d = h.shape[1]
    bsz, lp, d_attn = ya.shape
    d_conv = w_conv.shape[1] // 3
    tm = SEQ_TILE
    h_blk = pl.BlockSpec((tm, d), _row_tile_of(bsz, lp // tm))
    kern = functools.partial(_mix_out_kernel, tm=tm, n_groups=d_conv // HEAD_DIM, n_pad=n_pad)
    const = dict(pipeline_mode=pl.Buffered(1))
    return pl.pallas_call(
        kern,
        grid=(bsz, lp // tm),
        in_specs=[
            h_blk,
            h_blk,
            pl.BlockSpec((1, tm, d_attn), lambda b, i: (b, i, 0)),
            pl.BlockSpec((d, 3 * d_conv), lambda b, i: (0, 0), **const),
            pl.BlockSpec((HALO, d_conv), lambda b, i: (0, 0)),
            pl.BlockSpec((1, d_conv), lambda b, i: (0, 0)),
            pl.BlockSpec((d_attn + d_conv, d), lambda b, i: (0, 0), **const),
        ],
        out_specs=h_blk,
        out_shape=jax.ShapeDtypeStruct(h.shape, F32),
        scratch_shapes=[pltpu.VMEM((tm + HALO, d_conv), F32)],
        compiler_params=_cparams(2),
        name="mix_out",
    )(h, hn, ya, w_conv, conv_w, og_conv, w_o)


def _ffn_kernel(h_ref, g_ref, wg_ref, wu_ref, wd_ref, gf_ref, o_ref, hn_ref, *, final):
    f = pl.program_id(1)

    @pl.when(f == 0)
    def _norm():
        x = h_ref[...]
        hn_ref[...] = (x * _rms_scale(x) * g_ref[...]).astype(BF16)
        o_ref[...] = x

    hn = hn_ref[...]
    gate = jnp.dot(hn, wg_ref[...], preferred_element_type=F32)
    up = jnp.dot(hn, wu_ref[...], preferred_element_type=F32)
    act = (jax.nn.silu(gate) * up).astype(BF16)
    o_ref[...] += jnp.dot(act, wd_ref[...], preferred_element_type=F32)

    if final:
        @pl.when(f == pl.num_programs(1) - 1)
        def _final_norm():
            y = o_ref[...]
            o_ref[...] = y * _rms_scale(y) * gf_ref[...]


def _ffn(h, gain, w_gate, w_up, w_down, final_gain, *, rows, final):
    d = h.shape[1]
    d_ff = w_gate.shape[1]
    tm = 1024 if rows % 1024 == 0 else 512
    tf = 512 if d_ff % 512 == 0 else 256
    return pl.pallas_call(
        functools.partial(_ffn_kernel, final=final),
        grid=(rows // tm, d_ff // tf),
        in_specs=[
            pl.BlockSpec((tm, d), lambda r, f: (r, 0)),
            pl.BlockSpec((1, d), lambda r, f: (0, 0)),
            pl.BlockSpec((d, tf), lambda r, f: (0, f)),
            pl.BlockSpec((d, tf), lambda r, f: (0, f)),
            pl.BlockSpec((tf, d), lambda r, f: (f, 0)),
            pl.BlockSpec((1, d), lambda r, f: (0, 0)),
        ],
        out_specs=pl.BlockSpec((tm, d), lambda r, f: (r, 0)),
        out_shape=jax.ShapeDtypeStruct((rows, d), F32),
        scratch_shapes=[pltpu.VMEM((tm, d), BF16)],
        compiler_params=_cparams(2),
        name="ffn",
    )(h, gain, w_gate, w_up, w_down, final_gain)


def kernel(x, meta, norm_mix, w_in, b_f, conv_w, out_gain, w_out, norm_ffn, w_gate, w_up, w_down, final_norm):
    bsz, seq, d = x.shape
    n_meta = meta.shape[0]
    depth = w_in.shape[0]
    d_attn = d // 2
    n_heads = d_attn // HEAD_DIM
    assert n_heads <= HEAD_PAD and seq % SEQ_TILE == 0 and n_meta <= SEQ_TILE
    assert w_in.shape[2] == 6 * d_attn + n_heads
    lp = seq + SEQ_TILE
    n_pad = lp - seq - n_meta
    scale = HEAD_DIM ** -0.5

    m = jnp.broadcast_to(meta.astype(x.dtype)[None], (bsz, n_meta, d))
    head = jnp.concatenate([jnp.zeros((bsz, n_pad, d), x.dtype), m], axis=1)
    h = jnp.concatenate([x.reshape(bsz * seq, d), head.reshape(bsz * SEQ_TILE, d)], axis=0)
    final_gain = final_norm.reshape(1, d)

    for l in range(depth):
        w_qk = w_in[l, :, :2 * d_attn].astype(BF16)
        wv_t = w_in[l, :, 2 * d_attn:3 * d_attn].T.astype(BF16)
        w_conv = w_in[l, :, 3 * d_attn:6 * d_attn].astype(BF16)
        wf_t = jnp.zeros((HEAD_PAD, d), BF16).at[:n_heads].set(w_in[l, :, 6 * d_attn:].T.astype(BF16))
        bf_col = jnp.zeros((HEAD_PAD, 1), F32).at[:n_heads, 0].set(b_f[l])
        cw = jnp.zeros((HALO, d_attn), F32).at[:CONV_WIDTH].set(conv_w[l])
        og_heads = out_gain[l, :d_attn].reshape(n_heads, 1, HEAD_DIM)
        og_conv = out_gain[l, d_attn:].reshape(1, d_attn)

        gain = norm_mix[l].reshape(1, d)
        q, k, vt, stats, hn = _attn_proj(h, gain, w_qk, wv_t, wf_t, bf_col, bsz=bsz, n_pad=n_pad,
                                         scale=scale)
        ya = _flash(q, k, vt, stats[:, :, STAT_C, :, 0].reshape(-1),
                    _first_key_tile(stats, n_heads), og_heads)

        h = _mix_out(h, hn, ya, w_conv, cw, og_conv, w_out[l].astype(BF16), n_pad=n_pad)
        final = l == depth - 1
        h = _ffn(h, norm_ffn[l].reshape(1, d), w_gate[l].astype(BF16), w_up[l].astype(BF16),
                 w_down[l].astype(BF16), final_gain,
                 rows=bsz * seq if final else h.shape[0], final=final)

    return h.reshape(bsz, seq, d)
```

```python
import functools
import math

import jax
import jax.numpy as jnp
from jax import lax
from jax.experimental import pallas as pl
from jax.experimental.pallas import tpu as pltpu

F32 = jnp.float32
BF16 = jnp.bfloat16

HEAD_DIM = 128
AUG_DIM = 2 * HEAD_DIM
N_SPLIT = 3
CONV_WIDTH = 3
EPS = 1e-6
NEG = -1e30
FAKE = 2 * NEG
LOG2E = math.log2(math.e)
SEQ_TILE = 512
HEAD_PAD = 16
ONES_ROWS = 16
LANES = 128
STAT_C, STAT_BMAX, STAT_QN, STAT_KN, N_STAT = 0, 1, 2, 3, 4
SKIP_LOG2 = 140.0
HALO = 8
VMEM_LIMIT = 56 * 1024 * 1024

_NT = (((1,), (1,)), ((), ()))


def _cparams(n_axes):
    return pltpu.CompilerParams(dimension_semantics=("arbitrary",) * n_axes,
                                vmem_limit_bytes=VMEM_LIMIT)


def _rms_scale(x, axis=-1):
    return lax.rsqrt(jnp.mean(x * x, axis=axis, keepdims=True) + EPS)


def _cumsum_lanes(x):
    n = x.shape[1]
    lane = lax.broadcasted_iota(jnp.int32, x.shape, 1)
    shift = 1
    while shift < n:
        x = x + jnp.where(lane >= shift, pltpu.roll(x, shift, axis=1), 0.0)
        shift *= 2
    return x


def _round_bf16(x):
    return x.astype(BF16).astype(F32)


def _max_row_norm(x):
    xf = x.astype(F32)
    n2 = jnp.max(jnp.sum(xf * xf, axis=1, keepdims=True), axis=0, keepdims=True)
    return jnp.broadcast_to(jnp.sqrt(n2), (1, LANES))


def _attn_proj_kernel(h_ref, g_ref, wqk_ref, wvt_ref, wf_ref, bf_ref,
                      q_ref, k_ref, vt_ref, stat_ref, hn_ref, carry_ref,
                      *, tm, n_heads, n_pad, scale):
    i = pl.program_id(1)
    d_attn = n_heads * HEAD_DIM
    x = h_ref[...]
    hn = (x * _rms_scale(x) * g_ref[...]).astype(BF16)
    hn_ref[...] = hn

    @pl.when(i == 0)
    def _reset():
        carry_ref[...] = jnp.zeros_like(carry_ref)

    logit = lax.dot_general(wf_ref[...], hn, _NT, preferred_element_type=F32)
    log_f = jax.nn.log_sigmoid(logit + bf_ref[...])
    r = _cumsum_lanes(log_f)
    stat_ref[0, 0] = jnp.zeros(stat_ref.shape[2:], F32)
    stat_ref[0, 0, STAT_C] = carry_ref[...] * LOG2E
    carry_ref[...] = carry_ref[...] + jnp.broadcast_to(r[:, tm - 1:tm], carry_ref.shape)
    pos = i * tm + lax.broadcasted_iota(jnp.int32, r.shape, 1)
    bias = jnp.where(pos >= n_pad, -LOG2E * r, NEG)
    stat_ref[0, 0, STAT_BMAX] = jnp.broadcast_to(jnp.max(bias, axis=1, keepdims=True),
                                                 (HEAD_PAD, LANES))
    bias_col = jnp.concatenate([bias, jnp.zeros((LANES - HEAD_PAD, tm), F32)], axis=0).T

    lane = lax.broadcasted_iota(jnp.int32, (tm, HEAD_DIM), 1)
    ones = jnp.where(lane < N_SPLIT, 1.0, 0.0).astype(BF16)
    zq = jnp.dot(hn, wqk_ref[:, :d_attn], preferred_element_type=F32)
    for h in range(n_heads):
        qh = (zq[:, h * HEAD_DIM:(h + 1) * HEAD_DIM] * (scale * LOG2E)).astype(BF16)
        q_ref[0, :, h * AUG_DIM:h * AUG_DIM + HEAD_DIM] = qh
        q_ref[0, :, h * AUG_DIM + HEAD_DIM:(h + 1) * AUG_DIM] = ones
        stat_ref[0, 0, STAT_QN, h:h + 1, :] = _max_row_norm(qh)

    zk = jnp.dot(hn, wqk_ref[:, d_attn:], preferred_element_type=F32)
    for h in range(n_heads):
        kh = zk[:, h * HEAD_DIM:(h + 1) * HEAD_DIM].astype(BF16)
        k_ref[0, :, h * AUG_DIM:h * AUG_DIM + HEAD_DIM] = kh
        stat_ref[0, 0, STAT_KN, h:h + 1, :] = _max_row_norm(kh)
        xb = jnp.broadcast_to(bias_col[:, h:h + 1], (tm, HEAD_DIM))
        hi = _round_bf16(xb)
        mid = _round_bf16(xb - hi)
        lo = xb - hi - mid
        aug = jnp.where(lane == 0, hi, jnp.where(lane == 1, mid, jnp.where(lane == 2, lo, 0.0)))
        k_ref[0, :, h * AUG_DIM + HEAD_DIM:(h + 1) * AUG_DIM] = aug.astype(BF16)

    zvt = lax.dot_general(wvt_ref[...], hn, _NT, preferred_element_type=F32)
    vt_ref[0] = zvt.astype(BF16)


def _row_tile_of(bsz, n_t):
    n_x = n_t - 1
    return lambda b, i: (jnp.where(i == 0, bsz * n_x + b, b * n_x + i - 1), 0)


def _attn_proj(h, gain, w_qk, wv_t, wf_t, bf_col, *, bsz, n_pad, scale):
    d = h.shape[1]
    d_attn = wv_t.shape[0]
    n_heads = d_attn // HEAD_DIM
    tm = SEQ_TILE
    n_t = h.shape[0] // (bsz * tm)
    lp = n_t * tm
    kern = functools.partial(_attn_proj_kernel, tm=tm, n_heads=n_heads, n_pad=n_pad, scale=scale)
    aug_blk = pl.BlockSpec((1, tm, n_heads * AUG_DIM), lambda b, i: (b, i, 0))
    const = dict(pipeline_mode=pl.Buffered(1))
    return pl.pallas_call(
        kern,
        grid=(bsz, n_t),
        in_specs=[
            pl.BlockSpec((tm, d), _row_tile_of(bsz, n_t)),
            pl.BlockSpec((1, d), lambda b, i: (0, 0)),
            pl.BlockSpec((d, 2 * d_attn), lambda b, i: (0, 0), **const),
            pl.BlockSpec((d_attn, d), lambda b, i: (0, 0), **const),
            pl.BlockSpec((HEAD_PAD, d), lambda b, i: (0, 0)),
            pl.BlockSpec((HEAD_PAD, 1), lambda b, i: (0, 0)),
        ],
        out_specs=[
            aug_blk, aug_blk,
            pl.BlockSpec((1, d_attn, tm), lambda b, i: (b, 0, i)),
            pl.BlockSpec((1, 1, N_STAT, HEAD_PAD, LANES), lambda b, i: (b, i, 0, 0, 0)),
            pl.BlockSpec((tm, d), _row_tile_of(bsz, n_t)),
        ],
        out_shape=[
            jax.ShapeDtypeStruct((bsz, lp, n_heads * AUG_DIM), BF16),
            jax.ShapeDtypeStruct((bsz, lp, n_heads * AUG_DIM), BF16),
            jax.ShapeDtypeStruct((bsz, d_attn, lp), BF16),
            jax.ShapeDtypeStruct((bsz, n_t, N_STAT, HEAD_PAD, LANES), F32),
            jax.ShapeDtypeStruct(h.shape, BF16),
        ],
        scratch_shapes=[pltpu.VMEM((HEAD_PAD, LANES), F32)],
        compiler_params=_cparams(2),
        name="attn_proj",
    )(h, gain, w_qk, wv_t, wf_t, bf_col)


def _flash_kernel(c_ref, first_ref, q_ref, k_ref, vt_ref, g_ref, o_ref, acc_ref, sa_ref, sb_ref,
                  *, tq, n_q, hpb):
    b = pl.program_id(0)
    hp = pl.program_id(1)
    i = pl.program_id(2)
    heads = range(hpb)
    c_base = [(b * n_q) * HEAD_PAD + hp * hpb + hh for hh in heads]
    c_q = [c_ref[c_base[hh] + i * HEAD_PAD] for hh in heads]
    first = first_ref[(b * pl.num_programs(1) + hp) * n_q + i]
    n_unmasked = i - first
    last = first + jnp.maximum(n_unmasked - 1, 0)
    key = lax.broadcasted_iota(jnp.int32, (tq, tq), 0)
    qry = lax.broadcasted_iota(jnp.int32, (tq, tq), 1)
    ones_rows = jnp.ones((ONES_ROWS, tq), BF16)

    def scores_into(s_ref, j, masked=False):
        start = pl.multiple_of(j * tq, tq)
        col_max = []
        for hh in heads:
            cols = slice(hh * AUG_DIM, (hh + 1) * AUG_DIM)
            st = lax.dot_general(k_ref[0, pl.ds(start, tq), cols], q_ref[0, :, cols], _NT,
                                 preferred_element_type=F32)
            if masked:
                st = jnp.where(key <= qry, st, NEG)
            s_ref[hh] = st
            col_max.append(jnp.max(st, axis=0, keepdims=True))
        return tuple(col_max)

    def accumulate(s_ref, col_max, j, delta, m_all):
        start = pl.multiple_of(j * tq, tq)
        m_out = []
        for hh in heads:
            vt = vt_ref[0, hh * HEAD_DIM:(hh + 1) * HEAD_DIM, pl.ds(start, tq)]
            m_new = jnp.maximum(m_all[hh], col_max[hh] + delta[hh])
            alpha = jnp.exp2(m_all[hh] - m_new)
            p = jnp.exp2(s_ref[hh] - (m_new - delta[hh])).astype(BF16)
            pv = jnp.dot(jnp.concatenate([vt, ones_rows], axis=0), p,
                         preferred_element_type=F32)
            acc_ref[hh] = alpha * acc_ref[hh] + pv
            m_out.append(m_new)
        return tuple(m_out)

    def delta_of(j):
        return [c_q[hh] - c_ref[c_base[hh] + j * HEAD_PAD] for hh in heads]

    def tail_delta_of(j):
        real = delta_of(jnp.minimum(j, last))
        return [jnp.where(j < i, real[hh], FAKE) for hh in heads]

    acc_ref[...] = jnp.zeros_like(acc_ref)
    max_a = scores_into(sa_ref, first)
    m_all = tuple(jnp.full((1, tq), NEG, F32) for _ in heads)
    n_trips = jnp.maximum((n_unmasked - 1) // 2, 0)

    def trip(jj, carry):
        m_all, max_a = carry
        j0 = first + 2 * jj
        max_b = scores_into(sb_ref, j0 + 1)
        m_all = accumulate(sa_ref, max_a, j0, delta_of(j0), m_all)
        max_a = scores_into(sa_ref, j0 + 2)
        m_all = accumulate(sb_ref, max_b, j0 + 1, delta_of(j0 + 1), m_all)
        return m_all, max_a

    def two_trips(jj, carry):
        return trip(2 * jj + 1, trip(2 * jj, carry))

    carry = lax.fori_loop(0, n_trips // 2, two_trips, (m_all, max_a))
    m_all, max_a = lax.fori_loop(2 * (n_trips // 2), n_trips, trip, carry)
    j0 = first + 2 * n_trips
    no_delta = [0.0 for _ in heads]

    @pl.when(n_unmasked - 2 * n_trips == 2)
    def _tail_two_unmasked():
        max_b = scores_into(sb_ref, j0 + 1)
        m_1 = accumulate(sa_ref, max_a, j0, delta_of(j0), m_all)
        max_d = scores_into(sa_ref, i, masked=True)
        m_2 = accumulate(sb_ref, max_b, j0 + 1, delta_of(j0 + 1), m_1)
        accumulate(sa_ref, max_d, i, no_delta, m_2)

    @pl.when(n_unmasked - 2 * n_trips < 2)
    def _tail_one_unmasked():
        max_d = scores_into(sb_ref, i, masked=True)
        m_1 = accumulate(sa_ref, max_a, jnp.minimum(j0, last), tail_delta_of(j0), m_all)
        accumulate(sb_ref, max_d, i, no_delta, m_1)

    for hh in heads:
        ot = acc_ref[hh, :HEAD_DIM, :] / acc_ref[hh, HEAD_DIM:HEAD_DIM + 1, :]
        ot = ot * _rms_scale(ot, axis=0)
        o_ref[0, :, hh * HEAD_DIM:(hh + 1) * HEAD_DIM] = (ot.T * g_ref[hh]).astype(BF16)


def _heads_per_block(n_heads):
    return 2 if n_heads % 2 == 0 else 1


def _first_key_tile(stats, n_heads):
    hpb = _heads_per_block(n_heads)
    st = stats[:, :, :, :n_heads, 0]
    c, bmax, qn, kn = (jnp.moveaxis(st[:, :, s], 1, 2) for s in
                       (STAT_C, STAT_BMAX, STAT_QN, STAT_KN))
    upper = (qn[..., :, None] * (kn[..., None, :] + kn[..., :, None]) * 1.001 + 1.0
             + (c[..., :, None] - c[..., None, :]) + bmax[..., None, :])
    n_t = c.shape[-1]
    earlier = jnp.arange(n_t)[None, :] < jnp.arange(n_t)[:, None]
    negligible = (upper < -SKIP_LOG2) & earlier
    first = jnp.min(jnp.where(negligible, n_t, jnp.arange(n_t)), axis=-1)
    first = first.reshape(first.shape[0], n_heads // hpb, hpb, n_t).min(axis=2)
    return first.astype(jnp.int32).reshape(-1)


def _flash(q, k, vt, c_tiles, first_tile, og_heads):
    bsz, lp, _ = q.shape
    n_heads = vt.shape[1] // HEAD_DIM
    hpb = _heads_per_block(n_heads)
    tq = SEQ_TILE
    n_q = lp // tq
    kern = functools.partial(_flash_kernel, tq=tq, n_q=n_q, hpb=hpb)
    resident = pl.Buffered(1)
    return pl.pallas_call(
        kern,
        grid=(bsz, n_heads // hpb, n_q),
        in_specs=[
            pl.BlockSpec(memory_space=pltpu.SMEM),
            pl.BlockSpec(memory_space=pltpu.SMEM),
            pl.BlockSpec((1, tq, hpb * AUG_DIM), lambda b, h, i: (b, i, h)),
            pl.BlockSpec((1, lp, hpb * AUG_DIM), lambda b, h, i: (b, 0, h),
                         pipeline_mode=resident),
            pl.BlockSpec((1, hpb * HEAD_DIM, lp), lambda b, h, i: (b, h, 0),
                         pipeline_mode=resident),
            pl.BlockSpec((hpb, 1, HEAD_DIM), lambda b, h, i: (h, 0, 0)),
        ],
        out_specs=pl.BlockSpec((1, tq, hpb * HEAD_DIM), lambda b, h, i: (b, i, h)),
        out_shape=jax.ShapeDtypeStruct((bsz, lp, n_heads * HEAD_DIM), BF16),
        scratch_shapes=[
            pltpu.VMEM((hpb, HEAD_DIM + ONES_ROWS, tq), F32),
            pltpu.VMEM((hpb, tq, tq), F32),
            pltpu.VMEM((hpb, tq, tq), F32),
        ],
        compiler_params=_cparams(3),
        name="flash",
    )(c_tiles, first_tile, q, k, vt, og_heads)


def _mix_out_kernel(h_ref, hn_ref, ya_ref, wc_ref, cw_ref, og_ref, wo_ref, o_ref, u_ref,
                    *, tm, n_groups, n_pad):
    i = pl.program_id(1)
    d_conv = n_groups * HEAD_DIM
    hn = hn_ref[...]

    @pl.when(i == 0)
    def _reset():
        u_ref[pl.ds(0, HALO), :] = jnp.zeros((HALO, d_conv), F32)

    gate_b = jnp.dot(hn, wc_ref[:, :d_conv], preferred_element_type=F32)
    gate_c = jnp.dot(hn, wc_ref[:, d_conv:2 * d_conv], preferred_element_type=F32)
    hc = jnp.dot(hn, wc_ref[:, 2 * d_conv:], preferred_element_type=F32)
    u = gate_c * hc
    u_ref[pl.ds(HALO, tm), :] = u
    u1 = u_ref[pl.ds(HALO - 1, tm), :]
    u2 = u_ref[pl.ds(HALO - 2, tm), :]
    conv = cw_ref[0:1, :] * u2 + cw_ref[1:2, :] * u1 + cw_ref[2:3, :] * u
    y = gate_b * conv
    u_ref[pl.ds(0, HALO), :] = u_ref[pl.ds(tm, HALO), :]
    yc = jnp.concatenate(
        [(y[:, g * HEAD_DIM:(g + 1) * HEAD_DIM] * _rms_scale(y[:, g * HEAD_DIM:(g + 1) * HEAD_DIM])
          * og_ref[:, g * HEAD_DIM:(g + 1) * HEAD_DIM]).astype(BF16) for g in range(n_groups)],
        axis=1)
    d_attn = ya_ref.shape[2]
    out = (jnp.dot(ya_ref[0], wo_ref[:d_attn, :], preferred_element_type=F32)
           + jnp.dot(yc, wo_ref[d_attn:, :], preferred_element_type=F32))
    out = h_ref[...] + out
    pos = i * tm + lax.broadcasted_iota(jnp.int32, (tm, 1), 0)
    o_ref[...] = jnp.where(pos >= n_pad, out, 0.0)


def _mix_out(h, hn, ya, w_conv, conv_w, og_conv, w_o, *, n_pad):
    d = h.shape[1]
    bsz, lp, d_attn = ya.shape
    d_conv = w_conv.shape[1] // 3
    tm = SEQ_TILE
    h_blk = pl.BlockSpec((tm, d), _row_tile_of(bsz, lp // tm))
    kern = functools.partial(_mix_out_kernel, tm=tm, n_groups=d_conv // HEAD_DIM, n_pad=n_pad)
    const = dict(pipeline_mode=pl.Buffered(1))
    return pl.pallas_call(
        kern,
        grid=(bsz, lp // tm),
        in_specs=[
            h_blk,
            h_blk,
            pl.BlockSpec((1, tm, d_attn), lambda b, i: (b, i, 0)),
            pl.BlockSpec((d, 3 * d_conv), lambda b, i: (0, 0), **const),
            pl.BlockSpec((HALO, d_conv), lambda b, i: (0, 0)),
            pl.BlockSpec((1, d_conv), lambda b, i: (0, 0)),
            pl.BlockSpec((d_attn + d_conv, d), lambda b, i: (0, 0), **const),
        ],
        out_specs=h_blk,
        out_shape=jax.ShapeDtypeStruct(h.shape, F32),
        scratch_shapes=[pltpu.VMEM((tm + HALO, d_conv), F32)],
        compiler_params=_cparams(2),
        name="mix_out",
    )(h, hn, ya, w_conv, conv_w, og_conv, w_o)


def _ffn_kernel(h_ref, g_ref, wg_ref, wu_ref, wd_ref, gf_ref, o_ref, hn_ref, *, final):
    f = pl.program_id(1)

    @pl.when(f == 0)
    def _norm():
        x = h_ref[...]
        hn_ref[...] = (x * _rms_scale(x) * g_ref[...]).astype(BF16)
        o_ref[...] = x

    hn = hn_ref[...]
    gate = jnp.dot(hn, wg_ref[...], preferred_element_type=F32)
    up = jnp.dot(hn, wu_ref[...], preferred_element_type=F32)
    act = (jax.nn.silu(gate) * up).astype(BF16)
    o_ref[...] += jnp.dot(act, wd_ref[...], preferred_element_type=F32)

    if final:
        @pl.when(f == pl.num_programs(1) - 1)
        def _final_norm():
            y = o_ref[...]
            o_ref[...] = y * _rms_scale(y) * gf_ref[...]


def _ffn(h, gain, w_gate, w_up, w_down, final_gain, *, rows, final):
    d = h.shape[1]
    d_ff = w_gate.shape[1]
    tm = 1024 if rows % 1024 == 0 else 512
    tf = 512 if d_ff % 512 == 0 else 256
    return pl.pallas_call(
        functools.partial(_ffn_kernel, final=final),
        grid=(rows // tm, d_ff // tf),
        in_specs=[
            pl.BlockSpec((tm, d), lambda r, f: (r, 0)),
            pl.BlockSpec((1, d), lambda r, f: (0, 0)),
            pl.BlockSpec((d, tf), lambda r, f: (0, f)),
            pl.BlockSpec((d, tf), lambda r, f: (0, f)),
            pl.BlockSpec((tf, d), lambda r, f: (f, 0)),
            pl.BlockSpec((1, d), lambda r, f: (0, 0)),
        ],
        out_specs=pl.BlockSpec((tm, d), lambda r, f: (r, 0)),
        out_shape=jax.ShapeDtypeStruct((rows, d), F32),
        scratch_shapes=[pltpu.VMEM((tm, d), BF16)],
        compiler_params=_cparams(2),
        name="ffn",
    )(h, gain, w_gate, w_up, w_down, final_gain)


def kernel(x, meta, norm_mix, w_in, b_f, conv_w, out_gain, w_out, norm_ffn, w_gate, w_up, w_down, final_norm):
    bsz, seq, d = x.shape
    n_meta = meta.shape[0]
    depth = w_in.shape[0]
    d_attn = d // 2
    n_heads = d_attn // HEAD_DIM
    assert n_heads <= HEAD_PAD and seq % SEQ_TILE == 0 and n_meta <= SEQ_TILE
    assert w_in.shape[2] == 6 * d_attn + n_heads
    lp = seq + SEQ_TILE
    n_pad = lp - seq - n_meta
    scale = HEAD_DIM ** -0.5

    m = jnp.broadcast_to(meta.astype(x.dtype)[None], (bsz, n_meta, d))
    head = jnp.concatenate([jnp.zeros((bsz, n_pad, d), x.dtype), m], axis=1)
    h = jnp.concatenate([x.reshape(bsz * seq, d), head.reshape(bsz * SEQ_TILE, d)], axis=0)
    final_gain = final_norm.reshape(1, d)

    for l in range(depth):
        w_qk = w_in[l, :, :2 * d_attn].astype(BF16)
        wv_t = w_in[l, :, 2 * d_attn:3 * d_attn].T.astype(BF16)
        w_conv = w_in[l, :, 3 * d_attn:6 * d_attn].astype(BF16)
        wf_t = jnp.zeros((HEAD_PAD, d), BF16).at[:n_heads].set(w_in[l, :, 6 * d_attn:].T.astype(BF16))
        bf_col = jnp.zeros((HEAD_PAD, 1), F32).at[:n_heads, 0].set(b_f[l])
        cw = jnp.zeros((HALO, d_attn), F32).at[:CONV_WIDTH].set(conv_w[l])
        og_heads = out_gain[l, :d_attn].reshape(n_heads, 1, HEAD_DIM)
        og_conv = out_gain[l, d_attn:].reshape(1, d_attn)

        gain = norm_mix[l].reshape(1, d)
        q, k, vt, stats, hn = _attn_proj(h, gain, w_qk, wv_t, wf_t, bf_col, bsz=bsz, n_pad=n_pad,
                                         scale=scale)
        ya = _flash(q, k, vt, stats[:, :, STAT_C, :, 0].reshape(-1),
                    _first_key_tile(stats, n_heads), og_heads)

        h = _mix_out(h, hn, ya, w_conv, cw, og_conv, w_out[l].astype(BF16), n_pad=n_pad)
        final = l == depth - 1
        h = _ffn(h, norm_ffn[l].reshape(1, d), w_gate[l].astype(BF16), w_up[l].astype(BF16),
                 w_down[l].astype(BF16), final_gain,
                 rows=bsz * seq if final else h.shape[0], final=final)

    return h.reshape(bsz, seq, d)
```

```python
import functools
import math

import jax
import jax.numpy as jnp
from jax import lax
from jax.experimental import pallas as pl
from jax.experimental.pallas import tpu as pltpu

F32 = jnp.float32
BF16 = jnp.bfloat16

HEAD_DIM = 128
AUG_DIM = 2 * HEAD_DIM
N_SPLIT = 3
CONV_WIDTH = 3
EPS = 1e-6
NEG = -1e30
FAKE = 2 * NEG
LOG2E = math.log2(math.e)
SEQ_TILE = 512
HEAD_PAD = 16
ONES_ROWS = 16
LANES = 128
STAT_C, STAT_BMAX, STAT_QN, STAT_KN, N_STAT = 0, 1, 2, 3, 4
SKIP_LOG2 = 140.0
HALO = 8
VMEM_LIMIT = 60 * 1024 * 1024

_NT = (((1,), (1,)), ((), ()))


def _cparams(n_axes):
    return pltpu.CompilerParams(dimension_semantics=("arbitrary",) * n_axes,
                                vmem_limit_bytes=VMEM_LIMIT)


def _rms_scale(x, axis=-1):
    return lax.rsqrt(jnp.mean(x * x, axis=axis, keepdims=True) + EPS)


def _cumsum_lanes(x):
    n = x.shape[1]
    lane = lax.broadcasted_iota(jnp.int32, x.shape, 1)
    shift = 1
    while shift < n:
        x = x + jnp.where(lane >= shift, pltpu.roll(x, shift, axis=1), 0.0)
        shift *= 2
    return x


def _round_bf16(x):
    return x.astype(BF16).astype(F32)


def _max_row_norm(x):
    xf = x.astype(F32)
    n2 = jnp.max(jnp.sum(xf * xf, axis=1, keepdims=True), axis=0, keepdims=True)
    return jnp.broadcast_to(jnp.sqrt(n2), (1, LANES))


def _attn_proj_kernel(h_ref, g_ref, wqk_ref, wvt_ref, wf_ref, bf_ref,
                      q_ref, k_ref, vt_ref, stat_ref, hn_ref, carry_ref,
                      *, tm, n_heads, n_pad, scale):
    i = pl.program_id(1)
    d_attn = n_heads * HEAD_DIM
    x = h_ref[...]
    hn = (x * _rms_scale(x) * g_ref[...]).astype(BF16)
    hn_ref[...] = hn

    @pl.when(i == 0)
    def _reset():
        carry_ref[...] = jnp.zeros_like(carry_ref)

    logit = lax.dot_general(wf_ref[...], hn, _NT, preferred_element_type=F32)
    log_f = jax.nn.log_sigmoid(logit + bf_ref[...])
    r = _cumsum_lanes(log_f)
    stat_ref[0, 0] = jnp.zeros(stat_ref.shape[2:], F32)
    stat_ref[0, 0, STAT_C] = carry_ref[...] * LOG2E
    carry_ref[...] = carry_ref[...] + jnp.broadcast_to(r[:, tm - 1:tm], carry_ref.shape)
    pos = i * tm + lax.broadcasted_iota(jnp.int32, r.shape, 1)
    bias = jnp.where(pos >= n_pad, -LOG2E * r, NEG)
    stat_ref[0, 0, STAT_BMAX] = jnp.broadcast_to(jnp.max(bias, axis=1, keepdims=True),
                                                 (HEAD_PAD, LANES))
    bias_col = jnp.concatenate([bias, jnp.zeros((LANES - HEAD_PAD, tm), F32)], axis=0).T

    lane = lax.broadcasted_iota(jnp.int32, (tm, HEAD_DIM), 1)
    ones = jnp.where(lane < N_SPLIT, 1.0, 0.0).astype(BF16)
    zq = jnp.dot(hn, wqk_ref[:, :d_attn], preferred_element_type=F32)
    for h in range(n_heads):
        qh = (zq[:, h * HEAD_DIM:(h + 1) * HEAD_DIM] * (scale * LOG2E)).astype(BF16)
        q_ref[0, :, h * AUG_DIM:h * AUG_DIM + HEAD_DIM] = qh
        q_ref[0, :, h * AUG_DIM + HEAD_DIM:(h + 1) * AUG_DIM] = ones
        stat_ref[0, 0, STAT_QN, h:h + 1, :] = _max_row_norm(qh)

    zk = jnp.dot(hn, wqk_ref[:, d_attn:], preferred_element_type=F32)
    for h in range(n_heads):
        kh = zk[:, h * HEAD_DIM:(h + 1) * HEAD_DIM].astype(BF16)
        k_ref[0, :, h * AUG_DIM:h * AUG_DIM + HEAD_DIM] = kh
        stat_ref[0, 0, STAT_KN, h:h + 1, :] = _max_row_norm(kh)
        xb = jnp.broadcast_to(bias_col[:, h:h + 1], (tm, HEAD_DIM))
        hi = _round_bf16(xb)
        mid = _round_bf16(xb - hi)
        lo = xb - hi - mid
        aug = jnp.where(lane == 0, hi, jnp.where(lane == 1, mid, jnp.where(lane == 2, lo, 0.0)))
        k_ref[0, :, h * AUG_DIM + HEAD_DIM:(h + 1) * AUG_DIM] = aug.astype(BF16)

    zvt = lax.dot_general(wvt_ref[...], hn, _NT, preferred_element_type=F32)
    vt_ref[0] = zvt.astype(BF16)


def _row_tile_of(bsz, n_t):
    n_x = n_t - 1
    return lambda b, i: (jnp.where(i == 0, bsz * n_x + b, b * n_x + i - 1), 0)


def _attn_proj(h, gain, w_qk, wv_t, wf_t, bf_col, *, bsz, n_pad, scale):
    d = h.shape[1]
    d_attn = wv_t.shape[0]
    n_heads = d_attn // HEAD_DIM
    tm = SEQ_TILE
    n_t = h.shape[0] // (bsz * tm)
    lp = n_t * tm
    kern = functools.partial(_attn_proj_kernel, tm=tm, n_heads=n_heads, n_pad=n_pad, scale=scale)
    aug_blk = pl.BlockSpec((1, tm, n_heads * AUG_DIM), lambda b, i: (b, i, 0))
    const = dict(pipeline_mode=pl.Buffered(1))
    return pl.pallas_call(
        kern,
        grid=(bsz, n_t),
        in_specs=[
            pl.BlockSpec((tm, d), _row_tile_of(bsz, n_t)),
            pl.BlockSpec((1, d), lambda b, i: (0, 0)),
            pl.BlockSpec((d, 2 * d_attn), lambda b, i: (0, 0), **const),
            pl.BlockSpec((d_attn, d), lambda b, i: (0, 0), **const),
            pl.BlockSpec((HEAD_PAD, d), lambda b, i: (0, 0)),
            pl.BlockSpec((HEAD_PAD, 1), lambda b, i: (0, 0)),
        ],
        out_specs=[
            aug_blk, aug_blk,
            pl.BlockSpec((1, d_attn, tm), lambda b, i: (b, 0, i)),
            pl.BlockSpec((1, 1, N_STAT, HEAD_PAD, LANES), lambda b, i: (b, i, 0, 0, 0)),
            pl.BlockSpec((tm, d), _row_tile_of(bsz, n_t)),
        ],
        out_shape=[
            jax.ShapeDtypeStruct((bsz, lp, n_heads * AUG_DIM), BF16),
            jax.ShapeDtypeStruct((bsz, lp, n_heads * AUG_DIM), BF16),
            jax.ShapeDtypeStruct((bsz, d_attn, lp), BF16),
            jax.ShapeDtypeStruct((bsz, n_t, N_STAT, HEAD_PAD, LANES), F32),
            jax.ShapeDtypeStruct(h.shape, BF16),
        ],
        scratch_shapes=[pltpu.VMEM((HEAD_PAD, LANES), F32)],
        compiler_params=_cparams(2),
        name="attn_proj",
    )(h, gain, w_qk, wv_t, wf_t, bf_col)


def _flash_kernel(c_ref, first_ref, q_ref, k_ref, vt_ref, g_ref, o_ref, acc_ref, sa_ref, sb_ref,
                  *, tq, n_q, hpb):
    b = pl.program_id(0)
    hp = pl.program_id(1)
    i = pl.program_id(2)
    heads = range(hpb)
    c_base = [(b * n_q) * HEAD_PAD + hp * hpb + hh for hh in heads]
    c_q = [c_ref[c_base[hh] + i * HEAD_PAD] for hh in heads]
    first = first_ref[(b * pl.num_programs(1) + hp) * n_q + i]
    n_unmasked = i - first
    last = first + jnp.maximum(n_unmasked - 1, 0)
    key = lax.broadcasted_iota(jnp.int32, (tq, tq), 0)
    qry = lax.broadcasted_iota(jnp.int32, (tq, tq), 1)
    ones_rows = jnp.ones((ONES_ROWS, tq), BF16)

    def scores_into(s_ref, j, masked=False):
        start = pl.multiple_of(j * tq, tq)
        col_max = []
        for hh in heads:
            cols = slice(hh * AUG_DIM, (hh + 1) * AUG_DIM)
            st = lax.dot_general(k_ref[0, pl.ds(start, tq), cols], q_ref[0, :, cols], _NT,
                                 preferred_element_type=F32)
            if masked:
                st = jnp.where(key <= qry, st, NEG)
            s_ref[hh] = st
            col_max.append(jnp.max(st, axis=0, keepdims=True))
        return tuple(col_max)

    def accumulate(s_ref, col_max, j, delta, m_all):
        start = pl.multiple_of(j * tq, tq)
        m_out = []
        for hh in heads:
            vt = vt_ref[0, hh * HEAD_DIM:(hh + 1) * HEAD_DIM, pl.ds(start, tq)]
            m_new = jnp.maximum(m_all[hh], col_max[hh] + delta[hh])
            alpha = jnp.exp2(m_all[hh] - m_new)
            p = jnp.exp2(s_ref[hh] - (m_new - delta[hh])).astype(BF16)
            pv = jnp.dot(jnp.concatenate([vt, ones_rows], axis=0), p,
                         preferred_element_type=F32)
            acc_ref[hh] = alpha * acc_ref[hh] + pv
            m_out.append(m_new)
        return tuple(m_out)

    def delta_of(j):
        return [c_q[hh] - c_ref[c_base[hh] + j * HEAD_PAD] for hh in heads]

    def tail_delta_of(j):
        real = delta_of(jnp.minimum(j, last))
        return [jnp.where(j < i, real[hh], FAKE) for hh in heads]

    acc_ref[...] = jnp.zeros_like(acc_ref)
    max_a = scores_into(sa_ref, first)
    m_all = tuple(jnp.full((1, tq), NEG, F32) for _ in heads)
    n_trips = jnp.maximum((n_unmasked - 1) // 2, 0)

    def trip(jj, carry):
        m_all, max_a = carry
        j0 = first + 2 * jj
        max_b = scores_into(sb_ref, j0 + 1)
        m_all = accumulate(sa_ref, max_a, j0, delta_of(j0), m_all)
        max_a = scores_into(sa_ref, j0 + 2)
        m_all = accumulate(sb_ref, max_b, j0 + 1, delta_of(j0 + 1), m_all)
        return m_all, max_a

    def two_trips(jj, carry):
        return trip(2 * jj + 1, trip(2 * jj, carry))

    carry = lax.fori_loop(0, n_trips // 2, two_trips, (m_all, max_a))
    m_all, max_a = lax.fori_loop(2 * (n_trips // 2), n_trips, trip, carry)
    j0 = first + 2 * n_trips
    no_delta = [0.0 for _ in heads]

    @pl.when(n_unmasked - 2 * n_trips == 2)
    def _tail_two_unmasked():
        max_b = scores_into(sb_ref, j0 + 1)
        m_1 = accumulate(sa_ref, max_a, j0, delta_of(j0), m_all)
        max_d = scores_into(sa_ref, i, masked=True)
        m_2 = accumulate(sb_ref, max_b, j0 + 1, delta_of(j0 + 1), m_1)
        accumulate(sa_ref, max_d, i, no_delta, m_2)

    @pl.when(n_unmasked - 2 * n_trips < 2)
    def _tail_one_unmasked():
        max_d = scores_into(sb_ref, i, masked=True)
        m_1 = accumulate(sa_ref, max_a, jnp.minimum(j0, last), tail_delta_of(j0), m_all)
        accumulate(sb_ref, max_d, i, no_delta, m_1)

    for hh in heads:
        ot = acc_ref[hh, :HEAD_DIM, :] / acc_ref[hh, HEAD_DIM:HEAD_DIM + 1, :]
        ot = ot * _rms_scale(ot, axis=0)
        o_ref[0, :, hh * HEAD_DIM:(hh + 1) * HEAD_DIM] = (ot.T * g_ref[hh]).astype(BF16)


def _heads_per_block(n_heads):
    return 2 if n_heads % 2 == 0 else 1


def _first_key_tile(stats, n_heads):
    hpb = _heads_per_block(n_heads)
    st = stats[:, :, :, :n_heads, 0]
    c, bmax, qn, kn = (jnp.moveaxis(st[:, :, s], 1, 2) for s in
                       (STAT_C, STAT_BMAX, STAT_QN, STAT_KN))
    upper = (qn[..., :, None] * (kn[..., None, :] + kn[..., :, None]) * 1.001 + 1.0
             + (c[..., :, None] - c[..., None, :]) + bmax[..., None, :])
    n_t = c.shape[-1]
    earlier = jnp.arange(n_t)[None, :] < jnp.arange(n_t)[:, None]
    negligible = (upper < -SKIP_LOG2) & earlier
    first = jnp.min(jnp.where(negligible, n_t, jnp.arange(n_t)), axis=-1)
    first = first.reshape(first.shape[0], n_heads // hpb, hpb, n_t).min(axis=2)
    return first.astype(jnp.int32).reshape(-1)


def _flash(q, k, vt, c_tiles, first_tile, og_heads):
    bsz, lp, _ = q.shape
    n_heads = vt.shape[1] // HEAD_DIM
    hpb = _heads_per_block(n_heads)
    tq = SEQ_TILE
    n_q = lp // tq
    kern = functools.partial(_flash_kernel, tq=tq, n_q=n_q, hpb=hpb)
    resident = pl.Buffered(2)
    return pl.pallas_call(
        kern,
        grid=(bsz, n_heads // hpb, n_q),
        in_specs=[
            pl.BlockSpec(memory_space=pltpu.SMEM),
            pl.BlockSpec(memory_space=pltpu.SMEM),
            pl.BlockSpec((1, tq, hpb * AUG_DIM), lambda b, h, i: (b, i, h)),
            pl.BlockSpec((1, lp, hpb * AUG_DIM), lambda b, h, i: (b, 0, h),
                         pipeline_mode=resident),
            pl.BlockSpec((1, hpb * HEAD_DIM, lp), lambda b, h, i: (b, h, 0),
                         pipeline_mode=resident),
            pl.BlockSpec((hpb, 1, HEAD_DIM), lambda b, h, i: (h, 0, 0)),
        ],
        out_specs=pl.BlockSpec((1, tq, hpb * HEAD_DIM), lambda b, h, i: (b, i, h)),
        out_shape=jax.ShapeDtypeStruct((bsz, lp, n_heads * HEAD_DIM), BF16),
        scratch_shapes=[
            pltpu.VMEM((hpb, HEAD_DIM + ONES_ROWS, tq), F32),
            pltpu.VMEM((hpb, tq, tq), F32),
            pltpu.VMEM((hpb, tq, tq), F32),
        ],
        compiler_params=_cparams(3),
        name="flash",
    )(c_tiles, first_tile, q, k, vt, og_heads)


def _mix_out_kernel(h_ref, hn_ref, ya_ref, wc_ref, cw_ref, og_ref, wo_ref, o_ref, u_ref,
                    *, tm, n_groups, n_pad):
    i = pl.program_id(1)
    d_conv = n_groups * HEAD_DIM
    hn = hn_ref[...]

    @pl.when(i == 0)
    def _reset():
        u_ref[pl.ds(0, HALO), :] = jnp.zeros((HALO, d_conv), F32)

    gate_b = jnp.dot(hn, wc_ref[:, :d_conv], preferred_element_type=F32)
    gate_c = jnp.dot(hn, wc_ref[:, d_conv:2 * d_conv], preferred_element_type=F32)
    hc = jnp.dot(hn, wc_ref[:, 2 * d_conv:], preferred_element_type=F32)
    u = gate_c * hc
    u_ref[pl.ds(HALO, tm), :] = u
    u1 = u_ref[pl.ds(HALO - 1, tm), :]
    u2 = u_ref[pl.ds(HALO - 2, tm), :]
    conv = cw_ref[0:1, :] * u2 + cw_ref[1:2, :] * u1 + cw_ref[2:3, :] * u
    y = gate_b * conv
    u_ref[pl.ds(0, HALO), :] = u_ref[pl.ds(tm, HALO), :]
    yc = jnp.concatenate(
        [(y[:, g * HEAD_DIM:(g + 1) * HEAD_DIM] * _rms_scale(y[:, g * HEAD_DIM:(g + 1) * HEAD_DIM])
          * og_ref[:, g * HEAD_DIM:(g + 1) * HEAD_DIM]).astype(BF16) for g in range(n_groups)],
        axis=1)
    d_attn = ya_ref.shape[2]
    out = (jnp.dot(ya_ref[0], wo_ref[:d_attn, :], preferred_element_type=F32)
           + jnp.dot(yc, wo_ref[d_attn:, :], preferred_element_type=F32))
    out = h_ref[...] + out
    pos = i * tm + lax.broadcasted_iota(jnp.int32, (tm, 1), 0)
    o_ref[...] = jnp.where(pos >= n_pad, out, 0.0)


def _mix_out(h, hn, ya, w_conv, conv_w, og_conv, w_o, *, n_pad):
    d = h.shape[1]
    bsz, lp, d_attn = ya.shape
    d_conv = w_conv.shape[1] // 3
    tm = SEQ_TILE
    h_blk = pl.BlockSpec((tm, d), _row_tile_of(bsz, lp // tm))
    kern = functools.partial(_mix_out_kernel, tm=tm, n_groups=d_conv // HEAD_DIM, n_pad=n_pad)
    const = dict(pipeline_mode=pl.Buffered(1))
    return pl.pallas_call(
        kern,
        grid=(bsz, lp // tm),
        in_specs=[
            h_blk,
            h_blk,
            pl.BlockSpec((1, tm, d_attn), lambda b, i: (b, i, 0)),
            pl.BlockSpec((d, 3 * d_conv), lambda b, i: (0, 0), **const),
            pl.BlockSpec((HALO, d_conv), lambda b, i: (0, 0)),
            pl.BlockSpec((1, d_conv), lambda b, i: (0, 0)),
            pl.BlockSpec((d_attn + d_conv, d), lambda b, i: (0, 0), **const),
        ],
        out_specs=h_blk,
        out_shape=jax.ShapeDtypeStruct(h.shape, F32),
        scratch_shapes=[pltpu.VMEM((tm + HALO, d_conv), F32)],
        compiler_params=_cparams(2),
        name="mix_out",
    )(h, hn, ya, w_conv, conv_w, og_conv, w_o)


def _ffn_kernel(h_ref, g_ref, wg_ref, wu_ref, wd_ref, gf_ref, o_ref, hn_ref, *, final):
    f = pl.program_id(1)

    @pl.when(f == 0)
    def _norm():
        x = h_ref[...]
        hn_ref[...] = (x * _rms_scale(x) * g_ref[...]).astype(BF16)
        o_ref[...] = x

    hn = hn_ref[...]
    gate = jnp.dot(hn, wg_ref[...], preferred_element_type=F32)
    up = jnp.dot(hn, wu_ref[...], preferred_element_type=F32)
    act = (jax.nn.silu(gate) * up).astype(BF16)
    o_ref[...] += jnp.dot(act, wd_ref[...], preferred_element_type=F32)

    if final:
        @pl.when(f == pl.num_programs(1) - 1)
        def _final_norm():
            y = o_ref[...]
            o_ref[...] = y * _rms_scale(y) * gf_ref[...]


def _ffn(h, gain, w_gate, w_up, w_down, final_gain, *, rows, final):
    d = h.shape[1]
    d_ff = w_gate.shape[1]
    tm = 1024 if rows % 1024 == 0 else 512
    tf = 512 if d_ff % 512 == 0 else 256
    return pl.pallas_call(
        functools.partial(_ffn_kernel, final=final),
        grid=(rows // tm, d_ff // tf),
        in_specs=[
            pl.BlockSpec((tm, d), lambda r, f: (r, 0)),
            pl.BlockSpec((1, d), lambda r, f: (0, 0)),
            pl.BlockSpec((d, tf), lambda r, f: (0, f)),
            pl.BlockSpec((d, tf), lambda r, f: (0, f)),
            pl.BlockSpec((tf, d), lambda r, f: (f, 0)),
            pl.BlockSpec((1, d), lambda r, f: (0, 0)),
        ],
        out_specs=pl.BlockSpec((tm, d), lambda r, f: (r, 0)),
        out_shape=jax.ShapeDtypeStruct((rows, d), F32),
        scratch_shapes=[pltpu.VMEM((tm, d), BF16)],
        compiler_params=_cparams(2),
        name="ffn",
    )(h, gain, w_gate, w_up, w_down, final_gain)


def kernel(x, meta, norm_mix, w_in, b_f, conv_w, out_gain, w_out, norm_ffn, w_gate, w_up, w_down, final_norm):
    bsz, seq, d = x.shape
    n_meta = meta.shape[0]
    depth = w_in.shape[0]
    d_attn = d // 2
    n_heads = d_attn // HEAD_DIM
    assert n_heads <= HEAD_PAD and seq % SEQ_TILE == 0 and n_meta <= SEQ_TILE
    assert w_in.shape[2] == 6 * d_attn + n_heads
    lp = seq + SEQ_TILE
    n_pad = lp - seq - n_meta
    scale = HEAD_DIM ** -0.5

    m = jnp.broadcast_to(meta.astype(x.dtype)[None], (bsz, n_meta, d))
    head = jnp.concatenate([jnp.zeros((bsz, n_pad, d), x.dtype), m], axis=1)
    h = jnp.concatenate([x.reshape(bsz * seq, d), head.reshape(bsz * SEQ_TILE, d)], axis=0)
    final_gain = final_norm.reshape(1, d)

    for l in range(depth):
        w_qk = w_in[l, :, :2 * d_attn].astype(BF16)
        wv_t = w_in[l, :, 2 * d_attn:3 * d_attn].T.astype(BF16)
        w_conv = w_in[l, :, 3 * d_attn:6 * d_attn].astype(BF16)
        wf_t = jnp.zeros((HEAD_PAD, d), BF16).at[:n_heads].set(w_in[l, :, 6 * d_attn:].T.astype(BF16))
        bf_col = jnp.zeros((HEAD_PAD, 1), F32).at[:n_heads, 0].set(b_f[l])
        cw = jnp.zeros((HALO, d_attn), F32).at[:CONV_WIDTH].set(conv_w[l])
        og_heads = out_gain[l, :d_attn].reshape(n_heads, 1, HEAD_DIM)
        og_conv = out_gain[l, d_attn:].reshape(1, d_attn)

        gain = norm_mix[l].reshape(1, d)
        q, k, vt, stats, hn = _attn_proj(h, gain, w_qk, wv_t, wf_t, bf_col, bsz=bsz, n_pad=n_pad,
                                         scale=scale)
        ya = _flash(q, k, vt, stats[:, :, STAT_C, :, 0].reshape(-1),
                    _first_key_tile(stats, n_heads), og_heads)

        h = _mix_out(h, hn, ya, w_conv, cw, og_conv, w_out[l].astype(BF16), n_pad=n_pad)
        final = l == depth - 1
        h = _ffn(h, norm_ffn[l].reshape(1, d), w_gate[l].astype(BF16), w_up[l].astype(BF16),
                 w_down[l].astype(BF16), final_gain,
                 rows=bsz * seq if final else h.shape[0], final=final)

    return h.reshape(bsz, seq, d)
```

```python
import functools
import math

import jax
import jax.numpy as jnp
from jax import lax
from jax.experimental import pallas as pl
from jax.experimental.pallas import tpu as pltpu

F32 = jnp.float32
BF16 = jnp.bfloat16

HEAD_DIM = 128
AUG_DIM = 2 * HEAD_DIM
N_SPLIT = 3
CONV_WIDTH = 3
EPS = 1e-6
NEG = -1e30
FAKE = 2 * NEG
LOG2E = math.log2(math.e)
SEQ_TILE = 512
HEAD_PAD = 16
ONES_ROWS = 16
LANES = 128
STAT_C, STAT_BMAX, STAT_QN, STAT_KN, N_STAT = 0, 1, 2, 3, 4
SKIP_LOG2 = 140.0
HALO = 8
VMEM_LIMIT = 60 * 1024 * 1024

_NT = (((1,), (1,)), ((), ()))


def _cparams(n_axes):
    return pltpu.CompilerParams(dimension_semantics=("arbitrary",) * n_axes,
                                vmem_limit_bytes=VMEM_LIMIT)


def _rms_scale(x, axis=-1):
    return lax.rsqrt(jnp.mean(x * x, axis=axis, keepdims=True) + EPS)


def _cumsum_lanes(x):
    n = x.shape[1]
    lane = lax.broadcasted_iota(jnp.int32, x.shape, 1)
    shift = 1
    while shift < n:
        x = x + jnp.where(lane >= shift, pltpu.roll(x, shift, axis=1), 0.0)
        shift *= 2
    return x


def _round_bf16(x):
    return x.astype(BF16).astype(F32)


def _max_row_norm(x):
    xf = x.astype(F32)
    n2 = jnp.max(jnp.sum(xf * xf, axis=1, keepdims=True), axis=0, keepdims=True)
    return jnp.broadcast_to(jnp.sqrt(n2), (1, LANES))


def _attn_proj_kernel(h_ref, g_ref, wqk_ref, wv_ref, wf_ref, bf_ref,
                      q_ref, k_ref, vt_ref, stat_ref, hn_ref, carry_ref, wvt_ref,
                      *, tm, n_heads, n_pad, scale):
    i = pl.program_id(1)
    d_attn = n_heads * HEAD_DIM
    x = h_ref[...]
    hn = (x * _rms_scale(x) * g_ref[...]).astype(BF16)
    hn_ref[...] = hn

    @pl.when(i == 0)
    def _reset():
        carry_ref[...] = jnp.zeros_like(carry_ref)

    @pl.when((pl.program_id(0) == 0) & (i == 0))
    def _transpose_v_weights():
        wvt_ref[...] = wv_ref[...].T

    logit = lax.dot_general(wf_ref[...], hn, _NT, preferred_element_type=F32)
    log_f = jax.nn.log_sigmoid(logit + bf_ref[...])
    r = _cumsum_lanes(log_f)
    stat_ref[0, 0] = jnp.zeros(stat_ref.shape[2:], F32)
    stat_ref[0, 0, STAT_C] = carry_ref[...] * LOG2E
    carry_ref[...] = carry_ref[...] + jnp.broadcast_to(r[:, tm - 1:tm], carry_ref.shape)
    pos = i * tm + lax.broadcasted_iota(jnp.int32, r.shape, 1)
    bias = jnp.where(pos >= n_pad, -LOG2E * r, NEG)
    stat_ref[0, 0, STAT_BMAX] = jnp.broadcast_to(jnp.max(bias, axis=1, keepdims=True),
                                                 (HEAD_PAD, LANES))
    bias_col = jnp.concatenate([bias, jnp.zeros((LANES - HEAD_PAD, tm), F32)], axis=0).T

    lane = lax.broadcasted_iota(jnp.int32, (tm, HEAD_DIM), 1)
    ones = jnp.where(lane < N_SPLIT, 1.0, 0.0).astype(BF16)
    zq = jnp.dot(hn, wqk_ref[:, :d_attn], preferred_element_type=F32)
    for h in range(n_heads):
        qh = (zq[:, h * HEAD_DIM:(h + 1) * HEAD_DIM] * (scale * LOG2E)).astype(BF16)
        q_ref[0, :, h * AUG_DIM:h * AUG_DIM + HEAD_DIM] = qh
        q_ref[0, :, h * AUG_DIM + HEAD_DIM:(h + 1) * AUG_DIM] = ones
        stat_ref[0, 0, STAT_QN, h:h + 1, :] = _max_row_norm(qh)

    zk = jnp.dot(hn, wqk_ref[:, d_attn:], preferred_element_type=F32)
    for h in range(n_heads):
        kh = zk[:, h * HEAD_DIM:(h + 1) * HEAD_DIM].astype(BF16)
        k_ref[0, :, h * AUG_DIM:h * AUG_DIM + HEAD_DIM] = kh
        stat_ref[0, 0, STAT_KN, h:h + 1, :] = _max_row_norm(kh)
        xb = jnp.broadcast_to(bias_col[:, h:h + 1], (tm, HEAD_DIM))
        hi = _round_bf16(xb)
        mid = _round_bf16(xb - hi)
        lo = xb - hi - mid
        aug = jnp.where(lane == 0, hi, jnp.where(lane == 1, mid, jnp.where(lane == 2, lo, 0.0)))
        k_ref[0, :, h * AUG_DIM + HEAD_DIM:(h + 1) * AUG_DIM] = aug.astype(BF16)

    zvt = lax.dot_general(wvt_ref[...], hn, _NT, preferred_element_type=F32)
    vt_ref[0] = zvt.astype(BF16)


def _row_tile_of(bsz, n_t):
    n_x = n_t - 1
    return lambda b, i: (jnp.where(i == 0, bsz * n_x + b, b * n_x + i - 1), 0)


def _attn_proj(h, gain, w_in, layer, wf_t, bf_col, *, bsz, n_pad, scale):
    d = h.shape[1]
    d_attn = d // 2
    n_heads = d_attn // HEAD_DIM
    tm = SEQ_TILE
    n_t = h.shape[0] // (bsz * tm)
    lp = n_t * tm
    kern = functools.partial(_attn_proj_kernel, tm=tm, n_heads=n_heads, n_pad=n_pad, scale=scale)
    aug_blk = pl.BlockSpec((1, tm, n_heads * AUG_DIM), lambda b, i: (b, i, 0))
    const = dict(pipeline_mode=pl.Buffered(1))
    return pl.pallas_call(
        kern,
        grid=(bsz, n_t),
        in_specs=[
            pl.BlockSpec((tm, d), _row_tile_of(bsz, n_t)),
            pl.BlockSpec((1, d), lambda b, i: (0, 0)),
            pl.BlockSpec((None, d, 2 * d_attn), lambda b, i: (layer, 0, 0), **const),
            pl.BlockSpec((None, d, d_attn), lambda b, i: (layer, 0, 2), **const),
            pl.BlockSpec((HEAD_PAD, d), lambda b, i: (0, 0)),
            pl.BlockSpec((HEAD_PAD, 1), lambda b, i: (0, 0)),
        ],
        out_specs=[
            aug_blk, aug_blk,
            pl.BlockSpec((1, d_attn, tm), lambda b, i: (b, 0, i)),
            pl.BlockSpec((1, 1, N_STAT, HEAD_PAD, LANES), lambda b, i: (b, i, 0, 0, 0)),
            pl.BlockSpec((tm, d), _row_tile_of(bsz, n_t)),
        ],
        out_shape=[
            jax.ShapeDtypeStruct((bsz, lp, n_heads * AUG_DIM), BF16),
            jax.ShapeDtypeStruct((bsz, lp, n_heads * AUG_DIM), BF16),
            jax.ShapeDtypeStruct((bsz, d_attn, lp), BF16),
            jax.ShapeDtypeStruct((bsz, n_t, N_STAT, HEAD_PAD, LANES), F32),
            jax.ShapeDtypeStruct(h.shape, BF16),
        ],
        scratch_shapes=[pltpu.VMEM((HEAD_PAD, LANES), F32), pltpu.VMEM((d_attn, d), BF16)],
        compiler_params=_cparams(2),
        name="attn_proj",
    )(h, gain, w_in, w_in, wf_t, bf_col)


def _flash_kernel(c_ref, first_ref, q_ref, k_ref, vt_ref, g_ref, o_ref, acc_ref, sa_ref, sb_ref,
                  *, tq, n_q, hpb):
    b = pl.program_id(0)
    hp = pl.program_id(1)
    i = pl.program_id(2)
    heads = range(hpb)
    c_base = [(b * n_q) * HEAD_PAD + hp * hpb + hh for hh in heads]
    c_q = [c_ref[c_base[hh] + i * HEAD_PAD] for hh in heads]
    first = first_ref[(b * pl.num_programs(1) + hp) * n_q + i]
    n_unmasked = i - first
    last = first + jnp.maximum(n_unmasked - 1, 0)
    key = lax.broadcasted_iota(jnp.int32, (tq, tq), 0)
    qry = lax.broadcasted_iota(jnp.int32, (tq, tq), 1)
    ones_rows = jnp.ones((ONES_ROWS, tq), BF16)

    def scores_into(s_ref, j, masked=False):
        start = pl.multiple_of(j * tq, tq)
        col_max = []
        for hh in heads:
            cols = slice(hh * AUG_DIM, (hh + 1) * AUG_DIM)
            st = lax.dot_general(k_ref[0, pl.ds(start, tq), cols], q_ref[0, :, cols], _NT,
                                 preferred_element_type=F32)
            if masked:
                st = jnp.where(key <= qry, st, NEG)
            s_ref[hh] = st
            col_max.append(jnp.max(st, axis=0, keepdims=True))
        return tuple(col_max)

    def accumulate(s_ref, col_max, j, delta, m_all):
        start = pl.multiple_of(j * tq, tq)
        m_out = []
        for hh in heads:
            vt = vt_ref[0, hh * HEAD_DIM:(hh + 1) * HEAD_DIM, pl.ds(start, tq)]
            m_new = jnp.maximum(m_all[hh], col_max[hh] + delta[hh])
            alpha = jnp.exp2(m_all[hh] - m_new)
            p = jnp.exp2(s_ref[hh] - (m_new - delta[hh])).astype(BF16)
            pv = jnp.dot(jnp.concatenate([vt, ones_rows], axis=0), p,
                         preferred_element_type=F32)
            acc_ref[hh] = alpha * acc_ref[hh] + pv
            m_out.append(m_new)
        return tuple(m_out)

    def delta_of(j):
        return [c_q[hh] - c_ref[c_base[hh] + j * HEAD_PAD] for hh in heads]

    def tail_delta_of(j):
        real = delta_of(jnp.minimum(j, last))
        return [jnp.where(j < i, real[hh], FAKE) for hh in heads]

    acc_ref[...] = jnp.zeros_like(acc_ref)
    max_a = scores_into(sa_ref, first)
    m_all = tuple(jnp.full((1, tq), NEG, F32) for _ in heads)
    n_trips = jnp.maximum((n_unmasked - 1) // 2, 0)

    def trip(jj, carry):
        m_all, max_a = carry
        j0 = first + 2 * jj
        max_b = scores_into(sb_ref, j0 + 1)
        m_all = accumulate(sa_ref, max_a, j0, delta_of(j0), m_all)
        max_a = scores_into(sa_ref, j0 + 2)
        m_all = accumulate(sb_ref, max_b, j0 + 1, delta_of(j0 + 1), m_all)
        return m_all, max_a

    def two_trips(jj, carry):
        return trip(2 * jj + 1, trip(2 * jj, carry))

    carry = lax.fori_loop(0, n_trips // 2, two_trips, (m_all, max_a))
    m_all, max_a = lax.fori_loop(2 * (n_trips // 2), n_trips, trip, carry)
    j0 = first + 2 * n_trips
    no_delta = [0.0 for _ in heads]

    @pl.when(n_unmasked - 2 * n_trips == 2)
    def _tail_two_unmasked():
        max_b = scores_into(sb_ref, j0 + 1)
        m_1 = accumulate(sa_ref, max_a, j0, delta_of(j0), m_all)
        max_d = scores_into(sa_ref, i, masked=True)
        m_2 = accumulate(sb_ref, max_b, j0 + 1, delta_of(j0 + 1), m_1)
        accumulate(sa_ref, max_d, i, no_delta, m_2)

    @pl.when(n_unmasked - 2 * n_trips < 2)
    def _tail_one_unmasked():
        max_d = scores_into(sb_ref, i, masked=True)
        m_1 = accumulate(sa_ref, max_a, jnp.minimum(j0, last), tail_delta_of(j0), m_all)
        accumulate(sb_ref, max_d, i, no_delta, m_1)

    for hh in heads:
        ot = acc_ref[hh, :HEAD_DIM, :] / acc_ref[hh, HEAD_DIM:HEAD_DIM + 1, :]
        ot = ot * _rms_scale(ot, axis=0)
        o_ref[0, :, hh * HEAD_DIM:(hh + 1) * HEAD_DIM] = (ot.T * g_ref[hh]).astype(BF16)


def _heads_per_block(n_heads):
    return 2 if n_heads % 2 == 0 else 1


def _first_key_tile(stats, n_heads):
    hpb = _heads_per_block(n_heads)
    st = stats[:, :, :, :n_heads, 0]
    c, bmax, qn, kn = (jnp.moveaxis(st[:, :, s], 1, 2) for s in
                       (STAT_C, STAT_BMAX, STAT_QN, STAT_KN))
    upper = (qn[..., :, None] * (kn[..., None, :] + kn[..., :, None]) * 1.001 + 1.0
             + (c[..., :, None] - c[..., None, :]) + bmax[..., None, :])
    n_t = c.shape[-1]
    earlier = jnp.arange(n_t)[None, :] < jnp.arange(n_t)[:, None]
    negligible = (upper < -SKIP_LOG2) & earlier
    first = jnp.min(jnp.where(negligible, n_t, jnp.arange(n_t)), axis=-1)
    first = first.reshape(first.shape[0], n_heads // hpb, hpb, n_t).min(axis=2)
    return first.astype(jnp.int32).reshape(-1)


def _flash(q, k, vt, c_tiles, first_tile, og_heads):
    bsz, lp, _ = q.shape
    n_heads = vt.shape[1] // HEAD_DIM
    hpb = _heads_per_block(n_heads)
    tq = SEQ_TILE
    n_q = lp // tq
    kern = functools.partial(_flash_kernel, tq=tq, n_q=n_q, hpb=hpb)
    resident = pl.Buffered(2)
    return pl.pallas_call(
        kern,
        grid=(bsz, n_heads // hpb, n_q),
        in_specs=[
            pl.BlockSpec(memory_space=pltpu.SMEM),
            pl.BlockSpec(memory_space=pltpu.SMEM),
            pl.BlockSpec((1, tq, hpb * AUG_DIM), lambda b, h, i: (b, i, h)),
            pl.BlockSpec((1, lp, hpb * AUG_DIM), lambda b, h, i: (b, 0, h),
                         pipeline_mode=resident),
            pl.BlockSpec((1, hpb * HEAD_DIM, lp), lambda b, h, i: (b, h, 0),
                         pipeline_mode=resident),
            pl.BlockSpec((hpb, 1, HEAD_DIM), lambda b, h, i: (h, 0, 0)),
        ],
        out_specs=pl.BlockSpec((1, tq, hpb * HEAD_DIM), lambda b, h, i: (b, i, h)),
        out_shape=jax.ShapeDtypeStruct((bsz, lp, n_heads * HEAD_DIM), BF16),
        scratch_shapes=[
            pltpu.VMEM((hpb, HEAD_DIM + ONES_ROWS, tq), F32),
            pltpu.VMEM((hpb, tq, tq), F32),
            pltpu.VMEM((hpb, tq, tq), F32),
        ],
        compiler_params=_cparams(3),
        name="flash",
    )(c_tiles, first_tile, q, k, vt, og_heads)


def _mix_out_kernel(h_ref, hn_ref, ya_ref, wc_ref, cw_ref, og_ref, wo_ref, o_ref, u_ref,
                    *, tm, n_groups, n_pad):
    i = pl.program_id(1)
    d_conv = n_groups * HEAD_DIM
    hn = hn_ref[...]

    @pl.when(i == 0)
    def _reset():
        u_ref[pl.ds(0, HALO), :] = jnp.zeros((HALO, d_conv), F32)

    gate_b = jnp.dot(hn, wc_ref[:, :d_conv], preferred_element_type=F32)
    gate_c = jnp.dot(hn, wc_ref[:, d_conv:2 * d_conv], preferred_element_type=F32)
    hc = jnp.dot(hn, wc_ref[:, 2 * d_conv:], preferred_element_type=F32)
    u = gate_c * hc
    u_ref[pl.ds(HALO, tm), :] = u
    u1 = u_ref[pl.ds(HALO - 1, tm), :]
    u2 = u_ref[pl.ds(HALO - 2, tm), :]
    conv = cw_ref[0:1, :] * u2 + cw_ref[1:2, :] * u1 + cw_ref[2:3, :] * u
    y = gate_b * conv
    u_ref[pl.ds(0, HALO), :] = u_ref[pl.ds(tm, HALO), :]
    yc = jnp.concatenate(
        [(y[:, g * HEAD_DIM:(g + 1) * HEAD_DIM] * _rms_scale(y[:, g * HEAD_DIM:(g + 1) * HEAD_DIM])
          * og_ref[:, g * HEAD_DIM:(g + 1) * HEAD_DIM]).astype(BF16) for g in range(n_groups)],
        axis=1)
    d_attn = ya_ref.shape[2]
    out = (jnp.dot(ya_ref[0], wo_ref[:d_attn, :], preferred_element_type=F32)
           + jnp.dot(yc, wo_ref[d_attn:, :], preferred_element_type=F32))
    out = h_ref[...] + out
    pos = i * tm + lax.broadcasted_iota(jnp.int32, (tm, 1), 0)
    o_ref[...] = jnp.where(pos >= n_pad, out, 0.0)


def _mix_out(h, hn, ya, w_in, layer, conv_w, og_conv, w_o, *, n_pad):
    d = h.shape[1]
    bsz, lp, d_attn = ya.shape
    d_conv = d - d_attn
    assert d_conv == d_attn
    tm = SEQ_TILE
    h_blk = pl.BlockSpec((tm, d), _row_tile_of(bsz, lp // tm))
    kern = functools.partial(_mix_out_kernel, tm=tm, n_groups=d_conv // HEAD_DIM, n_pad=n_pad)
    const = dict(pipeline_mode=pl.Buffered(1))
    return pl.pallas_call(
        kern,
        grid=(bsz, lp // tm),
        in_specs=[
            h_blk,
            h_blk,
            pl.BlockSpec((1, tm, d_attn), lambda b, i: (b, i, 0)),
            pl.BlockSpec((None, d, 3 * d_conv), lambda b, i: (layer, 0, 1), **const),
            pl.BlockSpec((HALO, d_conv), lambda b, i: (0, 0)),
            pl.BlockSpec((1, d_conv), lambda b, i: (0, 0)),
            pl.BlockSpec((d_attn + d_conv, d), lambda b, i: (0, 0), **const),
        ],
        out_specs=h_blk,
        out_shape=jax.ShapeDtypeStruct(h.shape, F32),
        scratch_shapes=[pltpu.VMEM((tm + HALO, d_conv), F32)],
        compiler_params=_cparams(2),
        name="mix_out",
    )(h, hn, ya, w_in, conv_w, og_conv, w_o)


def _ffn_kernel(h_ref, g_ref, wg_ref, wu_ref, wd_ref, gf_ref, o_ref, hn_ref, *, final):
    f = pl.program_id(1)

    @pl.when(f == 0)
    def _norm():
        x = h_ref[...]
        hn_ref[...] = (x * _rms_scale(x) * g_ref[...]).astype(BF16)
        o_ref[...] = x

    hn = hn_ref[...]
    gate = jnp.dot(hn, wg_ref[...], preferred_element_type=F32)
    up = jnp.dot(hn, wu_ref[...], preferred_element_type=F32)
    act = (jax.nn.silu(gate) * up).astype(BF16)
    o_ref[...] += jnp.dot(act, wd_ref[...], preferred_element_type=F32)

    if final:
        @pl.when(f == pl.num_programs(1) - 1)
        def _final_norm():
            y = o_ref[...]
            o_ref[...] = y * _rms_scale(y) * gf_ref[...]


def _ffn(h, gain, w_gate, w_up, w_down, final_gain, *, rows, final):
    d = h.shape[1]
    d_ff = w_gate.shape[1]
    tm = 1024 if rows % 1024 == 0 else 512
    tf = 512 if d_ff % 512 == 0 else 256
    return pl.pallas_call(
        functools.partial(_ffn_kernel, final=final),
        grid=(rows // tm, d_ff // tf),
        in_specs=[
            pl.BlockSpec((tm, d), lambda r, f: (r, 0)),
            pl.BlockSpec((1, d), lambda r, f: (0, 0)),
            pl.BlockSpec((d, tf), lambda r, f: (0, f)),
            pl.BlockSpec((d, tf), lambda r, f: (0, f)),
            pl.BlockSpec((tf, d), lambda r, f: (f, 0)),
            pl.BlockSpec((1, d), lambda r, f: (0, 0)),
        ],
        out_specs=pl.BlockSpec((tm, d), lambda r, f: (r, 0)),
        out_shape=jax.ShapeDtypeStruct((rows, d), F32),
        scratch_shapes=[pltpu.VMEM((tm, d), BF16)],
        compiler_params=_cparams(2),
        name="ffn",
    )(h, gain, w_gate, w_up, w_down, final_gain)


def kernel(x, meta, norm_mix, w_in, b_f, conv_w, out_gain, w_out, norm_ffn, w_gate, w_up, w_down, final_norm):
    bsz, seq, d = x.shape
    n_meta = meta.shape[0]
    depth = w_in.shape[0]
    d_attn = d // 2
    n_heads = d_attn // HEAD_DIM
    assert n_heads <= HEAD_PAD and seq % SEQ_TILE == 0 and n_meta <= SEQ_TILE
    assert w_in.shape[2] == 6 * d_attn + n_heads
    lp = seq + SEQ_TILE
    n_pad = lp - seq - n_meta
    scale = HEAD_DIM ** -0.5

    m = jnp.broadcast_to(meta.astype(x.dtype)[None], (bsz, n_meta, d))
    head = jnp.concatenate([jnp.zeros((bsz, n_pad, d), x.dtype), m], axis=1)
    h = jnp.concatenate([x.reshape(bsz * seq, d), head.reshape(bsz * SEQ_TILE, d)], axis=0)
    final_gain = final_norm.reshape(1, d)

    w_in_bf = w_in.astype(BF16)
    for l in range(depth):
        wf_t = jnp.zeros((HEAD_PAD, d), BF16).at[:n_heads].set(w_in[l, :, 6 * d_attn:].T.astype(BF16))
        bf_col = jnp.zeros((HEAD_PAD, 1), F32).at[:n_heads, 0].set(b_f[l])
        cw = jnp.zeros((HALO, d_attn), F32).at[:CONV_WIDTH].set(conv_w[l])
        og_heads = out_gain[l, :d_attn].reshape(n_heads, 1, HEAD_DIM)
        og_conv = out_gain[l, d_attn:].reshape(1, d_attn)

        gain = norm_mix[l].reshape(1, d)
        q, k, vt, stats, hn = _attn_proj(h, gain, w_in_bf, l, wf_t, bf_col, bsz=bsz, n_pad=n_pad,
                                         scale=scale)
        ya = _flash(q, k, vt, stats[:, :, STAT_C, :, 0].reshape(-1),
                    _first_key_tile(stats, n_heads), og_heads)

        h = _mix_out(h, hn, ya, w_in_bf, l, cw, og_conv, w_out[l].astype(BF16), n_pad=n_pad)
        final = l == depth - 1
        h = _ffn(h, norm_ffn[l].reshape(1, d), w_gate[l].astype(BF16), w_up[l].astype(BF16),
                 w_down[l].astype(BF16), final_gain,
                 rows=bsz * seq if final else h.shape[0], final=final)

    return h.reshape(bsz, seq, d)
```

```python
import functools
import math

import jax
import jax.numpy as jnp
from jax import lax
from jax.experimental import pallas as pl
from jax.experimental.pallas import tpu as pltpu

F32 = jnp.float32
BF16 = jnp.bfloat16

HEAD_DIM = 128
AUG_DIM = 2 * HEAD_DIM
N_SPLIT = 3
CONV_WIDTH = 3
EPS = 1e-6
NEG = -1e30
FAKE = 2 * NEG
LOG2E = math.log2(math.e)
SEQ_TILE = 512
HEAD_PAD = 16
ONES_ROWS = 16
LANES = 128
STAT_C, STAT_BMAX, STAT_QN, STAT_KN, N_STAT = 0, 1, 2, 3, 4
SKIP_LOG2 = 140.0
HALO = 8
VMEM_LIMIT = 60 * 1024 * 1024

_NT = (((1,), (1,)), ((), ()))


def _cparams(n_axes):
    return pltpu.CompilerParams(dimension_semantics=("arbitrary",) * n_axes,
                                vmem_limit_bytes=VMEM_LIMIT)


def _rms_scale(x, axis=-1):
    return lax.rsqrt(jnp.mean(x * x, axis=axis, keepdims=True) + EPS)


def _cumsum_lanes(x):
    n = x.shape[1]
    lane = lax.broadcasted_iota(jnp.int32, x.shape, 1)
    shift = 1
    while shift < n:
        x = x + jnp.where(lane >= shift, pltpu.roll(x, shift, axis=1), 0.0)
        shift *= 2
    return x


def _round_bf16(x):
    return x.astype(BF16).astype(F32)


def _max_row_norm(x):
    xf = x.astype(F32)
    n2 = jnp.max(jnp.sum(xf * xf, axis=1, keepdims=True), axis=0, keepdims=True)
    return jnp.broadcast_to(jnp.sqrt(n2), (1, LANES))


def _attn_proj_kernel(h_ref, g_ref, wqk_ref, wv_ref, wf_ref, bf_ref,
                      q_ref, k_ref, vt_ref, stat_ref, hn_ref, carry_ref, wvt_ref,
                      *, tm, n_heads, n_pad, scale):
    i = pl.program_id(1)
    d_attn = n_heads * HEAD_DIM
    x = h_ref[...]
    hn = (x * _rms_scale(x) * g_ref[...]).astype(BF16)
    hn_ref[...] = hn

    @pl.when(i == 0)
    def _reset():
        carry_ref[...] = jnp.zeros_like(carry_ref)

    @pl.when((pl.program_id(0) == 0) & (i == 0))
    def _transpose_v_weights():
        wvt_ref[...] = wv_ref[...].T

    logit = lax.dot_general(wf_ref[...], hn, _NT, preferred_element_type=F32)
    log_f = jax.nn.log_sigmoid(logit + bf_ref[...])
    r = _cumsum_lanes(log_f)
    stat_ref[0, 0] = jnp.zeros(stat_ref.shape[2:], F32)
    stat_ref[0, 0, STAT_C] = carry_ref[...] * LOG2E
    carry_ref[...] = carry_ref[...] + jnp.broadcast_to(r[:, tm - 1:tm], carry_ref.shape)
    pos = i * tm + lax.broadcasted_iota(jnp.int32, r.shape, 1)
    bias = jnp.where(pos >= n_pad, -LOG2E * r, NEG)
    stat_ref[0, 0, STAT_BMAX] = jnp.broadcast_to(jnp.max(bias, axis=1, keepdims=True),
                                                 (HEAD_PAD, LANES))
    bias_col = jnp.concatenate([bias, jnp.zeros((LANES - HEAD_PAD, tm), F32)], axis=0).T

    lane = lax.broadcasted_iota(jnp.int32, (tm, HEAD_DIM), 1)
    ones = jnp.where(lane < N_SPLIT, 1.0, 0.0).astype(BF16)
    zq = jnp.dot(hn, wqk_ref[:, :d_attn], preferred_element_type=F32)
    for h in range(n_heads):
        qh = (zq[:, h * HEAD_DIM:(h + 1) * HEAD_DIM] * (scale * LOG2E)).astype(BF16)
        q_ref[0, :, h * AUG_DIM:h * AUG_DIM + HEAD_DIM] = qh
        q_ref[0, :, h * AUG_DIM + HEAD_DIM:(h + 1) * AUG_DIM] = ones
        stat_ref[0, 0, STAT_QN, h:h + 1, :] = _max_row_norm(qh)

    zk = jnp.dot(hn, wqk_ref[:, d_attn:], preferred_element_type=F32)
    for h in range(n_heads):
        kh = zk[:, h * HEAD_DIM:(h + 1) * HEAD_DIM].astype(BF16)
        k_ref[0, :, h * AUG_DIM:h * AUG_DIM + HEAD_DIM] = kh
        stat_ref[0, 0, STAT_KN, h:h + 1, :] = _max_row_norm(kh)
        xb = jnp.broadcast_to(bias_col[:, h:h + 1], (tm, HEAD_DIM))
        hi = _round_bf16(xb)
        mid = _round_bf16(xb - hi)
        lo = xb - hi - mid
        aug = jnp.where(lane == 0, hi, jnp.where(lane == 1, mid, jnp.where(lane == 2, lo, 0.0)))
        k_ref[0, :, h * AUG_DIM + HEAD_DIM:(h + 1) * AUG_DIM] = aug.astype(BF16)

    zvt = lax.dot_general(wvt_ref[...], hn, _NT, preferred_element_type=F32)
    vt_ref[0] = zvt.astype(BF16)


def _row_tile_of(bsz, n_t):
    n_x = n_t - 1
    return lambda b, i: (jnp.where(i == 0, bsz * n_x + b, b * n_x + i - 1), 0)


def _attn_proj(h, gain, w_in, layer, wf_t, bf_col, *, bsz, n_pad, scale):
    d = h.shape[1]
    d_attn = d // 2
    n_heads = d_attn // HEAD_DIM
    tm = SEQ_TILE
    n_t = h.shape[0] // (bsz * tm)
    lp = n_t * tm
    kern = functools.partial(_attn_proj_kernel, tm=tm, n_heads=n_heads, n_pad=n_pad, scale=scale)
    aug_blk = pl.BlockSpec((1, tm, n_heads * AUG_DIM), lambda b, i: (b, i, 0))
    const = dict(pipeline_mode=pl.Buffered(1))
    return pl.pallas_call(
        kern,
        grid=(bsz, n_t),
        in_specs=[
            pl.BlockSpec((tm, d), _row_tile_of(bsz, n_t)),
            pl.BlockSpec((1, d), lambda b, i: (0, 0)),
            pl.BlockSpec((None, d, 2 * d_attn), lambda b, i: (layer, 0, 0), **const),
            pl.BlockSpec((None, d, d_attn), lambda b, i: (layer, 0, 2), **const),
            pl.BlockSpec((HEAD_PAD, d), lambda b, i: (0, 0)),
            pl.BlockSpec((HEAD_PAD, 1), lambda b, i: (0, 0)),
        ],
        out_specs=[
            aug_blk, aug_blk,
            pl.BlockSpec((1, d_attn, tm), lambda b, i: (b, 0, i)),
            pl.BlockSpec((1, 1, N_STAT, HEAD_PAD, LANES), lambda b, i: (b, i, 0, 0, 0)),
            pl.BlockSpec((tm, d), _row_tile_of(bsz, n_t)),
        ],
        out_shape=[
            jax.ShapeDtypeStruct((bsz, lp, n_heads * AUG_DIM), BF16),
            jax.ShapeDtypeStruct((bsz, lp, n_heads * AUG_DIM), BF16),
            jax.ShapeDtypeStruct((bsz, d_attn, lp), BF16),
            jax.ShapeDtypeStruct((bsz, n_t, N_STAT, HEAD_PAD, LANES), F32),
            jax.ShapeDtypeStruct(h.shape, BF16),
        ],
        scratch_shapes=[pltpu.VMEM((HEAD_PAD, LANES), F32), pltpu.VMEM((d_attn, d), BF16)],
        compiler_params=_cparams(2),
        name="attn_proj",
    )(h, gain, w_in, w_in, wf_t, bf_col)


def _flash_kernel(c_ref, first_ref, q_ref, k_ref, vt_ref, g_ref, o_ref, acc_ref, sa_ref, sb_ref,
                  *, tq, n_q, hpb):
    b = pl.program_id(0)
    hp = pl.program_id(1)
    i = pl.program_id(2)
    heads = range(hpb)
    c_base = [(b * n_q) * HEAD_PAD + hp * hpb + hh for hh in heads]
    c_q = [c_ref[c_base[hh] + i * HEAD_PAD] for hh in heads]
    first = first_ref[(b * pl.num_programs(1) + hp) * n_q + i]
    n_unmasked = i - first
    last = first + jnp.maximum(n_unmasked - 1, 0)
    key = lax.broadcasted_iota(jnp.int32, (tq, tq), 0)
    qry = lax.broadcasted_iota(jnp.int32, (tq, tq), 1)
    ones_rows = jnp.ones((ONES_ROWS, tq), BF16)

    def scores_into(s_ref, j, masked=False):
        start = pl.multiple_of(j * tq, tq)
        col_max = []
        for hh in heads:
            cols = slice(hh * AUG_DIM, (hh + 1) * AUG_DIM)
            st = lax.dot_general(k_ref[0, pl.ds(start, tq), cols], q_ref[0, :, cols], _NT,
                                 preferred_element_type=F32)
            if masked:
                st = jnp.where(key <= qry, st, NEG)
            s_ref[hh] = st
            col_max.append(jnp.max(st, axis=0, keepdims=True))
        return tuple(col_max)

    def accumulate(s_ref, col_max, j, delta, m_all):
        start = pl.multiple_of(j * tq, tq)
        m_out = []
        for hh in heads:
            vt = vt_ref[0, hh * HEAD_DIM:(hh + 1) * HEAD_DIM, pl.ds(start, tq)]
            m_new = jnp.maximum(m_all[hh], col_max[hh] + delta[hh])
            alpha = jnp.exp2(m_all[hh] - m_new)
            p = jnp.exp2(s_ref[hh] - (m_new - delta[hh])).astype(BF16)
            pv = jnp.dot(jnp.concatenate([vt, ones_rows], axis=0), p,
                         preferred_element_type=F32)
            acc_ref[hh] = alpha * acc_ref[hh] + pv
            m_out.append(m_new)
        return tuple(m_out)

    def delta_of(j):
        return [c_q[hh] - c_ref[c_base[hh] + j * HEAD_PAD] for hh in heads]

    def tail_delta_of(j):
        real = delta_of(jnp.minimum(j, last))
        return [jnp.where(j < i, real[hh], FAKE) for hh in heads]

    acc_ref[...] = jnp.zeros_like(acc_ref)
    max_a = scores_into(sa_ref, first)
    m_all = tuple(jnp.full((1, tq), NEG, F32) for _ in heads)
    n_trips = jnp.maximum((n_unmasked - 1) // 2, 0)

    def trip(jj, carry):
        m_all, max_a = carry
        j0 = first + 2 * jj
        max_b = scores_into(sb_ref, j0 + 1)
        m_all = accumulate(sa_ref, max_a, j0, delta_of(j0), m_all)
        max_a = scores_into(sa_ref, j0 + 2)
        m_all = accumulate(sb_ref, max_b, j0 + 1, delta_of(j0 + 1), m_all)
        return m_all, max_a

    def two_trips(jj, carry):
        return trip(2 * jj + 1, trip(2 * jj, carry))

    carry = lax.fori_loop(0, n_trips // 2, two_trips, (m_all, max_a))
    m_all, max_a = lax.fori_loop(2 * (n_trips // 2), n_trips, trip, carry)
    j0 = first + 2 * n_trips
    no_delta = [0.0 for _ in heads]

    @pl.when(n_unmasked - 2 * n_trips == 2)
    def _tail_two_unmasked():
        max_b = scores_into(sb_ref, j0 + 1)
        m_1 = accumulate(sa_ref, max_a, j0, delta_of(j0), m_all)
        max_d = scores_into(sa_ref, i, masked=True)
        m_2 = accumulate(sb_ref, max_b, j0 + 1, delta_of(j0 + 1), m_1)
        accumulate(sa_ref, max_d, i, no_delta, m_2)

    @pl.when(n_unmasked - 2 * n_trips < 2)
    def _tail_one_unmasked():
        max_d = scores_into(sb_ref, i, masked=True)
        m_1 = accumulate(sa_ref, max_a, jnp.minimum(j0, last), tail_delta_of(j0), m_all)
        accumulate(sb_ref, max_d, i, no_delta, m_1)

    for hh in heads:
        ot = acc_ref[hh, :HEAD_DIM, :] / acc_ref[hh, HEAD_DIM:HEAD_DIM + 1, :]
        ot = ot * _rms_scale(ot, axis=0)
        o_ref[0, :, hh * HEAD_DIM:(hh + 1) * HEAD_DIM] = (ot.T * g_ref[hh]).astype(BF16)


def _heads_per_block(n_heads):
    return 2 if n_heads % 2 == 0 else 1


def _first_key_tile(stats, n_heads):
    hpb = _heads_per_block(n_heads)
    st = stats[:, :, :, :n_heads, 0]
    c, bmax, qn, kn = (jnp.moveaxis(st[:, :, s], 1, 2) for s in
                       (STAT_C, STAT_BMAX, STAT_QN, STAT_KN))
    upper = (qn[..., :, None] * (kn[..., None, :] + kn[..., :, None]) * 1.001 + 1.0
             + (c[..., :, None] - c[..., None, :]) + bmax[..., None, :])
    n_t = c.shape[-1]
    earlier = jnp.arange(n_t)[None, :] < jnp.arange(n_t)[:, None]
    negligible = (upper < -SKIP_LOG2) & earlier
    first = jnp.min(jnp.where(negligible, n_t, jnp.arange(n_t)), axis=-1)
    first = first.reshape(first.shape[0], n_heads // hpb, hpb, n_t).min(axis=2)
    return first.astype(jnp.int32).reshape(-1)


def _flash(q, k, vt, c_tiles, first_tile, og_heads):
    bsz, lp, _ = q.shape
    n_heads = vt.shape[1] // HEAD_DIM
    hpb = _heads_per_block(n_heads)
    tq = SEQ_TILE
    n_q = lp // tq
    kern = functools.partial(_flash_kernel, tq=tq, n_q=n_q, hpb=hpb)
    resident = pl.Buffered(2)
    return pl.pallas_call(
        kern,
        grid=(bsz, n_heads // hpb, n_q),
        in_specs=[
            pl.BlockSpec(memory_space=pltpu.SMEM),
            pl.BlockSpec(memory_space=pltpu.SMEM),
            pl.BlockSpec((1, tq, hpb * AUG_DIM), lambda b, h, i: (b, i, h)),
            pl.BlockSpec((1, lp, hpb * AUG_DIM), lambda b, h, i: (b, 0, h),
                         pipeline_mode=resident),
            pl.BlockSpec((1, hpb * HEAD_DIM, lp), lambda b, h, i: (b, h, 0),
                         pipeline_mode=resident),
            pl.BlockSpec((hpb, 1, HEAD_DIM), lambda b, h, i: (h, 0, 0)),
        ],
        out_specs=pl.BlockSpec((1, tq, hpb * HEAD_DIM), lambda b, h, i: (b, i, h)),
        out_shape=jax.ShapeDtypeStruct((bsz, lp, n_heads * HEAD_DIM), BF16),
        scratch_shapes=[
            pltpu.VMEM((hpb, HEAD_DIM + ONES_ROWS, tq), F32),
            pltpu.VMEM((hpb, tq, tq), F32),
            pltpu.VMEM((hpb, tq, tq), F32),
        ],
        compiler_params=_cparams(3),
        name="flash",
    )(c_tiles, first_tile, q, k, vt, og_heads)


def _mix_out_kernel(h_ref, hn_ref, ya_ref, wc_ref, cw_ref, og_ref, wo_ref, o_ref, u_ref,
                    *, tm, n_groups, n_pad):
    i = pl.program_id(1)
    d_conv = n_groups * HEAD_DIM
    hn = hn_ref[...]

    @pl.when(i == 0)
    def _reset():
        u_ref[pl.ds(0, HALO), :] = jnp.zeros((HALO, d_conv), F32)

    gate_b = jnp.dot(hn, wc_ref[:, :d_conv], preferred_element_type=F32)
    gate_c = jnp.dot(hn, wc_ref[:, d_conv:2 * d_conv], preferred_element_type=F32)
    hc = jnp.dot(hn, wc_ref[:, 2 * d_conv:], preferred_element_type=F32)
    u = gate_c * hc
    u_ref[pl.ds(HALO, tm), :] = u
    u1 = u_ref[pl.ds(HALO - 1, tm), :]
    u2 = u_ref[pl.ds(HALO - 2, tm), :]
    conv = cw_ref[0:1, :] * u2 + cw_ref[1:2, :] * u1 + cw_ref[2:3, :] * u
    y = gate_b * conv
    u_ref[pl.ds(0, HALO), :] = u_ref[pl.ds(tm, HALO), :]
    yc = jnp.concatenate(
        [(y[:, g * HEAD_DIM:(g + 1) * HEAD_DIM] * _rms_scale(y[:, g * HEAD_DIM:(g + 1) * HEAD_DIM])
          * og_ref[:, g * HEAD_DIM:(g + 1) * HEAD_DIM]).astype(BF16) for g in range(n_groups)],
        axis=1)
    d_attn = ya_ref.shape[2]
    out = (jnp.dot(ya_ref[0], wo_ref[:d_attn, :], preferred_element_type=F32)
           + jnp.dot(yc, wo_ref[d_attn:, :], preferred_element_type=F32))
    out = h_ref[...] + out
    pos = i * tm + lax.broadcasted_iota(jnp.int32, (tm, 1), 0)
    o_ref[...] = jnp.where(pos >= n_pad, out, 0.0)


def _mix_out(h, hn, ya, w_in, w_out, layer, conv_w, og_conv, *, n_pad):
    d = h.shape[1]
    bsz, lp, d_attn = ya.shape
    d_conv = d - d_attn
    assert d_conv == d_attn
    tm = SEQ_TILE
    h_blk = pl.BlockSpec((tm, d), _row_tile_of(bsz, lp // tm))
    kern = functools.partial(_mix_out_kernel, tm=tm, n_groups=d_conv // HEAD_DIM, n_pad=n_pad)
    const = dict(pipeline_mode=pl.Buffered(1))
    return pl.pallas_call(
        kern,
        grid=(bsz, lp // tm),
        in_specs=[
            h_blk,
            h_blk,
            pl.BlockSpec((1, tm, d_attn), lambda b, i: (b, i, 0)),
            pl.BlockSpec((None, d, 3 * d_conv), lambda b, i: (layer, 0, 1), **const),
            pl.BlockSpec((HALO, d_conv), lambda b, i: (0, 0)),
            pl.BlockSpec((1, d_conv), lambda b, i: (0, 0)),
            pl.BlockSpec((None, d_attn + d_conv, d), lambda b, i: (layer, 0, 0), **const),
        ],
        out_specs=h_blk,
        out_shape=jax.ShapeDtypeStruct(h.shape, F32),
        scratch_shapes=[pltpu.VMEM((tm + HALO, d_conv), F32)],
        compiler_params=_cparams(2),
        name="mix_out",
    )(h, hn, ya, w_in, conv_w, og_conv, w_out)


def _ffn_kernel(h_ref, g_ref, wg_ref, wu_ref, wd_ref, gf_ref, o_ref, hn_ref, *, final):
    f = pl.program_id(1)

    @pl.when(f == 0)
    def _norm():
        x = h_ref[...]
        hn_ref[...] = (x * _rms_scale(x) * g_ref[...]).astype(BF16)
        o_ref[...] = x

    hn = hn_ref[...]
    gate = jnp.dot(hn, wg_ref[...], preferred_element_type=F32)
    up = jnp.dot(hn, wu_ref[...], preferred_element_type=F32)
    act = (jax.nn.silu(gate) * up).astype(BF16)
    o_ref[...] += jnp.dot(act, wd_ref[...], preferred_element_type=F32)

    if final:
        @pl.when(f == pl.num_programs(1) - 1)
        def _final_norm():
            y = o_ref[...]
            o_ref[...] = y * _rms_scale(y) * gf_ref[...]


def _ffn(h, gain, w_gate, w_up, w_down, layer, final_gain, *, rows, final):
    d = h.shape[1]
    d_ff = w_gate.shape[2]
    tm = 1024 if rows % 1024 == 0 else 512
    tf = 512 if d_ff % 512 == 0 else 256
    return pl.pallas_call(
        functools.partial(_ffn_kernel, final=final),
        grid=(rows // tm, d_ff // tf),
        in_specs=[
            pl.BlockSpec((tm, d), lambda r, f: (r, 0)),
            pl.BlockSpec((1, d), lambda r, f: (0, 0)),
            pl.BlockSpec((None, d, tf), lambda r, f: (layer, 0, f)),
            pl.BlockSpec((None, d, tf), lambda r, f: (layer, 0, f)),
            pl.BlockSpec((None, tf, d), lambda r, f: (layer, f, 0)),
            pl.BlockSpec((1, d), lambda r, f: (0, 0)),
        ],
        out_specs=pl.BlockSpec((tm, d), lambda r, f: (r, 0)),
        out_shape=jax.ShapeDtypeStruct((rows, d), F32),
        scratch_shapes=[pltpu.VMEM((tm, d), BF16)],
        compiler_params=_cparams(2),
        name="ffn",
    )(h, gain, w_gate, w_up, w_down, final_gain)


def kernel(x, meta, norm_mix, w_in, b_f, conv_w, out_gain, w_out, norm_ffn, w_gate, w_up, w_down, final_norm):
    bsz, seq, d = x.shape
    n_meta = meta.shape[0]
    depth = w_in.shape[0]
    d_attn = d // 2
    n_heads = d_attn // HEAD_DIM
    assert n_heads <= HEAD_PAD and seq % SEQ_TILE == 0 and n_meta <= SEQ_TILE
    assert w_in.shape[2] == 6 * d_attn + n_heads
    lp = seq + SEQ_TILE
    n_pad = lp - seq - n_meta
    scale = HEAD_DIM ** -0.5

    m = jnp.broadcast_to(meta.astype(x.dtype)[None], (bsz, n_meta, d))
    head = jnp.concatenate([jnp.zeros((bsz, n_pad, d), x.dtype), m], axis=1)
    h = jnp.concatenate([x.reshape(bsz * seq, d), head.reshape(bsz * SEQ_TILE, d)], axis=0)
    final_gain = final_norm.reshape(1, d)

    w_in_bf, w_out_bf = w_in.astype(BF16), w_out.astype(BF16)
    w_gate_bf, w_up_bf, w_down_bf = w_gate.astype(BF16), w_up.astype(BF16), w_down.astype(BF16)
    for l in range(depth):
        wf_t = jnp.zeros((HEAD_PAD, d), BF16).at[:n_heads].set(w_in[l, :, 6 * d_attn:].T.astype(BF16))
        bf_col = jnp.zeros((HEAD_PAD, 1), F32).at[:n_heads, 0].set(b_f[l])
        cw = jnp.zeros((HALO, d_attn), F32).at[:CONV_WIDTH].set(conv_w[l])
        og_heads = out_gain[l, :d_attn].reshape(n_heads, 1, HEAD_DIM)
        og_conv = out_gain[l, d_attn:].reshape(1, d_attn)

        gain = norm_mix[l].reshape(1, d)
        q, k, vt, stats, hn = _attn_proj(h, gain, w_in_bf, l, wf_t, bf_col, bsz=bsz, n_pad=n_pad,
                                         scale=scale)
        ya = _flash(q, k, vt, stats[:, :, STAT_C, :, 0].reshape(-1),
                    _first_key_tile(stats, n_heads), og_heads)

        h = _mix_out(h, hn, ya, w_in_bf, w_out_bf, l, cw, og_conv, n_pad=n_pad)
        final = l == depth - 1
        h = _ffn(h, norm_ffn[l].reshape(1, d), w_gate_bf, w_up_bf, w_down_bf, l, final_gain,
                 rows=bsz * seq if final else h.shape[0], final=final)

    return h.reshape(bsz, seq, d)
```

```python
import functools
import math

import jax
import jax.numpy as jnp
from jax import lax
from jax.experimental import pallas as pl
from jax.experimental.pallas import tpu as pltpu

F32 = jnp.float32
BF16 = jnp.bfloat16

HEAD_DIM = 128
AUG_DIM = 2 * HEAD_DIM
N_SPLIT = 3
CONV_WIDTH = 3
EPS = 1e-6
NEG = -1e30
FAKE = 2 * NEG
LOG2E = math.log2(math.e)
SEQ_TILE = 512
HEAD_PAD = 16
ONES_ROWS = 16
LANES = 128
STAT_C, STAT_BMAX, STAT_QN, STAT_KN, N_STAT = 0, 1, 2, 3, 4
SKIP_LOG2 = 140.0
HALO = 8
VMEM_LIMIT = 60 * 1024 * 1024

_NT = (((1,), (1,)), ((), ()))


def _cparams(n_axes):
    return pltpu.CompilerParams(dimension_semantics=("arbitrary",) * n_axes,
                                vmem_limit_bytes=VMEM_LIMIT)


def _rms_scale(x, axis=-1):
    return lax.rsqrt(jnp.mean(x * x, axis=axis, keepdims=True) + EPS)


def _cumsum_lanes(x):
    n = x.shape[1]
    lane = lax.broadcasted_iota(jnp.int32, x.shape, 1)
    shift = 1
    while shift < n:
        x = x + jnp.where(lane >= shift, pltpu.roll(x, shift, axis=1), 0.0)
        shift *= 2
    return x


def _round_bf16(x):
    return x.astype(BF16).astype(F32)


def _max_row_norm(x):
    xf = x.astype(F32)
    n2 = jnp.max(jnp.sum(xf * xf, axis=1, keepdims=True), axis=0, keepdims=True)
    return jnp.broadcast_to(jnp.sqrt(n2), (1, LANES))


def _attn_proj_kernel(hx_ref, hm_ref, g_ref, wqk_ref, wv_ref, wf_ref, bf_ref,
                      q_ref, k_ref, vt_ref, stat_ref, hn_ref, carry_ref, wvt_ref,
                      *, tm, n_heads, n_pad, scale):
    i = pl.program_id(1)
    d_attn = n_heads * HEAD_DIM

    @pl.when(i == 0)
    def _first_tile():
        carry_ref[...] = jnp.zeros_like(carry_ref)
        x = hm_ref[...]
        hn_ref[...] = (x * _rms_scale(x) * g_ref[...]).astype(BF16)

    @pl.when(i != 0)
    def _x_tile():
        x = hx_ref[...]
        hn_ref[...] = (x * _rms_scale(x) * g_ref[...]).astype(BF16)

    hn = hn_ref[...]

    @pl.when((pl.program_id(0) == 0) & (i == 0))
    def _transpose_v_weights():
        wvt_ref[...] = wv_ref[...].T

    logit = lax.dot_general(wf_ref[...], hn, _NT, preferred_element_type=F32)
    log_f = jax.nn.log_sigmoid(logit + bf_ref[...])
    r = _cumsum_lanes(log_f)
    stat_ref[0, 0] = jnp.zeros(stat_ref.shape[2:], F32)
    stat_ref[0, 0, STAT_C] = carry_ref[...] * LOG2E
    carry_ref[...] = carry_ref[...] + jnp.broadcast_to(r[:, tm - 1:tm], carry_ref.shape)
    pos = i * tm + lax.broadcasted_iota(jnp.int32, r.shape, 1)
    bias = jnp.where(pos >= n_pad, -LOG2E * r, NEG)
    stat_ref[0, 0, STAT_BMAX] = jnp.broadcast_to(jnp.max(bias, axis=1, keepdims=True),
                                                 (HEAD_PAD, LANES))
    bias_col = jnp.concatenate([bias, jnp.zeros((LANES - HEAD_PAD, tm), F32)], axis=0).T

    lane = lax.broadcasted_iota(jnp.int32, (tm, HEAD_DIM), 1)
    ones = jnp.where(lane < N_SPLIT, 1.0, 0.0).astype(BF16)
    zq = jnp.dot(hn, wqk_ref[:, :d_attn], preferred_element_type=F32)
    for h in range(n_heads):
        qh = (zq[:, h * HEAD_DIM:(h + 1) * HEAD_DIM] * (scale * LOG2E)).astype(BF16)
        q_ref[0, :, h * AUG_DIM:h * AUG_DIM + HEAD_DIM] = qh
        q_ref[0, :, h * AUG_DIM + HEAD_DIM:(h + 1) * AUG_DIM] = ones
        stat_ref[0, 0, STAT_QN, h:h + 1, :] = _max_row_norm(qh)

    zk = jnp.dot(hn, wqk_ref[:, d_attn:], preferred_element_type=F32)
    for h in range(n_heads):
        kh = zk[:, h * HEAD_DIM:(h + 1) * HEAD_DIM].astype(BF16)
        k_ref[0, :, h * AUG_DIM:h * AUG_DIM + HEAD_DIM] = kh
        stat_ref[0, 0, STAT_KN, h:h + 1, :] = _max_row_norm(kh)
        xb = jnp.broadcast_to(bias_col[:, h:h + 1], (tm, HEAD_DIM))
        hi = _round_bf16(xb)
        mid = _round_bf16(xb - hi)
        lo = xb - hi - mid
        aug = jnp.where(lane == 0, hi, jnp.where(lane == 1, mid, jnp.where(lane == 2, lo, 0.0)))
        k_ref[0, :, h * AUG_DIM + HEAD_DIM:(h + 1) * AUG_DIM] = aug.astype(BF16)

    zvt = lax.dot_general(wvt_ref[...], hn, _NT, preferred_element_type=F32)
    vt_ref[0] = zvt.astype(BF16)


def _row_tile_of(bsz, n_t):
    n_x = n_t - 1
    return lambda b, i: (jnp.where(i == 0, bsz * n_x + b, b * n_x + i - 1), 0)


def _residual_specs(hx, hm, bsz, n_t, tm):
    n_x = n_t - 1
    first_meta = hm.shape[0] // tm - bsz
    d = hx.shape[1]
    return (pl.BlockSpec((tm, d), lambda b, i: (b * n_x + jnp.maximum(i - 1, 0), 0)),
            pl.BlockSpec((tm, d), lambda b, i: (first_meta + b, 0), pipeline_mode=pl.Buffered(1)))


def _attn_proj(hx, hm, gain, w_in, layer, wf_t, bf_col, *, bsz, n_t, n_pad, scale):
    d = hx.shape[1]
    d_attn = d // 2
    n_heads = d_attn // HEAD_DIM
    tm = SEQ_TILE
    lp = n_t * tm
    kern = functools.partial(_attn_proj_kernel, tm=tm, n_heads=n_heads, n_pad=n_pad, scale=scale)
    aug_blk = pl.BlockSpec((1, tm, n_heads * AUG_DIM), lambda b, i: (b, i, 0))
    const = dict(pipeline_mode=pl.Buffered(1))
    return pl.pallas_call(
        kern,
        grid=(bsz, n_t),
        in_specs=[
            *_residual_specs(hx, hm, bsz, n_t, tm),
            pl.BlockSpec((1, d), lambda b, i: (0, 0)),
            pl.BlockSpec((None, d, 2 * d_attn), lambda b, i: (layer, 0, 0), **const),
            pl.BlockSpec((None, d, d_attn), lambda b, i: (layer, 0, 2), **const),
            pl.BlockSpec((HEAD_PAD, d), lambda b, i: (0, 0)),
            pl.BlockSpec((HEAD_PAD, 1), lambda b, i: (0, 0)),
        ],
        out_specs=[
            aug_blk, aug_blk,
            pl.BlockSpec((1, d_attn, tm), lambda b, i: (b, 0, i)),
            pl.BlockSpec((1, 1, N_STAT, HEAD_PAD, LANES), lambda b, i: (b, i, 0, 0, 0)),
            pl.BlockSpec((tm, d), _row_tile_of(bsz, n_t)),
        ],
        out_shape=[
            jax.ShapeDtypeStruct((bsz, lp, n_heads * AUG_DIM), BF16),
            jax.ShapeDtypeStruct((bsz, lp, n_heads * AUG_DIM), BF16),
            jax.ShapeDtypeStruct((bsz, d_attn, lp), BF16),
            jax.ShapeDtypeStruct((bsz, n_t, N_STAT, HEAD_PAD, LANES), F32),
            jax.ShapeDtypeStruct((bsz * lp, d), BF16),
        ],
        scratch_shapes=[pltpu.VMEM((HEAD_PAD, LANES), F32), pltpu.VMEM((d_attn, d), BF16)],
        compiler_params=_cparams(2),
        name="attn_proj",
    )(hx, hm, gain, w_in, w_in, wf_t, bf_col)


def _flash_kernel(c_ref, first_ref, q_ref, k_ref, vt_ref, g_ref, o_ref, acc_ref, sa_ref, sb_ref,
                  *, tq, n_q, hpb):
    b = pl.program_id(0)
    hp = pl.program_id(1)
    i = pl.program_id(2)
    heads = range(hpb)
    c_base = [(b * n_q) * HEAD_PAD + hp * hpb + hh for hh in heads]
    c_q = [c_ref[c_base[hh] + i * HEAD_PAD] for hh in heads]
    first = first_ref[(b * pl.num_programs(1) + hp) * n_q + i]
    n_unmasked = i - first
    last = first + jnp.maximum(n_unmasked - 1, 0)
    key = lax.broadcasted_iota(jnp.int32, (tq, tq), 0)
    qry = lax.broadcasted_iota(jnp.int32, (tq, tq), 1)
    ones_rows = jnp.ones((ONES_ROWS, tq), BF16)

    def scores_into(s_ref, j, masked=False):
        start = pl.multiple_of(j * tq, tq)
        col_max = []
        for hh in heads:
            cols = slice(hh * AUG_DIM, (hh + 1) * AUG_DIM)
            st = lax.dot_general(k_ref[0, pl.ds(start, tq), cols], q_ref[0, :, cols], _NT,
                                 preferred_element_type=F32)
            if masked:
                st = jnp.where(key <= qry, st, NEG)
            s_ref[hh] = st
            col_max.append(jnp.max(st, axis=0, keepdims=True))
        return tuple(col_max)

    def accumulate(s_ref, col_max, j, delta, m_all):
        start = pl.multiple_of(j * tq, tq)
        m_out = []
        for hh in heads:
            vt = vt_ref[0, hh * HEAD_DIM:(hh + 1) * HEAD_DIM, pl.ds(start, tq)]
            m_new = jnp.maximum(m_all[hh], col_max[hh] + delta[hh])
            alpha = jnp.exp2(m_all[hh] - m_new)
            p = jnp.exp2(s_ref[hh] - (m_new - delta[hh])).astype(BF16)
            pv = jnp.dot(jnp.concatenate([vt, ones_rows], axis=0), p,
                         preferred_element_type=F32)
            acc_ref[hh] = alpha * acc_ref[hh] + pv
            m_out.append(m_new)
        return tuple(m_out)

    def delta_of(j):
        return [c_q[hh] - c_ref[c_base[hh] + j * HEAD_PAD] for hh in heads]

    def tail_delta_of(j):
        real = delta_of(jnp.minimum(j, last))
        return [jnp.where(j < i, real[hh], FAKE) for hh in heads]

    acc_ref[...] = jnp.zeros_like(acc_ref)
    max_a = scores_into(sa_ref, first)
    m_all = tuple(jnp.full((1, tq), NEG, F32) for _ in heads)
    n_trips = jnp.maximum((n_unmasked - 1) // 2, 0)

    def trip(jj, carry):
        m_all, max_a = carry
        j0 = first + 2 * jj
        max_b = scores_into(sb_ref, j0 + 1)
        m_all = accumulate(sa_ref, max_a, j0, delta_of(j0), m_all)
        max_a = scores_into(sa_ref, j0 + 2)
        m_all = accumulate(sb_ref, max_b, j0 + 1, delta_of(j0 + 1), m_all)
        return m_all, max_a

    def two_trips(jj, carry):
        return trip(2 * jj + 1, trip(2 * jj, carry))

    carry = lax.fori_loop(0, n_trips // 2, two_trips, (m_all, max_a))
    m_all, max_a = lax.fori_loop(2 * (n_trips // 2), n_trips, trip, carry)
    j0 = first + 2 * n_trips
    no_delta = [0.0 for _ in heads]

    @pl.when(n_unmasked - 2 * n_trips == 2)
    def _tail_two_unmasked():
        max_b = scores_into(sb_ref, j0 + 1)
        m_1 = accumulate(sa_ref, max_a, j0, delta_of(j0), m_all)
        max_d = scores_into(sa_ref, i, masked=True)
        m_2 = accumulate(sb_ref, max_b, j0 + 1, delta_of(j0 + 1), m_1)
        accumulate(sa_ref, max_d, i, no_delta, m_2)

    @pl.when(n_unmasked - 2 * n_trips < 2)
    def _tail_one_unmasked():
        max_d = scores_into(sb_ref, i, masked=True)
        m_1 = accumulate(sa_ref, max_a, jnp.minimum(j0, last), tail_delta_of(j0), m_all)
        accumulate(sb_ref, max_d, i, no_delta, m_1)

    for hh in heads:
        ot = acc_ref[hh, :HEAD_DIM, :] / acc_ref[hh, HEAD_DIM:HEAD_DIM + 1, :]
        ot = ot * _rms_scale(ot, axis=0)
        o_ref[0, :, hh * HEAD_DIM:(hh + 1) * HEAD_DIM] = (ot.T * g_ref[hh]).astype(BF16)


def _heads_per_block(n_heads):
    return 2 if n_heads % 2 == 0 else 1


def _first_key_tile(stats, n_heads):
    hpb = _heads_per_block(n_heads)
    st = stats[:, :, :, :n_heads, 0]
    c, bmax, qn, kn = (jnp.moveaxis(st[:, :, s], 1, 2) for s in
                       (STAT_C, STAT_BMAX, STAT_QN, STAT_KN))
    upper = (qn[..., :, None] * (kn[..., None, :] + kn[..., :, None]) * 1.001 + 1.0
             + (c[..., :, None] - c[..., None, :]) + bmax[..., None, :])
    n_t = c.shape[-1]
    earlier = jnp.arange(n_t)[None, :] < jnp.arange(n_t)[:, None]
    negligible = (upper < -SKIP_LOG2) & earlier
    first = jnp.min(jnp.where(negligible, n_t, jnp.arange(n_t)), axis=-1)
    first = first.reshape(first.shape[0], n_heads // hpb, hpb, n_t).min(axis=2)
    return first.astype(jnp.int32).reshape(-1)


def _flash(q, k, vt, c_tiles, first_tile, og_heads):
    bsz, lp, _ = q.shape
    n_heads = vt.shape[1] // HEAD_DIM
    hpb = _heads_per_block(n_heads)
    tq = SEQ_TILE
    n_q = lp // tq
    kern = functools.partial(_flash_kernel, tq=tq, n_q=n_q, hpb=hpb)
    resident = pl.Buffered(2)
    return pl.pallas_call(
        kern,
        grid=(bsz, n_heads // hpb, n_q),
        in_specs=[
            pl.BlockSpec(memory_space=pltpu.SMEM),
            pl.BlockSpec(memory_space=pltpu.SMEM),
            pl.BlockSpec((1, tq, hpb * AUG_DIM), lambda b, h, i: (b, i, h)),
            pl.BlockSpec((1, lp, hpb * AUG_DIM), lambda b, h, i: (b, 0, h),
                         pipeline_mode=resident),
            pl.BlockSpec((1, hpb * HEAD_DIM, lp), lambda b, h, i: (b, h, 0),
                         pipeline_mode=resident),
            pl.BlockSpec((hpb, 1, HEAD_DIM), lambda b, h, i: (h, 0, 0)),
        ],
        out_specs=pl.BlockSpec((1, tq, hpb * HEAD_DIM), lambda b, h, i: (b, i, h)),
        out_shape=jax.ShapeDtypeStruct((bsz, lp, n_heads * HEAD_DIM), BF16),
        scratch_shapes=[
            pltpu.VMEM((hpb, HEAD_DIM + ONES_ROWS, tq), F32),
            pltpu.VMEM((hpb, tq, tq), F32),
            pltpu.VMEM((hpb, tq, tq), F32),
        ],
        compiler_params=_cparams(3),
        name="flash",
    )(c_tiles, first_tile, q, k, vt, og_heads)


def _mix_out_kernel(hx_ref, hm_ref, hn_ref, ya_ref, wc_ref, cw_ref, og_ref, wo_ref, o_ref, u_ref,
                    *, tm, n_groups, n_pad):
    i = pl.program_id(1)
    d_conv = n_groups * HEAD_DIM
    hn = hn_ref[...]

    @pl.when(i == 0)
    def _reset():
        u_ref[pl.ds(0, HALO), :] = jnp.zeros((HALO, d_conv), F32)

    gate_b = jnp.dot(hn, wc_ref[:, :d_conv], preferred_element_type=F32)
    gate_c = jnp.dot(hn, wc_ref[:, d_conv:2 * d_conv], preferred_element_type=F32)
    hc = jnp.dot(hn, wc_ref[:, 2 * d_conv:], preferred_element_type=F32)
    u = gate_c * hc
    u_ref[pl.ds(HALO, tm), :] = u
    u1 = u_ref[pl.ds(HALO - 1, tm), :]
    u2 = u_ref[pl.ds(HALO - 2, tm), :]
    conv = cw_ref[0:1, :] * u2 + cw_ref[1:2, :] * u1 + cw_ref[2:3, :] * u
    y = gate_b * conv
    u_ref[pl.ds(0, HALO), :] = u_ref[pl.ds(tm, HALO), :]
    yc = jnp.concatenate(
        [(y[:, g * HEAD_DIM:(g + 1) * HEAD_DIM] * _rms_scale(y[:, g * HEAD_DIM:(g + 1) * HEAD_DIM])
          * og_ref[:, g * HEAD_DIM:(g + 1) * HEAD_DIM]).astype(BF16) for g in range(n_groups)],
        axis=1)
    d_attn = ya_ref.shape[2]
    out = (jnp.dot(ya_ref[0], wo_ref[:d_attn, :], preferred_element_type=F32)
           + jnp.dot(yc, wo_ref[d_attn:, :], preferred_element_type=F32))
    out = jnp.where(i == 0, hm_ref[...], hx_ref[...]) + out
    pos = i * tm + lax.broadcasted_iota(jnp.int32, (tm, 1), 0)
    o_ref[...] = jnp.where(pos >= n_pad, out, 0.0)


def _mix_out(hx, hm, hn, ya, w_in, w_out, layer, conv_w, og_conv, *, n_pad):
    d = hx.shape[1]
    bsz, lp, d_attn = ya.shape
    d_conv = d - d_attn
    assert d_conv == d_attn
    tm = SEQ_TILE
    h_blk = pl.BlockSpec((tm, d), _row_tile_of(bsz, lp // tm))
    kern = functools.partial(_mix_out_kernel, tm=tm, n_groups=d_conv // HEAD_DIM, n_pad=n_pad)
    const = dict(pipeline_mode=pl.Buffered(1))
    return pl.pallas_call(
        kern,
        grid=(bsz, lp // tm),
        in_specs=[
            *_residual_specs(hx, hm, bsz, lp // tm, tm),
            h_blk,
            pl.BlockSpec((1, tm, d_attn), lambda b, i: (b, i, 0)),
            pl.BlockSpec((None, d, 3 * d_conv), lambda b, i: (layer, 0, 1), **const),
            pl.BlockSpec((HALO, d_conv), lambda b, i: (0, 0)),
            pl.BlockSpec((1, d_conv), lambda b, i: (0, 0)),
            pl.BlockSpec((None, d_attn + d_conv, d), lambda b, i: (layer, 0, 0), **const),
        ],
        out_specs=h_blk,
        out_shape=jax.ShapeDtypeStruct((bsz * lp, d), F32),
        scratch_shapes=[pltpu.VMEM((tm + HALO, d_conv), F32)],
        compiler_params=_cparams(2),
        name="mix_out",
    )(hx, hm, hn, ya, w_in, conv_w, og_conv, w_out)


def _ffn_kernel(h_ref, g_ref, wg_ref, wu_ref, wd_ref, gf_ref, o_ref, hn_ref, *, final):
    f = pl.program_id(1)

    @pl.when(f == 0)
    def _norm():
        x = h_ref[...]
        hn_ref[...] = (x * _rms_scale(x) * g_ref[...]).astype(BF16)
        o_ref[...] = x

    hn = hn_ref[...]
    gate = jnp.dot(hn, wg_ref[...], preferred_element_type=F32)
    up = jnp.dot(hn, wu_ref[...], preferred_element_type=F32)
    act = (jax.nn.silu(gate) * up).astype(BF16)
    o_ref[...] += jnp.dot(act, wd_ref[...], preferred_element_type=F32)

    if final:
        @pl.when(f == pl.num_programs(1) - 1)
        def _final_norm():
            y = o_ref[...]
            o_ref[...] = y * _rms_scale(y) * gf_ref[...]


def _ffn(h, gain, w_gate, w_up, w_down, layer, final_gain, *, rows, final):
    d = h.shape[1]
    d_ff = w_gate.shape[2]
    tm = 1024 if rows % 1024 == 0 else 512
    tf = 512 if d_ff % 512 == 0 else 256
    return pl.pallas_call(
        functools.partial(_ffn_kernel, final=final),
        grid=(rows // tm, d_ff // tf),
        in_specs=[
            pl.BlockSpec((tm, d), lambda r, f: (r, 0)),
            pl.BlockSpec((1, d), lambda r, f: (0, 0)),
            pl.BlockSpec((None, d, tf), lambda r, f: (layer, 0, f)),
            pl.BlockSpec((None, d, tf), lambda r, f: (layer, 0, f)),
            pl.BlockSpec((None, tf, d), lambda r, f: (layer, f, 0)),
            pl.BlockSpec((1, d), lambda r, f: (0, 0)),
        ],
        out_specs=pl.BlockSpec((tm, d), lambda r, f: (r, 0)),
        out_shape=jax.ShapeDtypeStruct((rows, d), F32),
        scratch_shapes=[pltpu.VMEM((tm, d), BF16)],
        compiler_params=_cparams(2),
        name="ffn",
    )(h, gain, w_gate, w_up, w_down, final_gain)


def kernel(x, meta, norm_mix, w_in, b_f, conv_w, out_gain, w_out, norm_ffn, w_gate, w_up, w_down, final_norm):
    bsz, seq, d = x.shape
    n_meta = meta.shape[0]
    depth = w_in.shape[0]
    d_attn = d // 2
    n_heads = d_attn // HEAD_DIM
    assert n_heads <= HEAD_PAD and seq % SEQ_TILE == 0 and n_meta <= SEQ_TILE
    assert w_in.shape[2] == 6 * d_attn + n_heads
    lp = seq + SEQ_TILE
    n_pad = lp - seq - n_meta
    scale = HEAD_DIM ** -0.5

    m = jnp.broadcast_to(meta.astype(x.dtype)[None], (bsz, n_meta, d))
    head = jnp.concatenate([jnp.zeros((bsz, n_pad, d), x.dtype), m], axis=1)
    hx, hm = x.reshape(bsz * seq, d), head.reshape(bsz * SEQ_TILE, d)
    n_t = lp // SEQ_TILE
    final_gain = final_norm.reshape(1, d)

    w_in_bf, w_out_bf = w_in.astype(BF16), w_out.astype(BF16)
    w_gate_bf, w_up_bf, w_down_bf = w_gate.astype(BF16), w_up.astype(BF16), w_down.astype(BF16)
    for l in range(depth):
        wf_t = jnp.zeros((HEAD_PAD, d), BF16).at[:n_heads].set(w_in[l, :, 6 * d_attn:].T.astype(BF16))
        bf_col = jnp.zeros((HEAD_PAD, 1), F32).at[:n_heads, 0].set(b_f[l])
        cw = jnp.zeros((HALO, d_attn), F32).at[:CONV_WIDTH].set(conv_w[l])
        og_heads = out_gain[l, :d_attn].reshape(n_heads, 1, HEAD_DIM)
        og_conv = out_gain[l, d_attn:].reshape(1, d_attn)

        gain = norm_mix[l].reshape(1, d)
        q, k, vt, stats, hn = _attn_proj(hx, hm, gain, w_in_bf, l, wf_t, bf_col, bsz=bsz, n_t=n_t,
                                         n_pad=n_pad, scale=scale)
        ya = _flash(q, k, vt, stats[:, :, STAT_C, :, 0].reshape(-1),
                    _first_key_tile(stats, n_heads), og_heads)

        h = _mix_out(hx, hm, hn, ya, w_in_bf, w_out_bf, l, cw, og_conv, n_pad=n_pad)
        final = l == depth - 1
        h = _ffn(h, norm_ffn[l].reshape(1, d), w_gate_bf, w_up_bf, w_down_bf, l, final_gain,
                 rows=bsz * seq if final else h.shape[0], final=final)
        hx = hm = h

    return h.reshape(bsz, seq, d)
```

```python
import functools
import math

import jax
import jax.numpy as jnp
from jax import lax
from jax.experimental import pallas as pl
from jax.experimental.pallas import tpu as pltpu

F32 = jnp.float32
BF16 = jnp.bfloat16

HEAD_DIM = 128
AUG_DIM = 2 * HEAD_DIM
N_SPLIT = 3
CONV_WIDTH = 3
EPS = 1e-6
NEG = -1e30
FAKE = 2 * NEG
LOG2E = math.log2(math.e)
SEQ_TILE = 512
HEAD_PAD = 16
ONES_ROWS = 16
LANES = 128
STAT_C, STAT_BMAX, STAT_QN, STAT_KN, N_STAT = 0, 1, 2, 3, 4
SKIP_LOG2 = 140.0
HALO = 8
VMEM_LIMIT = 60 * 1024 * 1024

_NT = (((1,), (1,)), ((), ()))


def _cparams(n_axes):
    return pltpu.CompilerParams(dimension_semantics=("arbitrary",) * n_axes,
                                vmem_limit_bytes=VMEM_LIMIT)


def _rms_scale(x, axis=-1):
    return lax.rsqrt(jnp.mean(x * x, axis=axis, keepdims=True) + EPS)


def _cumsum_lanes(x):
    n = x.shape[1]
    lane = lax.broadcasted_iota(jnp.int32, x.shape, 1)
    shift = 1
    while shift < n:
        x = x + jnp.where(lane >= shift, pltpu.roll(x, shift, axis=1), 0.0)
        shift *= 2
    return x


def _round_bf16(x):
    return x.astype(BF16).astype(F32)


def _max_row_norm(x):
    xf = x.astype(F32)
    n2 = jnp.max(jnp.sum(xf * xf, axis=1, keepdims=True), axis=0, keepdims=True)
    return jnp.broadcast_to(jnp.sqrt(n2), (1, LANES))


def _attn_proj_kernel(hx_ref, hm_ref, g_ref, wqk_ref, wv_ref, wf_ref, bf_ref,
                      q_ref, k_ref, vt_ref, stat_ref, hn_ref, carry_ref, wvt_ref,
                      *, tm, n_heads, n_pad, scale):
    i = pl.program_id(1)
    d_attn = n_heads * HEAD_DIM

    @pl.when(i == 0)
    def _first_tile():
        carry_ref[...] = jnp.zeros_like(carry_ref)
        x = hm_ref[...]
        hn_ref[...] = (x * _rms_scale(x) * g_ref[...]).astype(BF16)

    @pl.when(i != 0)
    def _x_tile():
        x = hx_ref[...]
        hn_ref[...] = (x * _rms_scale(x) * g_ref[...]).astype(BF16)

    hn = hn_ref[...]

    @pl.when((pl.program_id(0) == 0) & (i == 0))
    def _transpose_v_weights():
        wvt_ref[...] = wv_ref[...].T

    logit = lax.dot_general(wf_ref[...], hn, _NT, preferred_element_type=F32)
    log_f = jax.nn.log_sigmoid(logit + bf_ref[...])
    r = _cumsum_lanes(log_f)
    stat_ref[0, 0] = jnp.zeros(stat_ref.shape[2:], F32)
    stat_ref[0, 0, STAT_C] = carry_ref[...] * LOG2E
    carry_ref[...] = carry_ref[...] + jnp.broadcast_to(r[:, tm - 1:tm], carry_ref.shape)
    pos = i * tm + lax.broadcasted_iota(jnp.int32, r.shape, 1)
    bias = jnp.where(pos >= n_pad, -LOG2E * r, NEG)
    stat_ref[0, 0, STAT_BMAX] = jnp.broadcast_to(jnp.max(bias, axis=1, keepdims=True),
                                                 (HEAD_PAD, LANES))
    bias_col = jnp.concatenate([bias, jnp.zeros((LANES - HEAD_PAD, tm), F32)], axis=0).T

    lane = lax.broadcasted_iota(jnp.int32, (tm, HEAD_DIM), 1)
    ones = jnp.where(lane < N_SPLIT, 1.0, 0.0).astype(BF16)
    zq = jnp.dot(hn, wqk_ref[:, :d_attn], preferred_element_type=F32)
    for h in range(n_heads):
        qh = (zq[:, h * HEAD_DIM:(h + 1) * HEAD_DIM] * (scale * LOG2E)).astype(BF16)
        q_ref[0, :, h * AUG_DIM:h * AUG_DIM + HEAD_DIM] = qh
        q_ref[0, :, h * AUG_DIM + HEAD_DIM:(h + 1) * AUG_DIM] = ones
        stat_ref[0, 0, STAT_QN, h:h + 1, :] = _max_row_norm(qh)

    zk = jnp.dot(hn, wqk_ref[:, d_attn:], preferred_element_type=F32)
    for h in range(n_heads):
        kh = zk[:, h * HEAD_DIM:(h + 1) * HEAD_DIM].astype(BF16)
        k_ref[0, :, h * AUG_DIM:h * AUG_DIM + HEAD_DIM] = kh
        stat_ref[0, 0, STAT_KN, h:h + 1, :] = _max_row_norm(kh)
        xb = jnp.broadcast_to(bias_col[:, h:h + 1], (tm, HEAD_DIM))
        hi = _round_bf16(xb)
        mid = _round_bf16(xb - hi)
        lo = xb - hi - mid
        aug = jnp.where(lane == 0, hi, jnp.where(lane == 1, mid, jnp.where(lane == 2, lo, 0.0)))
        k_ref[0, :, h * AUG_DIM + HEAD_DIM:(h + 1) * AUG_DIM] = aug.astype(BF16)

    zvt = lax.dot_general(wvt_ref[...], hn, _NT, preferred_element_type=F32)
    vt_ref[0] = zvt.astype(BF16)


def _row_tile_of(bsz, n_t):
    n_x = n_t - 1
    return lambda b, i: (jnp.where(i == 0, bsz * n_x + b, b * n_x + i - 1), 0)


def _residual_specs(hx, hm, bsz, n_t, tm):
    n_x = n_t - 1
    first_meta = hm.shape[0] // tm - bsz
    d = hx.shape[1]
    return (pl.BlockSpec((tm, d), lambda b, i: (b * n_x + jnp.maximum(i - 1, 0), 0)),
            pl.BlockSpec((tm, d), lambda b, i: (first_meta + b, 0), pipeline_mode=pl.Buffered(1)))


def _attn_proj(hx, hm, gain, w_in, layer, wf_t, bf_col, *, bsz, n_t, n_pad, scale):
    d = hx.shape[1]
    d_attn = d // 2
    n_heads = d_attn // HEAD_DIM
    tm = SEQ_TILE
    lp = n_t * tm
    kern = functools.partial(_attn_proj_kernel, tm=tm, n_heads=n_heads, n_pad=n_pad, scale=scale)
    aug_blk = pl.BlockSpec((1, tm, n_heads * AUG_DIM), lambda b, i: (b, i, 0))
    const = dict(pipeline_mode=pl.Buffered(1))
    return pl.pallas_call(
        kern,
        grid=(bsz, n_t),
        in_specs=[
            *_residual_specs(hx, hm, bsz, n_t, tm),
            pl.BlockSpec((1, d), lambda b, i: (0, 0)),
            pl.BlockSpec((None, d, 2 * d_attn), lambda b, i: (layer, 0, 0), **const),
            pl.BlockSpec((None, d, d_attn), lambda b, i: (layer, 0, 2), **const),
            pl.BlockSpec((HEAD_PAD, d), lambda b, i: (0, 0)),
            pl.BlockSpec((HEAD_PAD, 1), lambda b, i: (0, 0)),
        ],
        out_specs=[
            aug_blk, aug_blk,
            pl.BlockSpec((1, d_attn, tm), lambda b, i: (b, 0, i)),
            pl.BlockSpec((1, 1, N_STAT, HEAD_PAD, LANES), lambda b, i: (b, i, 0, 0, 0)),
            pl.BlockSpec((tm, d), _row_tile_of(bsz, n_t)),
        ],
        out_shape=[
            jax.ShapeDtypeStruct((bsz, lp, n_heads * AUG_DIM), BF16),
            jax.ShapeDtypeStruct((bsz, lp, n_heads * AUG_DIM), BF16),
            jax.ShapeDtypeStruct((bsz, d_attn, lp), BF16),
            jax.ShapeDtypeStruct((bsz, n_t, N_STAT, HEAD_PAD, LANES), F32),
            jax.ShapeDtypeStruct((bsz * lp, d), BF16),
        ],
        scratch_shapes=[pltpu.VMEM((HEAD_PAD, LANES), F32), pltpu.VMEM((d_attn, d), BF16)],
        compiler_params=_cparams(2),
        name="attn_proj",
    )(hx, hm, gain, w_in, w_in, wf_t, bf_col)


def _flash_kernel(c_ref, first_ref, q_ref, k_ref, vt_ref, g_ref, o_ref, acc_ref, sa_ref, sb_ref,
                  *, tq, n_q, hpb):
    b = pl.program_id(0)
    hp = pl.program_id(1)
    i = pl.program_id(2)
    heads = range(hpb)
    c_base = [(b * n_q) * HEAD_PAD + hp * hpb + hh for hh in heads]
    c_q = [c_ref[c_base[hh] + i * HEAD_PAD] for hh in heads]
    first = first_ref[(b * pl.num_programs(1) + hp) * n_q + i]
    n_unmasked = i - first
    last = first + jnp.maximum(n_unmasked - 1, 0)
    key = lax.broadcasted_iota(jnp.int32, (tq, tq), 0)
    qry = lax.broadcasted_iota(jnp.int32, (tq, tq), 1)
    ones_rows = jnp.ones((ONES_ROWS, tq), BF16)

    def scores_into(s_ref, j, masked=False):
        start = pl.multiple_of(j * tq, tq)
        col_max = []
        for hh in heads:
            cols = slice(hh * AUG_DIM, (hh + 1) * AUG_DIM)
            st = lax.dot_general(k_ref[0, pl.ds(start, tq), cols], q_ref[0, :, cols], _NT,
                                 preferred_element_type=F32)
            if masked:
                st = jnp.where(key <= qry, st, NEG)
            s_ref[hh] = st
            col_max.append(jnp.max(st, axis=0, keepdims=True))
        return tuple(col_max)

    def accumulate(s_ref, col_max, j, delta, m_all):
        start = pl.multiple_of(j * tq, tq)
        m_out = []
        for hh in heads:
            vt = vt_ref[0, hh * HEAD_DIM:(hh + 1) * HEAD_DIM, pl.ds(start, tq)]
            m_new = jnp.maximum(m_all[hh], col_max[hh] + delta[hh])
            alpha = jnp.exp2(m_all[hh] - m_new)
            p = jnp.exp2(s_ref[hh] - (m_new - delta[hh])).astype(BF16)
            pv = jnp.dot(jnp.concatenate([vt, ones_rows], axis=0), p,
                         preferred_element_type=F32)
            acc_ref[hh] = alpha * acc_ref[hh] + pv
            m_out.append(m_new)
        return tuple(m_out)

    def delta_of(j):
        return [c_q[hh] - c_ref[c_base[hh] + j * HEAD_PAD] for hh in heads]

    def tail_delta_of(j):
        real = delta_of(jnp.minimum(j, last))
        return [jnp.where(j < i, real[hh], FAKE) for hh in heads]

    acc_ref[...] = jnp.zeros_like(acc_ref)
    max_a = scores_into(sa_ref, first)
    m_all = tuple(jnp.full((1, tq), NEG, F32) for _ in heads)
    n_trips = jnp.maximum((n_unmasked - 1) // 2, 0)

    def trip(jj, carry):
        m_all, max_a = carry
        j0 = first + 2 * jj
        max_b = scores_into(sb_ref, j0 + 1)
        m_all = accumulate(sa_ref, max_a, j0, delta_of(j0), m_all)
        max_a = scores_into(sa_ref, j0 + 2)
        m_all = accumulate(sb_ref, max_b, j0 + 1, delta_of(j0 + 1), m_all)
        return m_all, max_a

    def two_trips(jj, carry):
        return trip(2 * jj + 1, trip(2 * jj, carry))

    def four_trips(jj, carry):
        return two_trips(2 * jj + 1, two_trips(2 * jj, carry))

    carry = lax.fori_loop(0, n_trips // 4, four_trips, (m_all, max_a))
    carry = lax.fori_loop(2 * (n_trips // 4), n_trips // 2, two_trips, carry)
    m_all, max_a = lax.fori_loop(2 * (n_trips // 2), n_trips, trip, carry)
    j0 = first + 2 * n_trips
    no_delta = [0.0 for _ in heads]

    @pl.when(n_unmasked - 2 * n_trips == 2)
    def _tail_two_unmasked():
        max_b = scores_into(sb_ref, j0 + 1)
        m_1 = accumulate(sa_ref, max_a, j0, delta_of(j0), m_all)
        max_d = scores_into(sa_ref, i, masked=True)
        m_2 = accumulate(sb_ref, max_b, j0 + 1, delta_of(j0 + 1), m_1)
        accumulate(sa_ref, max_d, i, no_delta, m_2)

    @pl.when(n_unmasked - 2 * n_trips < 2)
    def _tail_one_unmasked():
        max_d = scores_into(sb_ref, i, masked=True)
        m_1 = accumulate(sa_ref, max_a, jnp.minimum(j0, last), tail_delta_of(j0), m_all)
        accumulate(sb_ref, max_d, i, no_delta, m_1)

    for hh in heads:
        ot = acc_ref[hh, :HEAD_DIM, :] / acc_ref[hh, HEAD_DIM:HEAD_DIM + 1, :]
        ot = ot * _rms_scale(ot, axis=0)
        o_ref[0, :, hh * HEAD_DIM:(hh + 1) * HEAD_DIM] = (ot.T * g_ref[hh]).astype(BF16)


def _heads_per_block(n_heads):
    return 2 if n_heads % 2 == 0 else 1


def _first_key_tile(stats, n_heads):
    hpb = _heads_per_block(n_heads)
    st = stats[:, :, :, :n_heads, 0]
    c, bmax, qn, kn = (jnp.moveaxis(st[:, :, s], 1, 2) for s in
                       (STAT_C, STAT_BMAX, STAT_QN, STAT_KN))
    upper = (qn[..., :, None] * (kn[..., None, :] + kn[..., :, None]) * 1.001 + 1.0
             + (c[..., :, None] - c[..., None, :]) + bmax[..., None, :])
    n_t = c.shape[-1]
    earlier = jnp.arange(n_t)[None, :] < jnp.arange(n_t)[:, None]
    negligible = (upper < -SKIP_LOG2) & earlier
    first = jnp.min(jnp.where(negligible, n_t, jnp.arange(n_t)), axis=-1)
    first = first.reshape(first.shape[0], n_heads // hpb, hpb, n_t).min(axis=2)
    return first.astype(jnp.int32).reshape(-1)


def _flash(q, k, vt, c_tiles, first_tile, og_heads):
    bsz, lp, _ = q.shape
    n_heads = vt.shape[1] // HEAD_DIM
    hpb = _heads_per_block(n_heads)
    tq = SEQ_TILE
    n_q = lp // tq
    kern = functools.partial(_flash_kernel, tq=tq, n_q=n_q, hpb=hpb)
    resident = pl.Buffered(2)
    return pl.pallas_call(
        kern,
        grid=(bsz, n_heads // hpb, n_q),
        in_specs=[
            pl.BlockSpec(memory_space=pltpu.SMEM),
            pl.BlockSpec(memory_space=pltpu.SMEM),
            pl.BlockSpec((1, tq, hpb * AUG_DIM), lambda b, h, i: (b, i, h)),
            pl.BlockSpec((1, lp, hpb * AUG_DIM), lambda b, h, i: (b, 0, h),
                         pipeline_mode=resident),
            pl.BlockSpec((1, hpb * HEAD_DIM, lp), lambda b, h, i: (b, h, 0),
                         pipeline_mode=resident),
            pl.BlockSpec((hpb, 1, HEAD_DIM), lambda b, h, i: (h, 0, 0)),
        ],
        out_specs=pl.BlockSpec((1, tq, hpb * HEAD_DIM), lambda b, h, i: (b, i, h)),
        out_shape=jax.ShapeDtypeStruct((bsz, lp, n_heads * HEAD_DIM), BF16),
        scratch_shapes=[
            pltpu.VMEM((hpb, HEAD_DIM + ONES_ROWS, tq), F32),
            pltpu.VMEM((hpb, tq, tq), F32),
            pltpu.VMEM((hpb, tq, tq), F32),
        ],
        compiler_params=_cparams(3),
        name="flash",
    )(c_tiles, first_tile, q, k, vt, og_heads)


def _mix_out_kernel(hx_ref, hm_ref, hn_ref, ya_ref, wc_ref, cw_ref, og_ref, wo_ref, o_ref, u_ref,
                    *, tm, n_groups, n_pad):
    i = pl.program_id(1)
    d_conv = n_groups * HEAD_DIM
    hn = hn_ref[...]

    @pl.when(i == 0)
    def _reset():
        u_ref[pl.ds(0, HALO), :] = jnp.zeros((HALO, d_conv), F32)

    gate_b = jnp.dot(hn, wc_ref[:, :d_conv], preferred_element_type=F32)
    gate_c = jnp.dot(hn, wc_ref[:, d_conv:2 * d_conv], preferred_element_type=F32)
    hc = jnp.dot(hn, wc_ref[:, 2 * d_conv:], preferred_element_type=F32)
    u = gate_c * hc
    u_ref[pl.ds(HALO, tm), :] = u
    u1 = u_ref[pl.ds(HALO - 1, tm), :]
    u2 = u_ref[pl.ds(HALO - 2, tm), :]
    conv = cw_ref[0:1, :] * u2 + cw_ref[1:2, :] * u1 + cw_ref[2:3, :] * u
    y = gate_b * conv
    u_ref[pl.ds(0, HALO), :] = u_ref[pl.ds(tm, HALO), :]
    yc = jnp.concatenate(
        [(y[:, g * HEAD_DIM:(g + 1) * HEAD_DIM] * _rms_scale(y[:, g * HEAD_DIM:(g + 1) * HEAD_DIM])
          * og_ref[:, g * HEAD_DIM:(g + 1) * HEAD_DIM]).astype(BF16) for g in range(n_groups)],
        axis=1)
    d_attn = ya_ref.shape[2]
    out = (jnp.dot(ya_ref[0], wo_ref[:d_attn, :], preferred_element_type=F32)
           + jnp.dot(yc, wo_ref[d_attn:, :], preferred_element_type=F32))
    out = jnp.where(i == 0, hm_ref[...], hx_ref[...]) + out
    pos = i * tm + lax.broadcasted_iota(jnp.int32, (tm, 1), 0)
    o_ref[...] = jnp.where(pos >= n_pad, out, 0.0)


def _mix_out(hx, hm, hn, ya, w_in, w_out, layer, conv_w, og_conv, *, n_pad):
    d = hx.shape[1]
    bsz, lp, d_attn = ya.shape
    d_conv = d - d_attn
    assert d_conv == d_attn
    tm = SEQ_TILE
    h_blk = pl.BlockSpec((tm, d), _row_tile_of(bsz, lp // tm))
    kern = functools.partial(_mix_out_kernel, tm=tm, n_groups=d_conv // HEAD_DIM, n_pad=n_pad)
    const = dict(pipeline_mode=pl.Buffered(1))
    return pl.pallas_call(
        kern,
        grid=(bsz, lp // tm),
        in_specs=[
            *_residual_specs(hx, hm, bsz, lp // tm, tm),
            h_blk,
            pl.BlockSpec((1, tm, d_attn), lambda b, i: (b, i, 0)),
            pl.BlockSpec((None, d, 3 * d_conv), lambda b, i: (layer, 0, 1), **const),
            pl.BlockSpec((HALO, d_conv), lambda b, i: (0, 0)),
            pl.BlockSpec((1, d_conv), lambda b, i: (0, 0)),
            pl.BlockSpec((None, d_attn + d_conv, d), lambda b, i: (layer, 0, 0), **const),
        ],
        out_specs=h_blk,
        out_shape=jax.ShapeDtypeStruct((bsz * lp, d), F32),
        scratch_shapes=[pltpu.VMEM((tm + HALO, d_conv), F32)],
        compiler_params=_cparams(2),
        name="mix_out",
    )(hx, hm, hn, ya, w_in, conv_w, og_conv, w_out)


def _ffn_kernel(h_ref, g_ref, wg_ref, wu_ref, wd_ref, gf_ref, o_ref, hn_ref, *, final):
    f = pl.program_id(1)

    @pl.when(f == 0)
    def _norm():
        x = h_ref[...]
        hn_ref[...] = (x * _rms_scale(x) * g_ref[...]).astype(BF16)
        o_ref[...] = x

    hn = hn_ref[...]
    gate = jnp.dot(hn, wg_ref[...], preferred_element_type=F32)
    up = jnp.dot(hn, wu_ref[...], preferred_element_type=F32)
    act = (jax.nn.silu(gate) * up).astype(BF16)
    o_ref[...] += jnp.dot(act, wd_ref[...], preferred_element_type=F32)

    if final:
        @pl.when(f == pl.num_programs(1) - 1)
        def _final_norm():
            y = o_ref[...]
            o_ref[...] = y * _rms_scale(y) * gf_ref[...]


def _ffn(h, gain, w_gate, w_up, w_down, layer, final_gain, *, rows, final):
    d = h.shape[1]
    d_ff = w_gate.shape[2]
    tm = 1024 if rows % 1024 == 0 else 512
    tf = 512 if d_ff % 512 == 0 else 256
    return pl.pallas_call(
        functools.partial(_ffn_kernel, final=final),
        grid=(rows // tm, d_ff // tf),
        in_specs=[
            pl.BlockSpec((tm, d), lambda r, f: (r, 0)),
            pl.BlockSpec((1, d), lambda r, f: (0, 0)),
            pl.BlockSpec((None, d, tf), lambda r, f: (layer, 0, f)),
            pl.BlockSpec((None, d, tf), lambda r, f: (layer, 0, f)),
            pl.BlockSpec((None, tf, d), lambda r, f: (layer, f, 0)),
            pl.BlockSpec((1, d), lambda r, f: (0, 0)),
        ],
        out_specs=pl.BlockSpec((tm, d), lambda r, f: (r, 0)),
        out_shape=jax.ShapeDtypeStruct((rows, d), F32),
        scratch_shapes=[pltpu.VMEM((tm, d), BF16)],
        compiler_params=_cparams(2),
        name="ffn",
    )(h, gain, w_gate, w_up, w_down, final_gain)


def kernel(x, meta, norm_mix, w_in, b_f, conv_w, out_gain, w_out, norm_ffn, w_gate, w_up, w_down, final_norm):
    bsz, seq, d = x.shape
    n_meta = meta.shape[0]
    depth = w_in.shape[0]
    d_attn = d // 2
    n_heads = d_attn // HEAD_DIM
    assert n_heads <= HEAD_PAD and seq % SEQ_TILE == 0 and n_meta <= SEQ_TILE
    assert w_in.shape[2] == 6 * d_attn + n_heads
    lp = seq + SEQ_TILE
    n_pad = lp - seq - n_meta
    scale = HEAD_DIM ** -0.5

    m = jnp.broadcast_to(meta.astype(x.dtype)[None], (bsz, n_meta, d))
    head = jnp.concatenate([jnp.zeros((bsz, n_pad, d), x.dtype), m], axis=1)
    hx, hm = x.reshape(bsz * seq, d), head.reshape(bsz * SEQ_TILE, d)
    n_t = lp // SEQ_TILE
    final_gain = final_norm.reshape(1, d)

    w_in_bf, w_out_bf = w_in[:, :, :6 * d_attn].astype(BF16), w_out.astype(BF16)
    w_gate_bf, w_up_bf, w_down_bf = w_gate.astype(BF16), w_up.astype(BF16), w_down.astype(BF16)
    for l in range(depth):
        wf_t = jnp.zeros((HEAD_PAD, d), BF16).at[:n_heads].set(w_in[l, :, 6 * d_attn:].T.astype(BF16))
        bf_col = jnp.zeros((HEAD_PAD, 1), F32).at[:n_heads, 0].set(b_f[l])
        cw = jnp.zeros((HALO, d_attn), F32).at[:CONV_WIDTH].set(conv_w[l])
        og_heads = out_gain[l, :d_attn].reshape(n_heads, 1, HEAD_DIM)
        og_conv = out_gain[l, d_attn:].reshape(1, d_attn)

        gain = norm_mix[l].reshape(1, d)
        q, k, vt, stats, hn = _attn_proj(hx, hm, gain, w_in_bf, l, wf_t, bf_col, bsz=bsz, n_t=n_t,
                                         n_pad=n_pad, scale=scale)
        ya = _flash(q, k, vt, stats[:, :, STAT_C, :, 0].reshape(-1),
                    _first_key_tile(stats, n_heads), og_heads)

        h = _mix_out(hx, hm, hn, ya, w_in_bf, w_out_bf, l, cw, og_conv, n_pad=n_pad)
        final = l == depth - 1
        h = _ffn(h, norm_ffn[l].reshape(1, d), w_gate_bf, w_up_bf, w_down_bf, l, final_gain,
                 rows=bsz * seq if final else h.shape[0], final=final)
        hx = hm = h

    return h.reshape(bsz, seq, d)
```

```python
import functools
import math

import jax
import jax.numpy as jnp
from jax import lax
from jax.experimental import pallas as pl
from jax.experimental.pallas import tpu as pltpu

F32 = jnp.float32
BF16 = jnp.bfloat16

HEAD_DIM = 128
AUG_DIM = 2 * HEAD_DIM
N_SPLIT = 3
CONV_WIDTH = 3
EPS = 1e-6
NEG = -1e30
FAKE = 2 * NEG
LOG2E = math.log2(math.e)
SEQ_TILE = 512
HEAD_PAD = 16
ONES_ROWS = 16
LANES = 128
STAT_C, STAT_BMAX, STAT_QN, STAT_KN, N_STAT = 0, 1, 2, 3, 4
SKIP_LOG2 = 140.0
HALO = 8
VMEM_LIMIT = 60 * 1024 * 1024

_NT = (((1,), (1,)), ((), ()))


def _cparams(n_axes):
    return pltpu.CompilerParams(dimension_semantics=("arbitrary",) * n_axes,
                                vmem_limit_bytes=VMEM_LIMIT)


def _rms_scale(x, axis=-1):
    return lax.rsqrt(jnp.mean(x * x, axis=axis, keepdims=True) + EPS)


def _cumsum_lanes(x):
    n = x.shape[1]
    lane = lax.broadcasted_iota(jnp.int32, x.shape, 1)
    shift = 1
    while shift < n:
        x = x + jnp.where(lane >= shift, pltpu.roll(x, shift, axis=1), 0.0)
        shift *= 2
    return x


def _round_bf16(x):
    return x.astype(BF16).astype(F32)


def _max_row_norm(x):
    xf = x.astype(F32)
    n2 = jnp.max(jnp.sum(xf * xf, axis=1, keepdims=True), axis=0, keepdims=True)
    return jnp.broadcast_to(jnp.sqrt(n2), (1, LANES))


def _attn_proj_kernel(hx_ref, hm_ref, g_ref, wqk_ref, wv_ref, wf_ref, bf_ref,
                      q_ref, k_ref, vt_ref, stat_ref, hn_ref, carry_ref, wvt_ref,
                      *, tm, n_heads, n_pad, scale):
    i = pl.program_id(1)
    d_attn = n_heads * HEAD_DIM

    @pl.when(i == 0)
    def _first_tile():
        carry_ref[...] = jnp.zeros_like(carry_ref)
        x = hm_ref[...]
        hn_ref[...] = (x * _rms_scale(x) * g_ref[...]).astype(BF16)

    @pl.when(i != 0)
    def _x_tile():
        x = hx_ref[...]
        hn_ref[...] = (x * _rms_scale(x) * g_ref[...]).astype(BF16)

    hn = hn_ref[...]

    @pl.when((pl.program_id(0) == 0) & (i == 0))
    def _transpose_v_weights():
        wvt_ref[...] = wv_ref[...].T

    logit = lax.dot_general(wf_ref[...], hn, _NT, preferred_element_type=F32)
    log_f = jax.nn.log_sigmoid(logit + bf_ref[...])
    r = _cumsum_lanes(log_f)
    stat_ref[0, 0] = jnp.zeros(stat_ref.shape[2:], F32)
    stat_ref[0, 0, STAT_C] = carry_ref[...] * LOG2E
    carry_ref[...] = carry_ref[...] + jnp.broadcast_to(r[:, tm - 1:tm], carry_ref.shape)
    pos = i * tm + lax.broadcasted_iota(jnp.int32, r.shape, 1)
    bias = jnp.where(pos >= n_pad, -LOG2E * r, NEG)
    stat_ref[0, 0, STAT_BMAX] = jnp.broadcast_to(jnp.max(bias, axis=1, keepdims=True),
                                                 (HEAD_PAD, LANES))
    bias_col = jnp.concatenate([bias, jnp.zeros((LANES - HEAD_PAD, tm), F32)], axis=0).T

    lane = lax.broadcasted_iota(jnp.int32, (tm, HEAD_DIM), 1)
    ones = jnp.where(lane < N_SPLIT, 1.0, 0.0).astype(BF16)
    zq = jnp.dot(hn, wqk_ref[:, :d_attn], preferred_element_type=F32)
    for h in range(n_heads):
        qh = (zq[:, h * HEAD_DIM:(h + 1) * HEAD_DIM] * (scale * LOG2E)).astype(BF16)
        q_ref[0, :, h * AUG_DIM:h * AUG_DIM + HEAD_DIM] = qh
        q_ref[0, :, h * AUG_DIM + HEAD_DIM:(h + 1) * AUG_DIM] = ones
        stat_ref[0, 0, STAT_QN, h:h + 1, :] = _max_row_norm(qh)

    zk = jnp.dot(hn, wqk_ref[:, d_attn:], preferred_element_type=F32)
    for h in range(n_heads):
        kh = zk[:, h * HEAD_DIM:(h + 1) * HEAD_DIM].astype(BF16)
        k_ref[0, :, h * AUG_DIM:h * AUG_DIM + HEAD_DIM] = kh
        stat_ref[0, 0, STAT_KN, h:h + 1, :] = _max_row_norm(kh)
        xb = jnp.broadcast_to(bias_col[:, h:h + 1], (tm, HEAD_DIM))
        hi = _round_bf16(xb)
        mid = _round_bf16(xb - hi)
        lo = xb - hi - mid
        aug = jnp.where(lane == 0, hi, jnp.where(lane == 1, mid, jnp.where(lane == 2, lo, 0.0)))
        k_ref[0, :, h * AUG_DIM + HEAD_DIM:(h + 1) * AUG_DIM] = aug.astype(BF16)

    zvt = lax.dot_general(wvt_ref[...], hn, _NT, preferred_element_type=F32)
    vt_ref[0] = zvt.astype(BF16)


def _row_tile_of(bsz, n_t):
    n_x = n_t - 1
    return lambda b, i: (jnp.where(i == 0, bsz * n_x + b, b * n_x + i - 1), 0)


def _residual_specs(hx, hm, bsz, n_t, tm):
    n_x = n_t - 1
    first_meta = hm.shape[0] // tm - bsz
    d = hx.shape[1]
    return (pl.BlockSpec((tm, d), lambda b, i: (b * n_x + jnp.maximum(i - 1, 0), 0)),
            pl.BlockSpec((tm, d), lambda b, i: (first_meta + b, 0), pipeline_mode=pl.Buffered(1)))


def _attn_proj(hx, hm, gain, w_in, layer, wf_t, bf_col, *, bsz, n_t, n_pad, scale):
    d = hx.shape[1]
    d_attn = d // 2
    n_heads = d_attn // HEAD_DIM
    tm = SEQ_TILE
    lp = n_t * tm
    kern = functools.partial(_attn_proj_kernel, tm=tm, n_heads=n_heads, n_pad=n_pad, scale=scale)
    aug_blk = pl.BlockSpec((1, tm, n_heads * AUG_DIM), lambda b, i: (b, i, 0))
    const = dict(pipeline_mode=pl.Buffered(1))
    return pl.pallas_call(
        kern,
        grid=(bsz, n_t),
        in_specs=[
            *_residual_specs(hx, hm, bsz, n_t, tm),
            pl.BlockSpec((1, d), lambda b, i: (0, 0)),
            pl.BlockSpec((None, d, 2 * d_attn), lambda b, i: (layer, 0, 0), **const),
            pl.BlockSpec((None, d, d_attn), lambda b, i: (layer, 0, 2), **const),
            pl.BlockSpec((HEAD_PAD, d), lambda b, i: (0, 0)),
            pl.BlockSpec((HEAD_PAD, 1), lambda b, i: (0, 0)),
        ],
        out_specs=[
            aug_blk, aug_blk,
            pl.BlockSpec((1, d_attn, tm), lambda b, i: (b, 0, i)),
            pl.BlockSpec((1, 1, N_STAT, HEAD_PAD, LANES), lambda b, i: (b, i, 0, 0, 0)),
            pl.BlockSpec((tm, d), _row_tile_of(bsz, n_t)),
        ],
        out_shape=[
            jax.ShapeDtypeStruct((bsz, lp, n_heads * AUG_DIM), BF16),
            jax.ShapeDtypeStruct((bsz, lp, n_heads * AUG_DIM), BF16),
            jax.ShapeDtypeStruct((bsz, d_attn, lp), BF16),
            jax.ShapeDtypeStruct((bsz, n_t, N_STAT, HEAD_PAD, LANES), F32),
            jax.ShapeDtypeStruct((bsz * lp, d), BF16),
        ],
        scratch_shapes=[pltpu.VMEM((HEAD_PAD, LANES), F32), pltpu.VMEM((d_attn, d), BF16)],
        compiler_params=_cparams(2),
        name="attn_proj",
    )(hx, hm, gain, w_in, w_in, wf_t, bf_col)


def _flash_kernel(c_ref, first_ref, q_ref, qn_ref, k_ref, vt_ref, g_ref, o_ref,
                  acc_ref, sa_ref, sb_ref, maxa_ref, *, tq, n_q, hpb):
    b = pl.program_id(0)
    hp = pl.program_id(1)
    i = pl.program_id(2)
    heads = range(hpb)
    c_base = [(b * n_q) * HEAD_PAD + hp * hpb + hh for hh in heads]
    c_q = [c_ref[c_base[hh] + i * HEAD_PAD] for hh in heads]
    first_base = (b * pl.num_programs(1) + hp) * n_q
    first = first_ref[first_base + i]
    n_unmasked = i - first
    last = first + jnp.maximum(n_unmasked - 1, 0)
    key = lax.broadcasted_iota(jnp.int32, (tq, tq), 0)
    qry = lax.broadcasted_iota(jnp.int32, (tq, tq), 1)
    ones_rows = jnp.ones((ONES_ROWS, tq), BF16)

    def scores_into(s_ref, j, masked=False, queries=q_ref):
        start = pl.multiple_of(j * tq, tq)
        col_max = []
        for hh in heads:
            cols = slice(hh * AUG_DIM, (hh + 1) * AUG_DIM)
            st = lax.dot_general(k_ref[0, pl.ds(start, tq), cols], queries[0, :, cols], _NT,
                                 preferred_element_type=F32)
            if masked:
                st = jnp.where(key <= qry, st, NEG)
            s_ref[hh] = st
            col_max.append(jnp.max(st, axis=0, keepdims=True))
        return tuple(col_max)

    def accumulate(s_ref, col_max, j, delta, m_all):
        start = pl.multiple_of(j * tq, tq)
        m_out = []
        for hh in heads:
            vt = vt_ref[0, hh * HEAD_DIM:(hh + 1) * HEAD_DIM, pl.ds(start, tq)]
            m_new = jnp.maximum(m_all[hh], col_max[hh] + delta[hh])
            alpha = jnp.exp2(m_all[hh] - m_new)
            p = jnp.exp2(s_ref[hh] - (m_new - delta[hh])).astype(BF16)
            pv = jnp.dot(jnp.concatenate([vt, ones_rows], axis=0), p,
                         preferred_element_type=F32)
            acc_ref[hh] = alpha * acc_ref[hh] + pv
            m_out.append(m_new)
        return tuple(m_out)

    def delta_of(j):
        return [c_q[hh] - c_ref[c_base[hh] + j * HEAD_PAD] for hh in heads]

    def tail_delta_of(j):
        real = delta_of(jnp.minimum(j, last))
        return [jnp.where(j < i, real[hh], FAKE) for hh in heads]

    acc_ref[...] = jnp.zeros_like(acc_ref)

    @pl.when(i == 0)
    def _first_query_tile():
        for hh, col_max in enumerate(scores_into(sa_ref, first)):
            maxa_ref[hh] = col_max

    max_a = tuple(maxa_ref[hh] for hh in heads)
    m_all = tuple(jnp.full((1, tq), NEG, F32) for _ in heads)
    n_trips = jnp.maximum((n_unmasked - 1) // 2, 0)

    def trip(jj, carry):
        m_all, max_a = carry
        j0 = first + 2 * jj
        max_b = scores_into(sb_ref, j0 + 1)
        m_all = accumulate(sa_ref, max_a, j0, delta_of(j0), m_all)
        max_a = scores_into(sa_ref, j0 + 2)
        m_all = accumulate(sb_ref, max_b, j0 + 1, delta_of(j0 + 1), m_all)
        return m_all, max_a

    def two_trips(jj, carry):
        return trip(2 * jj + 1, trip(2 * jj, carry))

    def four_trips(jj, carry):
        return two_trips(2 * jj + 1, two_trips(2 * jj, carry))

    carry = lax.fori_loop(0, n_trips // 4, four_trips, (m_all, max_a))
    carry = lax.fori_loop(2 * (n_trips // 4), n_trips // 2, two_trips, carry)
    m_all, max_a = lax.fori_loop(2 * (n_trips // 2), n_trips, trip, carry)
    j0 = first + 2 * n_trips
    no_delta = [0.0 for _ in heads]

    def finish():
        first_next = first_ref[first_base + jnp.minimum(i + 1, n_q - 1)]
        for hh, col_max in enumerate(scores_into(sa_ref, first_next, queries=qn_ref)):
            maxa_ref[hh] = col_max
        for hh in heads:
            ot = acc_ref[hh, :HEAD_DIM, :] / acc_ref[hh, HEAD_DIM:HEAD_DIM + 1, :]
            ot = ot * _rms_scale(ot, axis=0)
            o_ref[0, :, hh * HEAD_DIM:(hh + 1) * HEAD_DIM] = (ot.T * g_ref[hh]).astype(BF16)

    @pl.when(n_unmasked - 2 * n_trips == 2)
    def _tail_two_unmasked():
        max_b = scores_into(sb_ref, j0 + 1)
        m_1 = accumulate(sa_ref, max_a, j0, delta_of(j0), m_all)
        max_d = scores_into(sa_ref, i, masked=True)
        m_2 = accumulate(sb_ref, max_b, j0 + 1, delta_of(j0 + 1), m_1)
        accumulate(sa_ref, max_d, i, no_delta, m_2)
        finish()

    @pl.when(n_unmasked - 2 * n_trips < 2)
    def _tail_one_unmasked():
        max_d = scores_into(sb_ref, i, masked=True)
        m_1 = accumulate(sa_ref, max_a, jnp.minimum(j0, last), tail_delta_of(j0), m_all)
        accumulate(sb_ref, max_d, i, no_delta, m_1)
        finish()


def _heads_per_block(n_heads):
    return 2 if n_heads % 2 == 0 else 1


def _first_key_tile(stats, n_heads):
    hpb = _heads_per_block(n_heads)
    st = stats[:, :, :, :n_heads, 0]
    c, bmax, qn, kn = (jnp.moveaxis(st[:, :, s], 1, 2) for s in
                       (STAT_C, STAT_BMAX, STAT_QN, STAT_KN))
    upper = (qn[..., :, None] * (kn[..., None, :] + kn[..., :, None]) * 1.001 + 1.0
             + (c[..., :, None] - c[..., None, :]) + bmax[..., None, :])
    n_t = c.shape[-1]
    earlier = jnp.arange(n_t)[None, :] < jnp.arange(n_t)[:, None]
    negligible = (upper < -SKIP_LOG2) & earlier
    first = jnp.min(jnp.where(negligible, n_t, jnp.arange(n_t)), axis=-1)
    first = first.reshape(first.shape[0], n_heads // hpb, hpb, n_t).min(axis=2)
    return first.astype(jnp.int32).reshape(-1)


def _flash(q, k, vt, c_tiles, first_tile, og_heads):
    bsz, lp, _ = q.shape
    n_heads = vt.shape[1] // HEAD_DIM
    hpb = _heads_per_block(n_heads)
    tq = SEQ_TILE
    n_q = lp // tq
    kern = functools.partial(_flash_kernel, tq=tq, n_q=n_q, hpb=hpb)
    resident = pl.Buffered(2)
    return pl.pallas_call(
        kern,
        grid=(bsz, n_heads // hpb, n_q),
        in_specs=[
            pl.BlockSpec(memory_space=pltpu.SMEM),
            pl.BlockSpec(memory_space=pltpu.SMEM),
            pl.BlockSpec((1, tq, hpb * AUG_DIM), lambda b, h, i: (b, i, h)),
            pl.BlockSpec((1, tq, hpb * AUG_DIM), lambda b, h, i: (b, jnp.minimum(i + 1, n_q - 1), h)),
            pl.BlockSpec((1, lp, hpb * AUG_DIM), lambda b, h, i: (b, 0, h),
                         pipeline_mode=resident),
            pl.BlockSpec((1, hpb * HEAD_DIM, lp), lambda b, h, i: (b, h, 0),
                         pipeline_mode=resident),
            pl.BlockSpec((hpb, 1, HEAD_DIM), lambda b, h, i: (h, 0, 0)),
        ],
        out_specs=pl.BlockSpec((1, tq, hpb * HEAD_DIM), lambda b, h, i: (b, i, h)),
        out_shape=jax.ShapeDtypeStruct((bsz, lp, n_heads * HEAD_DIM), BF16),
        scratch_shapes=[
            pltpu.VMEM((hpb, HEAD_DIM + ONES_ROWS, tq), F32),
            pltpu.VMEM((hpb, tq, tq), F32),
            pltpu.VMEM((hpb, tq, tq), F32),
            pltpu.VMEM((hpb, 1, tq), F32),
        ],
        compiler_params=_cparams(3),
        name="flash",
    )(c_tiles, first_tile, q, q, k, vt, og_heads)


def _mix_out_kernel(hx_ref, hm_ref, hn_ref, ya_ref, wc_ref, cw_ref, og_ref, wo_ref, o_ref, u_ref,
                    *, tm, n_groups, n_pad):
    i = pl.program_id(1)
    d_conv = n_groups * HEAD_DIM
    hn = hn_ref[...]

    @pl.when(i == 0)
    def _reset():
        u_ref[pl.ds(0, HALO), :] = jnp.zeros((HALO, d_conv), F32)

    gate_b = jnp.dot(hn, wc_ref[:, :d_conv], preferred_element_type=F32)
    gate_c = jnp.dot(hn, wc_ref[:, d_conv:2 * d_conv], preferred_element_type=F32)
    hc = jnp.dot(hn, wc_ref[:, 2 * d_conv:], preferred_element_type=F32)
    u = gate_c * hc
    u_ref[pl.ds(HALO, tm), :] = u
    u1 = u_ref[pl.ds(HALO - 1, tm), :]
    u2 = u_ref[pl.ds(HALO - 2, tm), :]
    conv = cw_ref[0:1, :] * u2 + cw_ref[1:2, :] * u1 + cw_ref[2:3, :] * u
    y = gate_b * conv
    u_ref[pl.ds(0, HALO), :] = u_ref[pl.ds(tm, HALO), :]
    yc = jnp.concatenate(
        [(y[:, g * HEAD_DIM:(g + 1) * HEAD_DIM] * _rms_scale(y[:, g * HEAD_DIM:(g + 1) * HEAD_DIM])
          * og_ref[:, g * HEAD_DIM:(g + 1) * HEAD_DIM]).astype(BF16) for g in range(n_groups)],
        axis=1)
    d_attn = ya_ref.shape[2]
    out = (jnp.dot(ya_ref[0], wo_ref[:d_attn, :], preferred_element_type=F32)
           + jnp.dot(yc, wo_ref[d_attn:, :], preferred_element_type=F32))
    out = jnp.where(i == 0, hm_ref[...], hx_ref[...]) + out
    pos = i * tm + lax.broadcasted_iota(jnp.int32, (tm, 1), 0)
    o_ref[...] = jnp.where(pos >= n_pad, out, 0.0)


def _mix_out(hx, hm, hn, ya, w_in, w_out, layer, conv_w, og_conv, *, n_pad):
    d = hx.shape[1]
    bsz, lp, d_attn = ya.shape
    d_conv = d - d_attn
    assert d_conv == d_attn
    tm = SEQ_TILE
    h_blk = pl.BlockSpec((tm, d), _row_tile_of(bsz, lp // tm))
    kern = functools.partial(_mix_out_kernel, tm=tm, n_groups=d_conv // HEAD_DIM, n_pad=n_pad)
    const = dict(pipeline_mode=pl.Buffered(1))
    return pl.pallas_call(
        kern,
        grid=(bsz, lp // tm),
        in_specs=[
            *_residual_specs(hx, hm, bsz, lp // tm, tm),
            h_blk,
            pl.BlockSpec((1, tm, d_attn), lambda b, i: (b, i, 0)),
            pl.BlockSpec((None, d, 3 * d_conv), lambda b, i: (layer, 0, 1), **const),
            pl.BlockSpec((HALO, d_conv), lambda b, i: (0, 0)),
            pl.BlockSpec((1, d_conv), lambda b, i: (0, 0)),
            pl.BlockSpec((None, d_attn + d_conv, d), lambda b, i: (layer, 0, 0), **const),
        ],
        out_specs=h_blk,
        out_shape=jax.ShapeDtypeStruct((bsz * lp, d), F32),
        scratch_shapes=[pltpu.VMEM((tm + HALO, d_conv), F32)],
        compiler_params=_cparams(2),
        name="mix_out",
    )(hx, hm, hn, ya, w_in, conv_w, og_conv, w_out)


def _ffn_kernel(h_ref, g_ref, wg_ref, wu_ref, wd_ref, gf_ref, o_ref, hn_ref, *, final):
    f = pl.program_id(1)

    @pl.when(f == 0)
    def _norm():
        x = h_ref[...]
        hn_ref[...] = (x * _rms_scale(x) * g_ref[...]).astype(BF16)
        o_ref[...] = x

    hn = hn_ref[...]
    gate = jnp.dot(hn, wg_ref[...], preferred_element_type=F32)
    up = jnp.dot(hn, wu_ref[...], preferred_element_type=F32)
    act = (jax.nn.silu(gate) * up).astype(BF16)
    o_ref[...] += jnp.dot(act, wd_ref[...], preferred_element_type=F32)

    if final:
        @pl.when(f == pl.num_programs(1) - 1)
        def _final_norm():
            y = o_ref[...]
            o_ref[...] = y * _rms_scale(y) * gf_ref[...]


def _ffn(h, gain, w_gate, w_up, w_down, layer, final_gain, *, rows, final):
    d = h.shape[1]
    d_ff = w_gate.shape[2]
    tm = 1024 if rows % 1024 == 0 else 512
    tf = 512 if d_ff % 512 == 0 else 256
    return pl.pallas_call(
        functools.partial(_ffn_kernel, final=final),
        grid=(rows // tm, d_ff // tf),
        in_specs=[
            pl.BlockSpec((tm, d), lambda r, f: (r, 0)),
            pl.BlockSpec((1, d), lambda r, f: (0, 0)),
            pl.BlockSpec((None, d, tf), lambda r, f: (layer, 0, f)),
            pl.BlockSpec((None, d, tf), lambda r, f: (layer, 0, f)),
            pl.BlockSpec((None, tf, d), lambda r, f: (layer, f, 0)),
            pl.BlockSpec((1, d), lambda r, f: (0, 0)),
        ],
        out_specs=pl.BlockSpec((tm, d), lambda r, f: (r, 0)),
        out_shape=jax.ShapeDtypeStruct((rows, d), F32),
        scratch_shapes=[pltpu.VMEM((tm, d), BF16)],
        compiler_params=_cparams(2),
        name="ffn",
    )(h, gain, w_gate, w_up, w_down, final_gain)


def kernel(x, meta, norm_mix, w_in, b_f, conv_w, out_gain, w_out, norm_ffn, w_gate, w_up, w_down, final_norm):
    bsz, seq, d = x.shape
    n_meta = meta.shape[0]
    depth = w_in.shape[0]
    d_attn = d // 2
    n_heads = d_attn // HEAD_DIM
    assert n_heads <= HEAD_PAD and seq % SEQ_TILE == 0 and n_meta <= SEQ_TILE
    assert w_in.shape[2] == 6 * d_attn + n_heads
    lp = seq + SEQ_TILE
    n_pad = lp - seq - n_meta
    scale = HEAD_DIM ** -0.5

    m = jnp.broadcast_to(meta.astype(x.dtype)[None], (bsz, n_meta, d))
    head = jnp.concatenate([jnp.zeros((bsz, n_pad, d), x.dtype), m], axis=1)
    hx, hm = x.reshape(bsz * seq, d), head.reshape(bsz * SEQ_TILE, d)
    n_t = lp // SEQ_TILE
    final_gain = final_norm.reshape(1, d)

    w_in_bf, w_out_bf = w_in.astype(BF16), w_out.astype(BF16)
    w_gate_bf, w_up_bf, w_down_bf = w_gate.astype(BF16), w_up.astype(BF16), w_down.astype(BF16)
    for l in range(depth):
        wf_t = jnp.zeros((HEAD_PAD, d), BF16).at[:n_heads].set(w_in[l, :, 6 * d_attn:].T.astype(BF16))
        bf_col = jnp.zeros((HEAD_PAD, 1), F32).at[:n_heads, 0].set(b_f[l])
        cw = jnp.zeros((HALO, d_attn), F32).at[:CONV_WIDTH].set(conv_w[l])
        og_heads = out_gain[l, :d_attn].reshape(n_heads, 1, HEAD_DIM)
        og_conv = out_gain[l, d_attn:].reshape(1, d_attn)

        gain = norm_mix[l].reshape(1, d)
        q, k, vt, stats, hn = _attn_proj(hx, hm, gain, w_in_bf, l, wf_t, bf_col, bsz=bsz, n_t=n_t,
                                         n_pad=n_pad, scale=scale)
        ya = _flash(q, k, vt, stats[:, :, STAT_C, :, 0].reshape(-1),
                    _first_key_tile(stats, n_heads), og_heads)

        h = _mix_out(hx, hm, hn, ya, w_in_bf, w_out_bf, l, cw, og_conv, n_pad=n_pad)
        final = l == depth - 1
        h = _ffn(h, norm_ffn[l].reshape(1, d), w_gate_bf, w_up_bf, w_down_bf, l, final_gain,
                 rows=bsz * seq if final else h.shape[0], final=final)
        hx = hm = h

    return h.reshape(bsz, seq, d)
```

```python
import functools
import math

import jax
import jax.numpy as jnp
from jax import lax
from jax.experimental import pallas as pl
from jax.experimental.pallas import tpu as pltpu

F32 = jnp.float32
BF16 = jnp.bfloat16

HEAD_DIM = 128
AUG_DIM = 2 * HEAD_DIM
N_SPLIT = 3
CONV_WIDTH = 3
EPS = 1e-6
NEG = -1e30
FAKE = 2 * NEG
LOG2E = math.log2(math.e)
SEQ_TILE = 512
HEAD_PAD = 16
ONES_ROWS = 16
LANES = 128
STAT_C, STAT_BMAX, STAT_QN, STAT_KN, N_STAT = 0, 1, 2, 3, 4
SKIP_LOG2 = 140.0
BOUND_REL_SLACK = 1.001
BOUND_ABS_SLACK = 1.0
HALO = 8
VMEM_LIMIT = 60 * 1024 * 1024

_NT = (((1,), (1,)), ((), ()))


def _cparams(n_axes):
    return pltpu.CompilerParams(dimension_semantics=("arbitrary",) * n_axes,
                                vmem_limit_bytes=VMEM_LIMIT)


def _rms_scale(x, axis=-1):
    return lax.rsqrt(jnp.mean(x * x, axis=axis, keepdims=True) + EPS)


def _cumsum_lanes(x):
    n = x.shape[1]
    lane = lax.broadcasted_iota(jnp.int32, x.shape, 1)
    shift = 1
    while shift < n:
        x = x + jnp.where(lane >= shift, pltpu.roll(x, shift, axis=1), 0.0)
        shift *= 2
    return x


def _round_bf16(x):
    return x.astype(BF16).astype(F32)


def _max_row_norm(x):
    xf = x.astype(F32)
    n2 = jnp.max(jnp.sum(xf * xf, axis=1, keepdims=True), axis=0, keepdims=True)
    return jnp.broadcast_to(jnp.sqrt(n2), (1, LANES))


def _attn_proj_kernel(hx_ref, hm_ref, g_ref, wqk_ref, wv_ref, wf_ref, bf_ref,
                      q_ref, k_ref, vt_ref, stat_ref, hn_ref, carry_ref, wvt_ref,
                      *, tm, n_heads, n_pad, scale):
    i = pl.program_id(1)
    d_attn = n_heads * HEAD_DIM

    @pl.when(i == 0)
    def _first_tile():
        carry_ref[...] = jnp.zeros_like(carry_ref)
        x = hm_ref[...]
        hn_ref[...] = (x * _rms_scale(x) * g_ref[...]).astype(BF16)

    @pl.when(i != 0)
    def _x_tile():
        x = hx_ref[...]
        hn_ref[...] = (x * _rms_scale(x) * g_ref[...]).astype(BF16)

    hn = hn_ref[...]

    @pl.when((pl.program_id(0) == 0) & (i == 0))
    def _transpose_v_weights():
        wvt_ref[...] = wv_ref[...].T

    logit = lax.dot_general(wf_ref[...], hn, _NT, preferred_element_type=F32)
    log_f = jax.nn.log_sigmoid(logit + bf_ref[...])
    r = _cumsum_lanes(log_f)
    stat_ref[0, 0] = jnp.zeros(stat_ref.shape[2:], F32)
    stat_ref[0, 0, STAT_C] = carry_ref[...] * LOG2E
    carry_ref[...] = carry_ref[...] + jnp.broadcast_to(r[:, tm - 1:tm], carry_ref.shape)
    pos = i * tm + lax.broadcasted_iota(jnp.int32, r.shape, 1)
    bias = jnp.where(pos >= n_pad, -LOG2E * r, NEG)
    stat_ref[0, 0, STAT_BMAX] = jnp.broadcast_to(jnp.max(bias, axis=1, keepdims=True),
                                                 (HEAD_PAD, LANES))
    bias_col = jnp.concatenate([bias, jnp.zeros((LANES - HEAD_PAD, tm), F32)], axis=0).T

    lane = lax.broadcasted_iota(jnp.int32, (tm, HEAD_DIM), 1)
    ones = jnp.where(lane < N_SPLIT, 1.0, 0.0).astype(BF16)
    zq = jnp.dot(hn, wqk_ref[:, :d_attn], preferred_element_type=F32)
    for h in range(n_heads):
        qh = (zq[:, h * HEAD_DIM:(h + 1) * HEAD_DIM] * (scale * LOG2E)).astype(BF16)
        q_ref[0, :, h * AUG_DIM:h * AUG_DIM + HEAD_DIM] = qh
        q_ref[0, :, h * AUG_DIM + HEAD_DIM:(h + 1) * AUG_DIM] = ones
        stat_ref[0, 0, STAT_QN, h:h + 1, :] = _max_row_norm(qh)

    zk = jnp.dot(hn, wqk_ref[:, d_attn:], preferred_element_type=F32)
    for h in range(n_heads):
        kh = zk[:, h * HEAD_DIM:(h + 1) * HEAD_DIM].astype(BF16)
        k_ref[0, :, h * AUG_DIM:h * AUG_DIM + HEAD_DIM] = kh
        stat_ref[0, 0, STAT_KN, h:h + 1, :] = _max_row_norm(kh)
        xb = jnp.broadcast_to(bias_col[:, h:h + 1], (tm, HEAD_DIM))
        hi = _round_bf16(xb)
        mid = _round_bf16(xb - hi)
        lo = xb - hi - mid
        aug = jnp.where(lane == 0, hi, jnp.where(lane == 1, mid, jnp.where(lane == 2, lo, 0.0)))
        k_ref[0, :, h * AUG_DIM + HEAD_DIM:(h + 1) * AUG_DIM] = aug.astype(BF16)

    zvt = lax.dot_general(wvt_ref[...], hn, _NT, preferred_element_type=F32)
    vt_ref[0] = zvt.astype(BF16)


def _row_tile_of(bsz, n_t):
    n_x = n_t - 1
    return lambda b, i: (jnp.where(i == 0, bsz * n_x + b, b * n_x + i - 1), 0)


def _residual_specs(hx, hm, bsz, n_t, tm):
    n_x = n_t - 1
    first_meta = hm.shape[0] // tm - bsz
    d = hx.shape[1]
    return (pl.BlockSpec((tm, d), lambda b, i: (b * n_x + jnp.maximum(i - 1, 0), 0)),
            pl.BlockSpec((tm, d), lambda b, i: (first_meta + b, 0), pipeline_mode=pl.Buffered(1)))


def _attn_proj(hx, hm, gain, w_in, layer, wf_t, bf_col, *, bsz, n_t, n_pad, scale):
    d = hx.shape[1]
    d_attn = d // 2
    n_heads = d_attn // HEAD_DIM
    tm = SEQ_TILE
    lp = n_t * tm
    kern = functools.partial(_attn_proj_kernel, tm=tm, n_heads=n_heads, n_pad=n_pad, scale=scale)
    aug_blk = pl.BlockSpec((1, tm, n_heads * AUG_DIM), lambda b, i: (b, i, 0))
    const = dict(pipeline_mode=pl.Buffered(1))
    return pl.pallas_call(
        kern,
        grid=(bsz, n_t),
        in_specs=[
            *_residual_specs(hx, hm, bsz, n_t, tm),
            pl.BlockSpec((1, d), lambda b, i: (0, 0)),
            pl.BlockSpec((None, d, 2 * d_attn), lambda b, i: (layer, 0, 0), **const),
            pl.BlockSpec((None, d, d_attn), lambda b, i: (layer, 0, 2), **const),
            pl.BlockSpec((HEAD_PAD, d), lambda b, i: (0, 0)),
            pl.BlockSpec((HEAD_PAD, 1), lambda b, i: (0, 0)),
        ],
        out_specs=[
            aug_blk, aug_blk,
            pl.BlockSpec((1, d_attn, tm), lambda b, i: (b, 0, i)),
            pl.BlockSpec((1, 1, N_STAT, HEAD_PAD, LANES), lambda b, i: (b, i, 0, 0, 0)),
            pl.BlockSpec((tm, d), _row_tile_of(bsz, n_t)),
        ],
        out_shape=[
            jax.ShapeDtypeStruct((bsz, lp, n_heads * AUG_DIM), BF16),
            jax.ShapeDtypeStruct((bsz, lp, n_heads * AUG_DIM), BF16),
            jax.ShapeDtypeStruct((bsz, d_attn, lp), BF16),
            jax.ShapeDtypeStruct((bsz, n_t, N_STAT, HEAD_PAD, LANES), F32),
            jax.ShapeDtypeStruct((bsz * lp, d), BF16),
        ],
        scratch_shapes=[pltpu.VMEM((HEAD_PAD, LANES), F32), pltpu.VMEM((d_attn, d), BF16)],
        compiler_params=_cparams(2),
        name="attn_proj",
    )(hx, hm, gain, w_in, w_in, wf_t, bf_col)


def _flash_kernel(c_ref, first_ref, q_ref, qn_ref, k_ref, vt_ref, g_ref, o_ref,
                  acc_ref, sa_ref, sb_ref, maxa_ref, *, tq, n_q, hpb):
    b = pl.program_id(0)
    hp = pl.program_id(1)
    i = pl.program_id(2)
    heads = range(hpb)
    c_base = [(b * n_q) * HEAD_PAD + hp * hpb + hh for hh in heads]
    c_q = [c_ref[c_base[hh] + i * HEAD_PAD] for hh in heads]
    first_base = (b * pl.num_programs(1) + hp) * n_q
    first = first_ref[first_base + i]
    n_unmasked = i - first
    last = first + jnp.maximum(n_unmasked - 1, 0)
    key = lax.broadcasted_iota(jnp.int32, (tq, tq), 0)
    qry = lax.broadcasted_iota(jnp.int32, (tq, tq), 1)
    ones_rows = jnp.ones((ONES_ROWS, tq), BF16)

    def scores_into(s_ref, j, masked=False, queries=q_ref):
        start = pl.multiple_of(j * tq, tq)
        col_max = []
        for hh in heads:
            cols = slice(hh * AUG_DIM, (hh + 1) * AUG_DIM)
            st = lax.dot_general(k_ref[0, pl.ds(start, tq), cols], queries[0, :, cols], _NT,
                                 preferred_element_type=F32)
            if masked:
                st = jnp.where(key <= qry, st, NEG)
            s_ref[hh] = st
            col_max.append(jnp.max(st, axis=0, keepdims=True))
        return tuple(col_max)

    def accumulate(s_ref, col_max, j, delta, m_all):
        start = pl.multiple_of(j * tq, tq)
        m_out = []
        for hh in heads:
            vt = vt_ref[0, hh * HEAD_DIM:(hh + 1) * HEAD_DIM, pl.ds(start, tq)]
            m_new = jnp.maximum(m_all[hh], col_max[hh] + delta[hh])
            alpha = jnp.exp2(m_all[hh] - m_new)
            p = jnp.exp2(s_ref[hh] - (m_new - delta[hh])).astype(BF16)
            pv = jnp.dot(jnp.concatenate([vt, ones_rows], axis=0), p,
                         preferred_element_type=F32)
            acc_ref[hh] = alpha * acc_ref[hh] + pv
            m_out.append(m_new)
        return tuple(m_out)

    def delta_of(j):
        return [c_q[hh] - c_ref[c_base[hh] + j * HEAD_PAD] for hh in heads]

    def tail_delta_of(j):
        real = delta_of(jnp.minimum(j, last))
        return [jnp.where(j < i, real[hh], FAKE) for hh in heads]

    acc_ref[...] = jnp.zeros_like(acc_ref)

    @pl.when(i == 0)
    def _first_query_tile():
        for hh, col_max in enumerate(scores_into(sa_ref, first)):
            maxa_ref[hh] = col_max

    max_a = tuple(maxa_ref[hh] for hh in heads)
    m_all = tuple(jnp.full((1, tq), NEG, F32) for _ in heads)
    n_trips = jnp.maximum((n_unmasked - 1) // 2, 0)

    def trip(jj, carry):
        m_all, max_a = carry
        j0 = first + 2 * jj
        max_b = scores_into(sb_ref, j0 + 1)
        m_all = accumulate(sa_ref, max_a, j0, delta_of(j0), m_all)
        max_a = scores_into(sa_ref, j0 + 2)
        m_all = accumulate(sb_ref, max_b, j0 + 1, delta_of(j0 + 1), m_all)
        return m_all, max_a

    def two_trips(jj, carry):
        return trip(2 * jj + 1, trip(2 * jj, carry))

    def four_trips(jj, carry):
        return two_trips(2 * jj + 1, two_trips(2 * jj, carry))

    carry = lax.fori_loop(0, n_trips // 4, four_trips, (m_all, max_a))
    carry = lax.fori_loop(2 * (n_trips // 4), n_trips // 2, two_trips, carry)
    m_all, max_a = lax.fori_loop(2 * (n_trips // 2), n_trips, trip, carry)
    j0 = first + 2 * n_trips
    no_delta = [0.0 for _ in heads]

    def finish():
        first_next = first_ref[first_base + jnp.minimum(i + 1, n_q - 1)]
        for hh, col_max in enumerate(scores_into(sa_ref, first_next, queries=qn_ref)):
            maxa_ref[hh] = col_max
        for hh in heads:
            ot = acc_ref[hh, :HEAD_DIM, :] / acc_ref[hh, HEAD_DIM:HEAD_DIM + 1, :]
            ot = ot * _rms_scale(ot, axis=0)
            o_ref[0, :, hh * HEAD_DIM:(hh + 1) * HEAD_DIM] = (ot.T * g_ref[hh]).astype(BF16)

    @pl.when(n_unmasked - 2 * n_trips == 2)
    def _tail_two_unmasked():
        max_b = scores_into(sb_ref, j0 + 1)
        m_1 = accumulate(sa_ref, max_a, j0, delta_of(j0), m_all)
        max_d = scores_into(sa_ref, i, masked=True)
        m_2 = accumulate(sb_ref, max_b, j0 + 1, delta_of(j0 + 1), m_1)
        accumulate(sa_ref, max_d, i, no_delta, m_2)
        finish()

    @pl.when(n_unmasked - 2 * n_trips < 2)
    def _tail_one_unmasked():
        max_d = scores_into(sb_ref, i, masked=True)
        m_1 = accumulate(sa_ref, max_a, jnp.minimum(j0, last), tail_delta_of(j0), m_all)
        accumulate(sb_ref, max_d, i, no_delta, m_1)
        finish()


def _heads_per_block(n_heads):
    return 2 if n_heads % 2 == 0 else 1


def _first_key_tile(stats, n_heads):
    hpb = _heads_per_block(n_heads)
    st = stats[:, :, :, :n_heads, 0]
    c, bmax, qn, kn = (jnp.moveaxis(st[:, :, s], 1, 2) for s in
                       (STAT_C, STAT_BMAX, STAT_QN, STAT_KN))
    upper = (qn[..., :, None] * (kn[..., None, :] + kn[..., :, None]) * BOUND_REL_SLACK
             + BOUND_ABS_SLACK
             + (c[..., :, None] - c[..., None, :]) + bmax[..., None, :])
    n_t = c.shape[-1]
    earlier = jnp.arange(n_t)[None, :] < jnp.arange(n_t)[:, None]
    negligible = (upper < -SKIP_LOG2) & earlier
    first = jnp.min(jnp.where(negligible, n_t, jnp.arange(n_t)), axis=-1)
    first = first.reshape(first.shape[0], n_heads // hpb, hpb, n_t).min(axis=2)
    return first.astype(jnp.int32).reshape(-1)


def _flash(q, k, vt, c_tiles, first_tile, og_heads):
    bsz, lp, _ = q.shape
    n_heads = vt.shape[1] // HEAD_DIM
    hpb = _heads_per_block(n_heads)
    tq = SEQ_TILE
    n_q = lp // tq
    kern = functools.partial(_flash_kernel, tq=tq, n_q=n_q, hpb=hpb)
    resident = pl.Buffered(2)
    return pl.pallas_call(
        kern,
        grid=(bsz, n_heads // hpb, n_q),
        in_specs=[
            pl.BlockSpec(memory_space=pltpu.SMEM),
            pl.BlockSpec(memory_space=pltpu.SMEM),
            pl.BlockSpec((1, tq, hpb * AUG_DIM), lambda b, h, i: (b, i, h)),
            pl.BlockSpec((1, tq, hpb * AUG_DIM), lambda b, h, i: (b, jnp.minimum(i + 1, n_q - 1), h)),
            pl.BlockSpec((1, lp, hpb * AUG_DIM), lambda b, h, i: (b, 0, h),
                         pipeline_mode=resident),
            pl.BlockSpec((1, hpb * HEAD_DIM, lp), lambda b, h, i: (b, h, 0),
                         pipeline_mode=resident),
            pl.BlockSpec((hpb, 1, HEAD_DIM), lambda b, h, i: (h, 0, 0)),
        ],
        out_specs=pl.BlockSpec((1, tq, hpb * HEAD_DIM), lambda b, h, i: (b, i, h)),
        out_shape=jax.ShapeDtypeStruct((bsz, lp, n_heads * HEAD_DIM), BF16),
        scratch_shapes=[
            pltpu.VMEM((hpb, HEAD_DIM + ONES_ROWS, tq), F32),
            pltpu.VMEM((hpb, tq, tq), F32),
            pltpu.VMEM((hpb, tq, tq), F32),
            pltpu.VMEM((hpb, 1, tq), F32),
        ],
        compiler_params=_cparams(3),
        name="flash",
    )(c_tiles, first_tile, q, q, k, vt, og_heads)


def _mix_out_kernel(hx_ref, hm_ref, hn_ref, ya_ref, wc_ref, cw_ref, og_ref, wo_ref, o_ref, u_ref,
                    *, tm, n_groups, n_pad):
    i = pl.program_id(1)
    d_conv = n_groups * HEAD_DIM
    hn = hn_ref[...]

    @pl.when(i == 0)
    def _reset():
        u_ref[pl.ds(0, HALO), :] = jnp.zeros((HALO, d_conv), F32)

    gate_b = jnp.dot(hn, wc_ref[:, :d_conv], preferred_element_type=F32)
    gate_c = jnp.dot(hn, wc_ref[:, d_conv:2 * d_conv], preferred_element_type=F32)
    hc = jnp.dot(hn, wc_ref[:, 2 * d_conv:], preferred_element_type=F32)
    u = gate_c * hc
    u_ref[pl.ds(HALO, tm), :] = u
    u1 = u_ref[pl.ds(HALO - 1, tm), :]
    u2 = u_ref[pl.ds(HALO - 2, tm), :]
    conv = cw_ref[0:1, :] * u2 + cw_ref[1:2, :] * u1 + cw_ref[2:3, :] * u
    y = gate_b * conv
    u_ref[pl.ds(0, HALO), :] = u_ref[pl.ds(tm, HALO), :]
    yc = jnp.concatenate(
        [(y[:, g * HEAD_DIM:(g + 1) * HEAD_DIM] * _rms_scale(y[:, g * HEAD_DIM:(g + 1) * HEAD_DIM])
          * og_ref[:, g * HEAD_DIM:(g + 1) * HEAD_DIM]).astype(BF16) for g in range(n_groups)],
        axis=1)
    d_attn = ya_ref.shape[2]
    out = (jnp.dot(ya_ref[0], wo_ref[:d_attn, :], preferred_element_type=F32)
           + jnp.dot(yc, wo_ref[d_attn:, :], preferred_element_type=F32))
    out = jnp.where(i == 0, hm_ref[...], hx_ref[...]) + out
    pos = i * tm + lax.broadcasted_iota(jnp.int32, (tm, 1), 0)
    o_ref[...] = jnp.where(pos >= n_pad, out, 0.0)


def _mix_out(hx, hm, hn, ya, w_in, w_out, layer, conv_w, og_conv, *, n_pad):
    d = hx.shape[1]
    bsz, lp, d_attn = ya.shape
    d_conv = d - d_attn
    assert d_conv == d_attn
    tm = SEQ_TILE
    h_blk = pl.BlockSpec((tm, d), _row_tile_of(bsz, lp // tm))
    kern = functools.partial(_mix_out_kernel, tm=tm, n_groups=d_conv // HEAD_DIM, n_pad=n_pad)
    const = dict(pipeline_mode=pl.Buffered(1))
    return pl.pallas_call(
        kern,
        grid=(bsz, lp // tm),
        in_specs=[
            *_residual_specs(hx, hm, bsz, lp // tm, tm),
            h_blk,
            pl.BlockSpec((1, tm, d_attn), lambda b, i: (b, i, 0)),
            pl.BlockSpec((None, d, 3 * d_conv), lambda b, i: (layer, 0, 1), **const),
            pl.BlockSpec((HALO, d_conv), lambda b, i: (0, 0)),
            pl.BlockSpec((1, d_conv), lambda b, i: (0, 0)),
            pl.BlockSpec((None, d_attn + d_conv, d), lambda b, i: (layer, 0, 0), **const),
        ],
        out_specs=h_blk,
        out_shape=jax.ShapeDtypeStruct((bsz * lp, d), F32),
        scratch_shapes=[pltpu.VMEM((tm + HALO, d_conv), F32)],
        compiler_params=_cparams(2),
        name="mix_out",
    )(hx, hm, hn, ya, w_in, conv_w, og_conv, w_out)


def _ffn_kernel(h_ref, g_ref, wg_ref, wu_ref, wd_ref, gf_ref, o_ref, hn_ref, *, final):
    f = pl.program_id(1)

    @pl.when(f == 0)
    def _norm():
        x = h_ref[...]
        hn_ref[...] = (x * _rms_scale(x) * g_ref[...]).astype(BF16)
        o_ref[...] = x

    hn = hn_ref[...]
    gate = jnp.dot(hn, wg_ref[...], preferred_element_type=F32)
    up = jnp.dot(hn, wu_ref[...], preferred_element_type=F32)
    act = (jax.nn.silu(gate) * up).astype(BF16)
    o_ref[...] += jnp.dot(act, wd_ref[...], preferred_element_type=F32)

    if final:
        @pl.when(f == pl.num_programs(1) - 1)
        def _final_norm():
            y = o_ref[...]
            o_ref[...] = y * _rms_scale(y) * gf_ref[...]


def _ffn(h, gain, w_gate, w_up, w_down, layer, final_gain, *, rows, final):
    d = h.shape[1]
    d_ff = w_gate.shape[2]
    tm = 1024 if rows % 1024 == 0 else 512
    tf = 512 if d_ff % 512 == 0 else 256
    return pl.pallas_call(
        functools.partial(_ffn_kernel, final=final),
        grid=(rows // tm, d_ff // tf),
        in_specs=[
            pl.BlockSpec((tm, d), lambda r, f: (r, 0)),
            pl.BlockSpec((1, d), lambda r, f: (0, 0)),
            pl.BlockSpec((None, d, tf), lambda r, f: (layer, 0, f)),
            pl.BlockSpec((None, d, tf), lambda r, f: (layer, 0, f)),
            pl.BlockSpec((None, tf, d), lambda r, f: (layer, f, 0)),
            pl.BlockSpec((1, d), lambda r, f: (0, 0)),
        ],
        out_specs=pl.BlockSpec((tm, d), lambda r, f: (r, 0)),
        out_shape=jax.ShapeDtypeStruct((rows, d), F32),
        scratch_shapes=[pltpu.VMEM((tm, d), BF16)],
        compiler_params=_cparams(2),
        name="ffn",
    )(h, gain, w_gate, w_up, w_down, final_gain)


def kernel(x, meta, norm_mix, w_in, b_f, conv_w, out_gain, w_out, norm_ffn, w_gate, w_up, w_down, final_norm):
    bsz, seq, d = x.shape
    n_meta = meta.shape[0]
    depth = w_in.shape[0]
    d_attn = d // 2
    n_heads = d_attn // HEAD_DIM
    assert n_heads <= HEAD_PAD and seq % SEQ_TILE == 0 and n_meta <= SEQ_TILE
    assert w_in.shape[2] == 6 * d_attn + n_heads
    lp = seq + SEQ_TILE
    n_pad = lp - seq - n_meta
    scale = HEAD_DIM ** -0.5

    m = jnp.broadcast_to(meta.astype(x.dtype)[None], (bsz, n_meta, d))
    head = jnp.concatenate([jnp.zeros((bsz, n_pad, d), x.dtype), m], axis=1)
    hx, hm = x.reshape(bsz * seq, d), head.reshape(bsz * SEQ_TILE, d)
    n_t = lp // SEQ_TILE
    final_gain = final_norm.reshape(1, d)

    w_in_bf, w_out_bf = w_in.astype(BF16), w_out.astype(BF16)
    w_gate_bf, w_up_bf, w_down_bf = w_gate.astype(BF16), w_up.astype(BF16), w_down.astype(BF16)
    for l in range(depth):
        wf_t = jnp.zeros((HEAD_PAD, d), BF16).at[:n_heads].set(w_in[l, :, 6 * d_attn:].T.astype(BF16))
        bf_col = jnp.zeros((HEAD_PAD, 1), F32).at[:n_heads, 0].set(b_f[l])
        cw = jnp.zeros((HALO, d_attn), F32).at[:CONV_WIDTH].set(conv_w[l])
        og_heads = out_gain[l, :d_attn].reshape(n_heads, 1, HEAD_DIM)
        og_conv = out_gain[l, d_attn:].reshape(1, d_attn)

        gain = norm_mix[l].reshape(1, d)
        q, k, vt, stats, hn = _attn_proj(hx, hm, gain, w_in_bf, l, wf_t, bf_col, bsz=bsz, n_t=n_t,
                                         n_pad=n_pad, scale=scale)
        ya = _flash(q, k, vt, stats[:, :, STAT_C, :, 0].reshape(-1),
                    _first_key_tile(stats, n_heads), og_heads)

        h = _mix_out(hx, hm, hn, ya, w_in_bf, w_out_bf, l, cw, og_conv, n_pad=n_pad)
        final = l == depth - 1
        h = _ffn(h, norm_ffn[l].reshape(1, d), w_gate_bf, w_up_bf, w_down_bf, l, final_gain,
                 rows=bsz * seq if final else h.shape[0], final=final)
        hx = hm = h

    return h.reshape(bsz, seq, d)
```

```python
import functools
import math

import jax
import jax.numpy as jnp
from jax import lax
from jax.experimental import pallas as pl
from jax.experimental.pallas import tpu as pltpu

F32 = jnp.float32
BF16 = jnp.bfloat16

HEAD_DIM = 128
AUG_DIM = 2 * HEAD_DIM
N_SPLIT = 3
CONV_WIDTH = 3
EPS = 1e-6
NEG = -1e30
FAKE = 2 * NEG
LOG2E = math.log2(math.e)
SEQ_TILE = 512
HEAD_PAD = 16
ONES_ROWS = 16
LANES = 128
STAT_C, STAT_BMAX, STAT_QN, STAT_KN, N_STAT = 0, 1, 2, 3, 4
SKIP_LOG2 = 140.0
BOUND_REL_SLACK = 1.001
BOUND_ABS_SLACK = 1.0
HALO = 8
VMEM_LIMIT = 60 * 1024 * 1024

_NT = (((1,), (1,)), ((), ()))


def _cparams(n_axes):
    return pltpu.CompilerParams(dimension_semantics=("arbitrary",) * n_axes,
                                vmem_limit_bytes=VMEM_LIMIT)


def _rms_scale(x, axis=-1):
    return lax.rsqrt(jnp.mean(x * x, axis=axis, keepdims=True) + EPS)


def _cumsum_lanes(x):
    n = x.shape[1]
    lane = lax.broadcasted_iota(jnp.int32, x.shape, 1)
    shift = 1
    while shift < n:
        x = x + jnp.where(lane >= shift, pltpu.roll(x, shift, axis=1), 0.0)
        shift *= 2
    return x


def _round_bf16(x):
    return x.astype(BF16).astype(F32)


def _max_row_norm(x):
    xf = x.astype(F32)
    n2 = jnp.max(jnp.sum(xf * xf, axis=1, keepdims=True), axis=0, keepdims=True)
    return jnp.broadcast_to(jnp.sqrt(n2), (1, LANES))


def _attn_proj_kernel(hx_ref, hm_ref, g_ref, wqk_ref, wv_ref, wf_ref, bf_ref,
                      q_ref, k_ref, vt_ref, stat_ref, hn_ref, carry_ref, wvt_ref,
                      *, tm, n_heads, n_pad, scale):
    i = pl.program_id(1)
    d_attn = n_heads * HEAD_DIM

    @pl.when(i == 0)
    def _first_tile():
        carry_ref[...] = jnp.zeros_like(carry_ref)
        x = hm_ref[...]
        hn_ref[...] = (x * _rms_scale(x) * g_ref[...]).astype(BF16)

    @pl.when(i != 0)
    def _x_tile():
        x = hx_ref[...]
        hn_ref[...] = (x * _rms_scale(x) * g_ref[...]).astype(BF16)

    hn = hn_ref[...]

    @pl.when((pl.program_id(0) == 0) & (i == 0))
    def _transpose_v_weights():
        wvt_ref[...] = wv_ref[...].T

    logit = lax.dot_general(wf_ref[...], hn, _NT, preferred_element_type=F32)
    log_f = jax.nn.log_sigmoid(logit + bf_ref[...])
    r = _cumsum_lanes(log_f)
    stat_ref[0, 0] = jnp.zeros(stat_ref.shape[2:], F32)
    stat_ref[0, 0, STAT_C] = carry_ref[...] * LOG2E
    carry_ref[...] = carry_ref[...] + jnp.broadcast_to(r[:, tm - 1:tm], carry_ref.shape)
    pos = i * tm + lax.broadcasted_iota(jnp.int32, r.shape, 1)
    bias = jnp.where(pos >= n_pad, -LOG2E * r, NEG)
    stat_ref[0, 0, STAT_BMAX] = jnp.broadcast_to(jnp.max(bias, axis=1, keepdims=True),
                                                 (HEAD_PAD, LANES))
    bias_col = jnp.concatenate([bias, jnp.zeros((LANES - HEAD_PAD, tm), F32)], axis=0).T

    lane = lax.broadcasted_iota(jnp.int32, (tm, HEAD_DIM), 1)
    ones = jnp.where(lane < N_SPLIT, 1.0, 0.0).astype(BF16)
    zq = jnp.dot(hn, wqk_ref[:, :d_attn], preferred_element_type=F32)
    for h in range(n_heads):
        qh = (zq[:, h * HEAD_DIM:(h + 1) * HEAD_DIM] * (scale * LOG2E)).astype(BF16)
        q_ref[0, :, h * AUG_DIM:h * AUG_DIM + HEAD_DIM] = qh
        q_ref[0, :, h * AUG_DIM + HEAD_DIM:(h + 1) * AUG_DIM] = ones
        stat_ref[0, 0, STAT_QN, h:h + 1, :] = _max_row_norm(qh)

    zk = jnp.dot(hn, wqk_ref[:, d_attn:], preferred_element_type=F32)
    for h in range(n_heads):
        kh = zk[:, h * HEAD_DIM:(h + 1) * HEAD_DIM].astype(BF16)
        k_ref[0, :, h * AUG_DIM:h * AUG_DIM + HEAD_DIM] = kh
        stat_ref[0, 0, STAT_KN, h:h + 1, :] = _max_row_norm(kh)
        xb = jnp.broadcast_to(bias_col[:, h:h + 1], (tm, HEAD_DIM))
        hi = _round_bf16(xb)
        mid = _round_bf16(xb - hi)
        lo = xb - hi - mid
        aug = jnp.where(lane == 0, hi, jnp.where(lane == 1, mid, jnp.where(lane == 2, lo, 0.0)))
        k_ref[0, :, h * AUG_DIM + HEAD_DIM:(h + 1) * AUG_DIM] = aug.astype(BF16)

    zvt = lax.dot_general(wvt_ref[...], hn, _NT, preferred_element_type=F32)
    vt_ref[0] = zvt.astype(BF16)


def _row_tile_of(bsz, n_t):
    n_x = n_t - 1
    return lambda b, i: (jnp.where(i == 0, bsz * n_x + b, b * n_x + i - 1), 0)


def _residual_specs(hx, hm, bsz, n_t, tm):
    n_x = n_t - 1
    first_meta = hm.shape[0] // tm - bsz
    d = hx.shape[1]
    return (pl.BlockSpec((tm, d), lambda b, i: (b * n_x + jnp.maximum(i - 1, 0), 0)),
            pl.BlockSpec((tm, d), lambda b, i: (first_meta + b, 0), pipeline_mode=pl.Buffered(1)))


def _attn_proj(hx, hm, gain, w_in, layer, wf_t, bf_col, *, bsz, n_t, n_pad, scale):
    d = hx.shape[1]
    d_attn = d // 2
    n_heads = d_attn // HEAD_DIM
    tm = SEQ_TILE
    lp = n_t * tm
    kern = functools.partial(_attn_proj_kernel, tm=tm, n_heads=n_heads, n_pad=n_pad, scale=scale)
    aug_blk = pl.BlockSpec((1, tm, n_heads * AUG_DIM), lambda b, i: (b, i, 0))
    const = dict(pipeline_mode=pl.Buffered(1))
    return pl.pallas_call(
        kern,
        grid=(bsz, n_t),
        in_specs=[
            *_residual_specs(hx, hm, bsz, n_t, tm),
            pl.BlockSpec((1, d), lambda b, i: (0, 0)),
            pl.BlockSpec((None, d, 2 * d_attn), lambda b, i: (layer, 0, 0), **const),
            pl.BlockSpec((None, d, d_attn), lambda b, i: (layer, 0, 2), **const),
            pl.BlockSpec((HEAD_PAD, d), lambda b, i: (0, 0)),
            pl.BlockSpec((HEAD_PAD, 1), lambda b, i: (0, 0)),
        ],
        out_specs=[
            aug_blk, aug_blk,
            pl.BlockSpec((1, d_attn, tm), lambda b, i: (b, 0, i)),
            pl.BlockSpec((1, 1, N_STAT, HEAD_PAD, LANES), lambda b, i: (b, i, 0, 0, 0)),
            pl.BlockSpec((tm, d), _row_tile_of(bsz, n_t)),
        ],
        out_shape=[
            jax.ShapeDtypeStruct((bsz, lp, n_heads * AUG_DIM), BF16),
            jax.ShapeDtypeStruct((bsz, lp, n_heads * AUG_DIM), BF16),
            jax.ShapeDtypeStruct((bsz, d_attn, lp), BF16),
            jax.ShapeDtypeStruct((bsz, n_t, N_STAT, HEAD_PAD, LANES), F32),
            jax.ShapeDtypeStruct((bsz * lp, d), BF16),
        ],
        scratch_shapes=[pltpu.VMEM((HEAD_PAD, LANES), F32), pltpu.VMEM((d_attn, d), BF16)],
        compiler_params=_cparams(2),
        name="attn_proj",
    )(hx, hm, gain, w_in, w_in, wf_t, bf_col)


def _flash_kernel(c_ref, first_ref, q_ref, qn_ref, k_ref, vt_ref, g_ref, o_ref,
                  acc_ref, sa_ref, sb_ref, maxa_ref, *, tq, n_q, hpb):
    b = pl.program_id(0)
    hp = pl.program_id(1)
    i = pl.program_id(2)
    heads = range(hpb)
    c_base = [(b * n_q) * HEAD_PAD + hp * hpb + hh for hh in heads]
    c_q = [c_ref[c_base[hh] + i * HEAD_PAD] for hh in heads]
    first_base = (b * pl.num_programs(1) + hp) * n_q
    first = first_ref[first_base + i]
    n_unmasked = i - first
    last = first + jnp.maximum(n_unmasked - 1, 0)
    key = lax.broadcasted_iota(jnp.int32, (tq, tq), 0)
    qry = lax.broadcasted_iota(jnp.int32, (tq, tq), 1)
    ones_rows = jnp.ones((ONES_ROWS, tq), BF16)

    def scores_into(s_ref, j, masked=False, queries=q_ref):
        start = pl.multiple_of(j * tq, tq)
        col_max = []
        for hh in heads:
            cols = slice(hh * AUG_DIM, (hh + 1) * AUG_DIM)
            st = lax.dot_general(k_ref[0, pl.ds(start, tq), cols], queries[0, :, cols], _NT,
                                 preferred_element_type=F32)
            if masked:
                st = jnp.where(key <= qry, st, NEG)
            s_ref[hh] = st
            col_max.append(jnp.max(st, axis=0, keepdims=True))
        return tuple(col_max)

    def accumulate(s_ref, col_max, j, delta, m_all):
        start = pl.multiple_of(j * tq, tq)
        m_out = []
        for hh in heads:
            vt = vt_ref[0, hh * HEAD_DIM:(hh + 1) * HEAD_DIM, pl.ds(start, tq)]
            m_new = jnp.maximum(m_all[hh], col_max[hh] + delta[hh])
            alpha = jnp.exp2(m_all[hh] - m_new)
            p = jnp.exp2(s_ref[hh] - (m_new - delta[hh])).astype(BF16)
            pv = jnp.dot(jnp.concatenate([vt, ones_rows], axis=0), p,
                         preferred_element_type=F32)
            acc_ref[hh] = alpha * acc_ref[hh] + pv
            m_out.append(m_new)
        return tuple(m_out)

    def delta_of(j):
        return [c_q[hh] - c_ref[c_base[hh] + j * HEAD_PAD] for hh in heads]

    def tail_delta_of(j):
        real = delta_of(jnp.minimum(j, last))
        return [jnp.where(j < i, real[hh], FAKE) for hh in heads]

    acc_ref[...] = jnp.zeros_like(acc_ref)

    @pl.when(i == 0)
    def _first_query_tile():
        for hh, col_max in enumerate(scores_into(sa_ref, first)):
            maxa_ref[hh] = col_max

    max_a = tuple(maxa_ref[hh] for hh in heads)
    m_all = tuple(jnp.full((1, tq), NEG, F32) for _ in heads)
    n_trips = jnp.maximum((n_unmasked - 1) // 2, 0)

    def trip(jj, carry):
        m_all, max_a = carry
        j0 = first + 2 * jj
        max_b = scores_into(sb_ref, j0 + 1)
        m_all = accumulate(sa_ref, max_a, j0, delta_of(j0), m_all)
        max_a = scores_into(sa_ref, j0 + 2)
        m_all = accumulate(sb_ref, max_b, j0 + 1, delta_of(j0 + 1), m_all)
        return m_all, max_a

    def two_trips(jj, carry):
        return trip(2 * jj + 1, trip(2 * jj, carry))

    def four_trips(jj, carry):
        return two_trips(2 * jj + 1, two_trips(2 * jj, carry))

    carry = lax.fori_loop(0, n_trips // 4, four_trips, (m_all, max_a))
    carry = lax.fori_loop(2 * (n_trips // 4), n_trips // 2, two_trips, carry)
    m_all, max_a = lax.fori_loop(2 * (n_trips // 2), n_trips, trip, carry)
    j0 = first + 2 * n_trips
    no_delta = [0.0 for _ in heads]

    def finish():
        first_next = first_ref[first_base + jnp.minimum(i + 1, n_q - 1)]
        for hh, col_max in enumerate(scores_into(sa_ref, first_next, queries=qn_ref)):
            maxa_ref[hh] = col_max
        for hh in heads:
            ot = acc_ref[hh, :HEAD_DIM, :] / acc_ref[hh, HEAD_DIM:HEAD_DIM + 1, :]
            ot = ot * _rms_scale(ot, axis=0)
            o_ref[0, :, hh * HEAD_DIM:(hh + 1) * HEAD_DIM] = (ot.T * g_ref[hh]).astype(BF16)

    @pl.when(n_unmasked - 2 * n_trips == 2)
    def _tail_two_unmasked():
        max_b = scores_into(sb_ref, j0 + 1)
        m_1 = accumulate(sa_ref, max_a, j0, delta_of(j0), m_all)
        max_d = scores_into(sa_ref, i, masked=True)
        m_2 = accumulate(sb_ref, max_b, j0 + 1, delta_of(j0 + 1), m_1)
        accumulate(sa_ref, max_d, i, no_delta, m_2)
        finish()

    @pl.when(n_unmasked - 2 * n_trips < 2)
    def _tail_one_unmasked():
        max_d = scores_into(sb_ref, i, masked=True)
        m_1 = accumulate(sa_ref, max_a, jnp.minimum(j0, last), tail_delta_of(j0), m_all)
        accumulate(sb_ref, max_d, i, no_delta, m_1)
        finish()


def _heads_per_block(n_heads):
    return 2 if n_heads % 2 == 0 else 1


def _first_key_tile(stats, n_heads):
    hpb = _heads_per_block(n_heads)
    st = stats[:, :, :, :n_heads, 0]
    c, bmax, qn, kn = (jnp.moveaxis(st[:, :, s], 1, 2) for s in
                       (STAT_C, STAT_BMAX, STAT_QN, STAT_KN))
    upper = (qn[..., :, None] * (kn[..., None, :] + kn[..., :, None]) * BOUND_REL_SLACK
             + BOUND_ABS_SLACK
             + (c[..., :, None] - c[..., None, :]) + bmax[..., None, :])
    n_t = c.shape[-1]
    earlier = jnp.arange(n_t)[None, :] < jnp.arange(n_t)[:, None]
    negligible = (upper < -SKIP_LOG2) & earlier
    first = jnp.min(jnp.where(negligible, n_t, jnp.arange(n_t)), axis=-1)
    first = first.reshape(first.shape[0], n_heads // hpb, hpb, n_t).min(axis=2)
    return first.astype(jnp.int32).reshape(-1)


def _flash(q, k, vt, c_tiles, first_tile, og_heads):
    bsz, lp, _ = q.shape
    n_heads = vt.shape[1] // HEAD_DIM
    hpb = _heads_per_block(n_heads)
    tq = SEQ_TILE
    n_q = lp // tq
    kern = functools.partial(_flash_kernel, tq=tq, n_q=n_q, hpb=hpb)
    resident = pl.Buffered(2)
    return pl.pallas_call(
        kern,
        grid=(bsz, n_heads // hpb, n_q),
        in_specs=[
            pl.BlockSpec(memory_space=pltpu.SMEM),
            pl.BlockSpec(memory_space=pltpu.SMEM),
            pl.BlockSpec((1, tq, hpb * AUG_DIM), lambda b, h, i: (b, i, h)),
            pl.BlockSpec((1, tq, hpb * AUG_DIM), lambda b, h, i: (b, jnp.minimum(i + 1, n_q - 1), h)),
            pl.BlockSpec((1, lp, hpb * AUG_DIM), lambda b, h, i: (b, 0, h),
                         pipeline_mode=resident),
            pl.BlockSpec((1, hpb * HEAD_DIM, lp), lambda b, h, i: (b, h, 0),
                         pipeline_mode=resident),
            pl.BlockSpec((hpb, 1, HEAD_DIM), lambda b, h, i: (h, 0, 0)),
        ],
        out_specs=pl.BlockSpec((1, tq, hpb * HEAD_DIM), lambda b, h, i: (b, i, h)),
        out_shape=jax.ShapeDtypeStruct((bsz, lp, n_heads * HEAD_DIM), BF16),
        scratch_shapes=[
            pltpu.VMEM((hpb, HEAD_DIM + ONES_ROWS, tq), F32),
            pltpu.VMEM((hpb, tq, tq), F32),
            pltpu.VMEM((hpb, tq, tq), F32),
            pltpu.VMEM((hpb, 1, tq), F32),
        ],
        compiler_params=_cparams(3),
        name="flash",
    )(c_tiles, first_tile, q, q, k, vt, og_heads)


def _mix_out_kernel(hx_ref, hm_ref, hn_ref, ya_ref, wc_ref, cw_ref, og_ref, wo_ref, o_ref, u_ref,
                    *, tm, n_groups, n_pad):
    i = pl.program_id(1)
    d_conv = n_groups * HEAD_DIM
    hn = hn_ref[...]

    @pl.when(i == 0)
    def _reset():
        u_ref[pl.ds(0, HALO), :] = jnp.zeros((HALO, d_conv), F32)

    gate_b = jnp.dot(hn, wc_ref[:, :d_conv], preferred_element_type=F32)
    gate_c = jnp.dot(hn, wc_ref[:, d_conv:2 * d_conv], preferred_element_type=F32)
    hc = jnp.dot(hn, wc_ref[:, 2 * d_conv:], preferred_element_type=F32)
    u = gate_c * hc
    u_ref[pl.ds(HALO, tm), :] = u
    u1 = u_ref[pl.ds(HALO - 1, tm), :]
    u2 = u_ref[pl.ds(HALO - 2, tm), :]
    conv = cw_ref[0:1, :] * u2 + cw_ref[1:2, :] * u1 + cw_ref[2:3, :] * u
    y = gate_b * conv
    u_ref[pl.ds(0, HALO), :] = u_ref[pl.ds(tm, HALO), :]
    yc = jnp.concatenate(
        [(y[:, g * HEAD_DIM:(g + 1) * HEAD_DIM] * _rms_scale(y[:, g * HEAD_DIM:(g + 1) * HEAD_DIM])
          * og_ref[:, g * HEAD_DIM:(g + 1) * HEAD_DIM]).astype(BF16) for g in range(n_groups)],
        axis=1)
    d_attn = ya_ref.shape[2]
    out = (jnp.dot(ya_ref[0], wo_ref[:d_attn, :], preferred_element_type=F32)
           + jnp.dot(yc, wo_ref[d_attn:, :], preferred_element_type=F32))
    out = jnp.where(i == 0, hm_ref[...], hx_ref[...]) + out
    pos = i * tm + lax.broadcasted_iota(jnp.int32, (tm, 1), 0)
    o_ref[...] = jnp.where(pos >= n_pad, out, 0.0)


def _mix_out(hx, hm, hn, ya, w_in, w_out, layer, conv_w, og_conv, *, n_pad):
    d = hx.shape[1]
    bsz, lp, d_attn = ya.shape
    d_conv = d - d_attn
    assert d_conv == d_attn
    tm = SEQ_TILE
    h_blk = pl.BlockSpec((tm, d), _row_tile_of(bsz, lp // tm))
    kern = functools.partial(_mix_out_kernel, tm=tm, n_groups=d_conv // HEAD_DIM, n_pad=n_pad)
    const = dict(pipeline_mode=pl.Buffered(1))
    return pl.pallas_call(
        kern,
        grid=(bsz, lp // tm),
        in_specs=[
            *_residual_specs(hx, hm, bsz, lp // tm, tm),
            h_blk,
            pl.BlockSpec((1, tm, d_attn), lambda b, i: (b, i, 0)),
            pl.BlockSpec((None, d, 3 * d_conv), lambda b, i: (layer, 0, 1), **const),
            pl.BlockSpec((HALO, d_conv), lambda b, i: (0, 0)),
            pl.BlockSpec((1, d_conv), lambda b, i: (0, 0)),
            pl.BlockSpec((None, d_attn + d_conv, d), lambda b, i: (layer, 0, 0), **const),
        ],
        out_specs=h_blk,
        out_shape=jax.ShapeDtypeStruct((bsz * lp, d), F32),
        scratch_shapes=[pltpu.VMEM((tm + HALO, d_conv), F32)],
        compiler_params=_cparams(2),
        name="mix_out",
    )(hx, hm, hn, ya, w_in, conv_w, og_conv, w_out)


def _ffn_kernel(h_ref, g_ref, wg_hbm, wu_hbm, wd_hbm, gf_ref, o_ref,
                hn_ref, wg_buf, wu_buf, wd_buf, sem, *, layer, tf, n_f, final):
    r = pl.program_id(0)
    n_r = pl.num_programs(0)

    def chunk_copies(f, slot):
        cols = pl.ds(pl.multiple_of(f * tf, tf), tf)
        return (pltpu.make_async_copy(wg_hbm.at[layer, :, cols], wg_buf.at[slot], sem.at[0, slot]),
                pltpu.make_async_copy(wu_hbm.at[layer, :, cols], wu_buf.at[slot], sem.at[1, slot]),
                pltpu.make_async_copy(wd_hbm.at[layer, cols, :], wd_buf.at[slot], sem.at[2, slot]))

    @pl.when(r == 0)
    def _first_chunk():
        for copy in chunk_copies(0, 0):
            copy.start()

    x = h_ref[...]
    hn_ref[...] = (x * _rms_scale(x) * g_ref[...]).astype(BF16)
    o_ref[...] = x

    def chunk(f, carry):
        slot = (r * n_f + f) % 2
        for copy in chunk_copies(f, slot):
            copy.wait()

        @pl.when((f + 1 < n_f) | (r + 1 < n_r))
        def _prefetch():
            for copy in chunk_copies(jnp.where(f + 1 < n_f, f + 1, 0), 1 - slot):
                copy.start()

        hn = hn_ref[...]
        gate = jnp.dot(hn, wg_buf[slot], preferred_element_type=F32)
        up = jnp.dot(hn, wu_buf[slot], preferred_element_type=F32)
        act = (jax.nn.silu(gate) * up).astype(BF16)
        o_ref[...] += jnp.dot(act, wd_buf[slot], preferred_element_type=F32)
        return carry

    lax.fori_loop(0, n_f, chunk, 0)

    if final:
        y = o_ref[...]
        o_ref[...] = y * _rms_scale(y) * gf_ref[...]


def _ffn(h, gain, w_gate, w_up, w_down, layer, final_gain, *, rows, final):
    d = h.shape[1]
    d_ff = w_gate.shape[2]
    tm = 1024 if rows % 1024 == 0 else 512
    tf = 512 if d_ff % 512 == 0 else 256
    in_hbm = pl.BlockSpec(memory_space=pl.ANY)
    return pl.pallas_call(
        functools.partial(_ffn_kernel, layer=layer, tf=tf, n_f=d_ff // tf, final=final),
        grid=(rows // tm,),
        in_specs=[
            pl.BlockSpec((tm, d), lambda r: (r, 0)),
            pl.BlockSpec((1, d), lambda r: (0, 0)),
            in_hbm, in_hbm, in_hbm,
            pl.BlockSpec((1, d), lambda r: (0, 0)),
        ],
        out_specs=pl.BlockSpec((tm, d), lambda r: (r, 0)),
        out_shape=jax.ShapeDtypeStruct((rows, d), F32),
        scratch_shapes=[
            pltpu.VMEM((tm, d), BF16),
            pltpu.VMEM((2, d, tf), BF16),
            pltpu.VMEM((2, d, tf), BF16),
            pltpu.VMEM((2, tf, d), BF16),
            pltpu.SemaphoreType.DMA((3, 2)),
        ],
        compiler_params=_cparams(1),
        name="ffn",
    )(h, gain, w_gate, w_up, w_down, final_gain)


def kernel(x, meta, norm_mix, w_in, b_f, conv_w, out_gain, w_out, norm_ffn, w_gate, w_up, w_down, final_norm):
    bsz, seq, d = x.shape
    n_meta = meta.shape[0]
    depth = w_in.shape[0]
    d_attn = d // 2
    n_heads = d_attn // HEAD_DIM
    assert n_heads <= HEAD_PAD and seq % SEQ_TILE == 0 and n_meta <= SEQ_TILE
    assert w_in.shape[2] == 6 * d_attn + n_heads
    lp = seq + SEQ_TILE
    n_pad = lp - seq - n_meta
    scale = HEAD_DIM ** -0.5

    m = jnp.broadcast_to(meta.astype(x.dtype)[None], (bsz, n_meta, d))
    head = jnp.concatenate([jnp.zeros((bsz, n_pad, d), x.dtype), m], axis=1)
    hx, hm = x.reshape(bsz * seq, d), head.reshape(bsz * SEQ_TILE, d)
    n_t = lp // SEQ_TILE
    final_gain = final_norm.reshape(1, d)

    w_in_bf, w_out_bf = w_in.astype(BF16), w_out.astype(BF16)
    w_gate_bf, w_up_bf, w_down_bf = w_gate.astype(BF16), w_up.astype(BF16), w_down.astype(BF16)
    for l in range(depth):
        wf_t = jnp.zeros((HEAD_PAD, d), BF16).at[:n_heads].set(w_in[l, :, 6 * d_attn:].T.astype(BF16))
        bf_col = jnp.zeros((HEAD_PAD, 1), F32).at[:n_heads, 0].set(b_f[l])
        cw = jnp.zeros((HALO, d_attn), F32).at[:CONV_WIDTH].set(conv_w[l])
        og_heads = out_gain[l, :d_attn].reshape(n_heads, 1, HEAD_DIM)
        og_conv = out_gain[l, d_attn:].reshape(1, d_attn)

        gain = norm_mix[l].reshape(1, d)
        q, k, vt, stats, hn = _attn_proj(hx, hm, gain, w_in_bf, l, wf_t, bf_col, bsz=bsz, n_t=n_t,
                                         n_pad=n_pad, scale=scale)
        ya = _flash(q, k, vt, stats[:, :, STAT_C, :, 0].reshape(-1),
                    _first_key_tile(stats, n_heads), og_heads)

        h = _mix_out(hx, hm, hn, ya, w_in_bf, w_out_bf, l, cw, og_conv, n_pad=n_pad)
        final = l == depth - 1
        h = _ffn(h, norm_ffn[l].reshape(1, d), w_gate_bf, w_up_bf, w_down_bf, l, final_gain,
                 rows=bsz * seq if final else h.shape[0], final=final)
        hx = hm = h

    return h.reshape(bsz, seq, d)
```

```python
import functools
import math

import jax
import jax.numpy as jnp
from jax import lax
from jax.experimental import pallas as pl
from jax.experimental.pallas import tpu as pltpu

F32 = jnp.float32
BF16 = jnp.bfloat16

HEAD_DIM = 128
AUG_DIM = 2 * HEAD_DIM
N_SPLIT = 3
CONV_WIDTH = 3
EPS = 1e-6
NEG = -1e30
FAKE = 2 * NEG
LOG2E = math.log2(math.e)
SEQ_TILE = 512
HEAD_PAD = 16
ONES_ROWS = 16
LANES = 128
STAT_C, STAT_BMAX, STAT_QN, STAT_KN, N_STAT = 0, 1, 2, 3, 4
SKIP_LOG2 = 140.0
BOUND_REL_SLACK = 1.001
BOUND_ABS_SLACK = 1.0
HALO = 8
VMEM_LIMIT = 60 * 1024 * 1024

_NT = (((1,), (1,)), ((), ()))


def _cparams(n_axes):
    return pltpu.CompilerParams(dimension_semantics=("arbitrary",) * n_axes,
                                vmem_limit_bytes=VMEM_LIMIT)


def _rms_scale(x, axis=-1):
    return lax.rsqrt(jnp.mean(x * x, axis=axis, keepdims=True) + EPS)


def _cumsum_lanes(x):
    n = x.shape[1]
    lane = lax.broadcasted_iota(jnp.int32, x.shape, 1)
    shift = 1
    while shift < n:
        x = x + jnp.where(lane >= shift, pltpu.roll(x, shift, axis=1), 0.0)
        shift *= 2
    return x


def _round_bf16(x):
    return x.astype(BF16).astype(F32)


def _max_row_norm(x):
    xf = x.astype(F32)
    n2 = jnp.max(jnp.sum(xf * xf, axis=1, keepdims=True), axis=0, keepdims=True)
    return jnp.broadcast_to(jnp.sqrt(n2), (1, LANES))


def _attn_proj_kernel(hx_ref, hm_ref, g_ref, wqk_ref, wv_ref, wf_ref, bf_ref,
                      q_ref, k_ref, vt_ref, stat_ref, hn_ref, carry_ref, wvt_ref,
                      *, tm, n_heads, n_pad, scale):
    i = pl.program_id(1)
    d_attn = n_heads * HEAD_DIM

    @pl.when(i == 0)
    def _first_tile():
        carry_ref[...] = jnp.zeros_like(carry_ref)
        x = hm_ref[...]
        hn_ref[...] = (x * _rms_scale(x) * g_ref[...]).astype(BF16)

    @pl.when(i != 0)
    def _x_tile():
        x = hx_ref[...]
        hn_ref[...] = (x * _rms_scale(x) * g_ref[...]).astype(BF16)

    hn = hn_ref[...]

    @pl.when((pl.program_id(0) == 0) & (i == 0))
    def _transpose_v_weights():
        wvt_ref[pl.ds(0, d_attn), :] = wv_ref[...].T
        wvt_ref[pl.ds(d_attn, HEAD_PAD), :] = wf_ref[...]

    zvt = lax.dot_general(wvt_ref[...], hn, _NT, preferred_element_type=F32)
    vt_ref[0] = zvt[:d_attn].astype(BF16)

    logit = zvt[d_attn:]
    log_f = jax.nn.log_sigmoid(logit + bf_ref[...])
    r = _cumsum_lanes(log_f)
    stat_ref[0, 0] = jnp.zeros(stat_ref.shape[2:], F32)
    stat_ref[0, 0, STAT_C] = carry_ref[...] * LOG2E
    carry_ref[...] = carry_ref[...] + jnp.broadcast_to(r[:, tm - 1:tm], carry_ref.shape)
    pos = i * tm + lax.broadcasted_iota(jnp.int32, r.shape, 1)
    bias = jnp.where(pos >= n_pad, -LOG2E * r, NEG)
    stat_ref[0, 0, STAT_BMAX] = jnp.broadcast_to(jnp.max(bias, axis=1, keepdims=True),
                                                 (HEAD_PAD, LANES))
    bias_col = jnp.concatenate([bias, jnp.zeros((LANES - HEAD_PAD, tm), F32)], axis=0).T

    lane = lax.broadcasted_iota(jnp.int32, (tm, HEAD_DIM), 1)
    ones = jnp.where(lane < N_SPLIT, 1.0, 0.0).astype(BF16)
    zk = jnp.dot(hn, wqk_ref[:, d_attn:], preferred_element_type=F32)
    for h in range(n_heads):
        kh = zk[:, h * HEAD_DIM:(h + 1) * HEAD_DIM].astype(BF16)
        k_ref[0, :, h * AUG_DIM:h * AUG_DIM + HEAD_DIM] = kh
        stat_ref[0, 0, STAT_KN, h:h + 1, :] = _max_row_norm(kh)
        xb = jnp.broadcast_to(bias_col[:, h:h + 1], (tm, HEAD_DIM))
        hi = _round_bf16(xb)
        mid = _round_bf16(xb - hi)
        lo = xb - hi - mid
        aug = jnp.where(lane == 0, hi, jnp.where(lane == 1, mid, jnp.where(lane == 2, lo, 0.0)))
        k_ref[0, :, h * AUG_DIM + HEAD_DIM:(h + 1) * AUG_DIM] = aug.astype(BF16)

    zq = jnp.dot(hn, wqk_ref[:, :d_attn], preferred_element_type=F32)
    for h in range(n_heads):
        qh = (zq[:, h * HEAD_DIM:(h + 1) * HEAD_DIM] * (scale * LOG2E)).astype(BF16)
        q_ref[0, :, h * AUG_DIM:h * AUG_DIM + HEAD_DIM] = qh
        q_ref[0, :, h * AUG_DIM + HEAD_DIM:(h + 1) * AUG_DIM] = ones
        stat_ref[0, 0, STAT_QN, h:h + 1, :] = _max_row_norm(qh)


def _row_tile_of(bsz, n_t):
    n_x = n_t - 1
    return lambda b, i: (jnp.where(i == 0, bsz * n_x + b, b * n_x + i - 1), 0)


def _residual_specs(hx, hm, bsz, n_t, tm):
    n_x = n_t - 1
    first_meta = hm.shape[0] // tm - bsz
    d = hx.shape[1]
    return (pl.BlockSpec((tm, d), lambda b, i: (b * n_x + jnp.maximum(i - 1, 0), 0)),
            pl.BlockSpec((tm, d), lambda b, i: (first_meta + b, 0), pipeline_mode=pl.Buffered(1)))


def _attn_proj(hx, hm, gain, w_in, layer, wf_t, bf_col, *, bsz, n_t, n_pad, scale):
    d = hx.shape[1]
    d_attn = d // 2
    n_heads = d_attn // HEAD_DIM
    tm = SEQ_TILE
    lp = n_t * tm
    kern = functools.partial(_attn_proj_kernel, tm=tm, n_heads=n_heads, n_pad=n_pad, scale=scale)
    aug_blk = pl.BlockSpec((1, tm, n_heads * AUG_DIM), lambda b, i: (b, i, 0))
    const = dict(pipeline_mode=pl.Buffered(1))
    return pl.pallas_call(
        kern,
        grid=(bsz, n_t),
        in_specs=[
            *_residual_specs(hx, hm, bsz, n_t, tm),
            pl.BlockSpec((1, d), lambda b, i: (0, 0)),
            pl.BlockSpec((None, d, 2 * d_attn), lambda b, i: (layer, 0, 0), **const),
            pl.BlockSpec((None, d, d_attn), lambda b, i: (layer, 0, 2), **const),
            pl.BlockSpec((HEAD_PAD, d), lambda b, i: (0, 0)),
            pl.BlockSpec((HEAD_PAD, 1), lambda b, i: (0, 0)),
        ],
        out_specs=[
            aug_blk, aug_blk,
            pl.BlockSpec((1, d_attn, tm), lambda b, i: (b, 0, i)),
            pl.BlockSpec((1, 1, N_STAT, HEAD_PAD, LANES), lambda b, i: (b, i, 0, 0, 0)),
            pl.BlockSpec((tm, d), _row_tile_of(bsz, n_t)),
        ],
        out_shape=[
            jax.ShapeDtypeStruct((bsz, lp, n_heads * AUG_DIM), BF16),
            jax.ShapeDtypeStruct((bsz, lp, n_heads * AUG_DIM), BF16),
            jax.ShapeDtypeStruct((bsz, d_attn, lp), BF16),
            jax.ShapeDtypeStruct((bsz, n_t, N_STAT, HEAD_PAD, LANES), F32),
            jax.ShapeDtypeStruct((bsz * lp, d), BF16),
        ],
        scratch_shapes=[pltpu.VMEM((HEAD_PAD, LANES), F32),
                        pltpu.VMEM((d_attn + HEAD_PAD, d), BF16)],
        compiler_params=_cparams(2),
        name="attn_proj",
    )(hx, hm, gain, w_in, w_in, wf_t, bf_col)


def _flash_kernel(c_ref, first_ref, q_ref, qn_ref, k_ref, vt_ref, g_ref, o_ref,
                  acc_ref, sa_ref, sb_ref, maxa_ref, *, tq, n_q, hpb):
    b = pl.program_id(0)
    hp = pl.program_id(1)
    i = pl.program_id(2)
    heads = range(hpb)
    c_base = [(b * n_q) * HEAD_PAD + hp * hpb + hh for hh in heads]
    c_q = [c_ref[c_base[hh] + i * HEAD_PAD] for hh in heads]
    first_base = (b * pl.num_programs(1) + hp) * n_q
    first = first_ref[first_base + i]
    n_unmasked = i - first
    last = first + jnp.maximum(n_unmasked - 1, 0)
    key = lax.broadcasted_iota(jnp.int32, (tq, tq), 0)
    qry = lax.broadcasted_iota(jnp.int32, (tq, tq), 1)
    ones_rows = jnp.ones((ONES_ROWS, tq), BF16)

    def scores_into(s_ref, j, masked=False, queries=q_ref):
        start = pl.multiple_of(j * tq, tq)
        col_max = []
        for hh in heads:
            cols = slice(hh * AUG_DIM, (hh + 1) * AUG_DIM)
            st = lax.dot_general(k_ref[0, pl.ds(start, tq), cols], queries[0, :, cols], _NT,
                                 preferred_element_type=F32)
            if masked:
                st = jnp.where(key <= qry, st, NEG)
            s_ref[hh] = st
            col_max.append(jnp.max(st, axis=0, keepdims=True))
        return tuple(col_max)

    def accumulate(s_ref, col_max, j, delta, m_all):
        start = pl.multiple_of(j * tq, tq)
        m_out = []
        for hh in heads:
            vt = vt_ref[0, hh * HEAD_DIM:(hh + 1) * HEAD_DIM, pl.ds(start, tq)]
            m_new = jnp.maximum(m_all[hh], col_max[hh] + delta[hh])
            alpha = jnp.exp2(m_all[hh] - m_new)
            p = jnp.exp2(s_ref[hh] - (m_new - delta[hh])).astype(BF16)
            pv = jnp.dot(jnp.concatenate([vt, ones_rows], axis=0), p,
                         preferred_element_type=F32)
            acc_ref[hh] = alpha * acc_ref[hh] + pv
            m_out.append(m_new)
        return tuple(m_out)

    def delta_of(j):
        return [c_q[hh] - c_ref[c_base[hh] + j * HEAD_PAD] for hh in heads]

    def tail_delta_of(j):
        real = delta_of(jnp.minimum(j, last))
        return [jnp.where(j < i, real[hh], FAKE) for hh in heads]

    acc_ref[...] = jnp.zeros_like(acc_ref)

    @pl.when(i == 0)
    def _first_query_tile():
        for hh, col_max in enumerate(scores_into(sa_ref, first)):
            maxa_ref[hh] = col_max

    max_a = tuple(maxa_ref[hh] for hh in heads)
    m_all = tuple(jnp.full((1, tq), NEG, F32) for _ in heads)
    n_trips = jnp.maximum((n_unmasked - 1) // 2, 0)

    def trip(jj, carry):
        m_all, max_a = carry
        j0 = first + 2 * jj
        max_b = scores_into(sb_ref, j0 + 1)
        m_all = accumulate(sa_ref, max_a, j0, delta_of(j0), m_all)
        max_a = scores_into(sa_ref, j0 + 2)
        m_all = accumulate(sb_ref, max_b, j0 + 1, delta_of(j0 + 1), m_all)
        return m_all, max_a

    def two_trips(jj, carry):
        return trip(2 * jj + 1, trip(2 * jj, carry))

    def four_trips(jj, carry):
        return two_trips(2 * jj + 1, two_trips(2 * jj, carry))

    carry = lax.fori_loop(0, n_trips // 4, four_trips, (m_all, max_a))
    carry = lax.fori_loop(2 * (n_trips // 4), n_trips // 2, two_trips, carry)
    m_all, max_a = lax.fori_loop(2 * (n_trips // 2), n_trips, trip, carry)
    j0 = first + 2 * n_trips
    no_delta = [0.0 for _ in heads]

    def finish():
        first_next = first_ref[first_base + jnp.minimum(i + 1, n_q - 1)]
        for hh, col_max in enumerate(scores_into(sa_ref, first_next, queries=qn_ref)):
            maxa_ref[hh] = col_max
        for hh in heads:
            ot = acc_ref[hh, :HEAD_DIM, :] / acc_ref[hh, HEAD_DIM:HEAD_DIM + 1, :]
            ot = ot * _rms_scale(ot, axis=0)
            o_ref[0, :, hh * HEAD_DIM:(hh + 1) * HEAD_DIM] = (ot.T * g_ref[hh]).astype(BF16)

    @pl.when(n_unmasked - 2 * n_trips == 2)
    def _tail_two_unmasked():
        max_b = scores_into(sb_ref, j0 + 1)
        m_1 = accumulate(sa_ref, max_a, j0, delta_of(j0), m_all)
        max_d = scores_into(sa_ref, i, masked=True)
        m_2 = accumulate(sb_ref, max_b, j0 + 1, delta_of(j0 + 1), m_1)
        accumulate(sa_ref, max_d, i, no_delta, m_2)
        finish()

    @pl.when(n_unmasked - 2 * n_trips < 2)
    def _tail_one_unmasked():
        max_d = scores_into(sb_ref, i, masked=True)
        m_1 = accumulate(sa_ref, max_a, jnp.minimum(j0, last), tail_delta_of(j0), m_all)
        accumulate(sb_ref, max_d, i, no_delta, m_1)
        finish()


def _heads_per_block(n_heads):
    return 2 if n_heads % 2 == 0 else 1


def _first_key_tile(stats, n_heads):
    hpb = _heads_per_block(n_heads)
    st = stats[:, :, :, :n_heads, 0]
    c, bmax, qn, kn = (jnp.moveaxis(st[:, :, s], 1, 2) for s in
                       (STAT_C, STAT_BMAX, STAT_QN, STAT_KN))
    upper = (qn[..., :, None] * (kn[..., None, :] + kn[..., :, None]) * BOUND_REL_SLACK
             + BOUND_ABS_SLACK
             + (c[..., :, None] - c[..., None, :]) + bmax[..., None, :])
    n_t = c.shape[-1]
    earlier = jnp.arange(n_t)[None, :] < jnp.arange(n_t)[:, None]
    negligible = (upper < -SKIP_LOG2) & earlier
    first = jnp.min(jnp.where(negligible, n_t, jnp.arange(n_t)), axis=-1)
    first = first.reshape(first.shape[0], n_heads // hpb, hpb, n_t).min(axis=2)
    return first.astype(jnp.int32).reshape(-1)


def _flash(q, k, vt, c_tiles, first_tile, og_heads):
    bsz, lp, _ = q.shape
    n_heads = vt.shape[1] // HEAD_DIM
    hpb = _heads_per_block(n_heads)
    tq = SEQ_TILE
    n_q = lp // tq
    kern = functools.partial(_flash_kernel, tq=tq, n_q=n_q, hpb=hpb)
    resident = pl.Buffered(2)
    return pl.pallas_call(
        kern,
        grid=(bsz, n_heads // hpb, n_q),
        in_specs=[
            pl.BlockSpec(memory_space=pltpu.SMEM),
            pl.BlockSpec(memory_space=pltpu.SMEM),
            pl.BlockSpec((1, tq, hpb * AUG_DIM), lambda b, h, i: (b, i, h)),
            pl.BlockSpec((1, tq, hpb * AUG_DIM), lambda b, h, i: (b, jnp.minimum(i + 1, n_q - 1), h)),
            pl.BlockSpec((1, lp, hpb * AUG_DIM), lambda b, h, i: (b, 0, h),
                         pipeline_mode=resident),
            pl.BlockSpec((1, hpb * HEAD_DIM, lp), lambda b, h, i: (b, h, 0),
                         pipeline_mode=resident),
            pl.BlockSpec((hpb, 1, HEAD_DIM), lambda b, h, i: (h, 0, 0)),
        ],
        out_specs=pl.BlockSpec((1, tq, hpb * HEAD_DIM), lambda b, h, i: (b, i, h)),
        out_shape=jax.ShapeDtypeStruct((bsz, lp, n_heads * HEAD_DIM), BF16),
        scratch_shapes=[
            pltpu.VMEM((hpb, HEAD_DIM + ONES_ROWS, tq), F32),
            pltpu.VMEM((hpb, tq, tq), F32),
            pltpu.VMEM((hpb, tq, tq), F32),
            pltpu.VMEM((hpb, 1, tq), F32),
        ],
        compiler_params=_cparams(3),
        name="flash",
    )(c_tiles, first_tile, q, q, k, vt, og_heads)


def _mix_out_kernel(hx_ref, hm_ref, hn_ref, ya_ref, wc_ref, cw_ref, og_ref, wo_ref, o_ref, u_ref,
                    *, tm, n_groups, n_pad):
    i = pl.program_id(1)
    d_conv = n_groups * HEAD_DIM
    hn = hn_ref[...]

    @pl.when(i == 0)
    def _reset():
        u_ref[pl.ds(0, HALO), :] = jnp.zeros((HALO, d_conv), F32)

    gate_b = jnp.dot(hn, wc_ref[:, :d_conv], preferred_element_type=F32)
    gate_c = jnp.dot(hn, wc_ref[:, d_conv:2 * d_conv], preferred_element_type=F32)
    hc = jnp.dot(hn, wc_ref[:, 2 * d_conv:], preferred_element_type=F32)
    u = gate_c * hc
    u_ref[pl.ds(HALO, tm), :] = u
    u1 = u_ref[pl.ds(HALO - 1, tm), :]
    u2 = u_ref[pl.ds(HALO - 2, tm), :]
    conv = cw_ref[0:1, :] * u2 + cw_ref[1:2, :] * u1 + cw_ref[2:3, :] * u
    y = gate_b * conv
    u_ref[pl.ds(0, HALO), :] = u_ref[pl.ds(tm, HALO), :]
    yc = jnp.concatenate(
        [(y[:, g * HEAD_DIM:(g + 1) * HEAD_DIM] * _rms_scale(y[:, g * HEAD_DIM:(g + 1) * HEAD_DIM])
          * og_ref[:, g * HEAD_DIM:(g + 1) * HEAD_DIM]).astype(BF16) for g in range(n_groups)],
        axis=1)
    d_attn = ya_ref.shape[2]
    out = (jnp.dot(ya_ref[0], wo_ref[:d_attn, :], preferred_element_type=F32)
           + jnp.dot(yc, wo_ref[d_attn:, :], preferred_element_type=F32))
    out = jnp.where(i == 0, hm_ref[...], hx_ref[...]) + out
    pos = i * tm + lax.broadcasted_iota(jnp.int32, (tm, 1), 0)
    o_ref[...] = jnp.where(pos >= n_pad, out, 0.0)


def _mix_out(hx, hm, hn, ya, w_in, w_out, layer, conv_w, og_conv, *, n_pad):
    d = hx.shape[1]
    bsz, lp, d_attn = ya.shape
    d_conv = d - d_attn
    assert d_conv == d_attn
    tm = SEQ_TILE
    h_blk = pl.BlockSpec((tm, d), _row_tile_of(bsz, lp // tm))
    kern = functools.partial(_mix_out_kernel, tm=tm, n_groups=d_conv // HEAD_DIM, n_pad=n_pad)
    const = dict(pipeline_mode=pl.Buffered(1))
    return pl.pallas_call(
        kern,
        grid=(bsz, lp // tm),
        in_specs=[
            *_residual_specs(hx, hm, bsz, lp // tm, tm),
            h_blk,
            pl.BlockSpec((1, tm, d_attn), lambda b, i: (b, i, 0)),
            pl.BlockSpec((None, d, 3 * d_conv), lambda b, i: (layer, 0, 1), **const),
            pl.BlockSpec((HALO, d_conv), lambda b, i: (0, 0)),
            pl.BlockSpec((1, d_conv), lambda b, i: (0, 0)),
            pl.BlockSpec((None, d_attn + d_conv, d), lambda b, i: (layer, 0, 0), **const),
        ],
        out_specs=h_blk,
        out_shape=jax.ShapeDtypeStruct((bsz * lp, d), F32),
        scratch_shapes=[pltpu.VMEM((tm + HALO, d_conv), F32)],
        compiler_params=_cparams(2),
        name="mix_out",
    )(hx, hm, hn, ya, w_in, conv_w, og_conv, w_out)


def _ffn_kernel(h_ref, g_ref, wg_hbm, wu_hbm, wd_hbm, gf_ref, o_ref,
                hn_ref, wg_buf, wu_buf, wd_buf, sem, *, layer, tf, n_f, final):
    r = pl.program_id(0)
    n_r = pl.num_programs(0)

    def chunk_copies(f, slot):
        cols = pl.ds(pl.multiple_of(f * tf, tf), tf)
        return (pltpu.make_async_copy(wg_hbm.at[layer, :, cols], wg_buf.at[slot], sem.at[0, slot]),
                pltpu.make_async_copy(wu_hbm.at[layer, :, cols], wu_buf.at[slot], sem.at[1, slot]),
                pltpu.make_async_copy(wd_hbm.at[layer, cols, :], wd_buf.at[slot], sem.at[2, slot]))

    @pl.when(r == 0)
    def _first_chunk():
        for copy in chunk_copies(0, 0):
            copy.start()

    x = h_ref[...]
    hn_ref[...] = (x * _rms_scale(x) * g_ref[...]).astype(BF16)
    o_ref[...] = x

    def chunk(f, carry):
        slot = (r * n_f + f) % 2
        for copy in chunk_copies(f, slot):
            copy.wait()

        @pl.when((f + 1 < n_f) | (r + 1 < n_r))
        def _prefetch():
            for copy in chunk_copies(jnp.where(f + 1 < n_f, f + 1, 0), 1 - slot):
                copy.start()

        hn = hn_ref[...]
        gate = jnp.dot(hn, wg_buf[slot], preferred_element_type=F32)
        up = jnp.dot(hn, wu_buf[slot], preferred_element_type=F32)
        act = (jax.nn.silu(gate) * up).astype(BF16)
        o_ref[...] += jnp.dot(act, wd_buf[slot], preferred_element_type=F32)
        return carry

    lax.fori_loop(0, n_f, chunk, 0)

    if final:
        y = o_ref[...]
        o_ref[...] = y * _rms_scale(y) * gf_ref[...]


def _ffn(h, gain, w_gate, w_up, w_down, layer, final_gain, *, rows, final):
    d = h.shape[1]
    d_ff = w_gate.shape[2]
    tm = 1024 if rows % 1024 == 0 else 512
    tf = 512 if d_ff % 512 == 0 else 256
    in_hbm = pl.BlockSpec(memory_space=pl.ANY)
    return pl.pallas_call(
        functools.partial(_ffn_kernel, layer=layer, tf=tf, n_f=d_ff // tf, final=final),
        grid=(rows // tm,),
        in_specs=[
            pl.BlockSpec((tm, d), lambda r: (r, 0)),
            pl.BlockSpec((1, d), lambda r: (0, 0)),
            in_hbm, in_hbm, in_hbm,
            pl.BlockSpec((1, d), lambda r: (0, 0)),
        ],
        out_specs=pl.BlockSpec((tm, d), lambda r: (r, 0)),
        out_shape=jax.ShapeDtypeStruct((rows, d), F32),
        scratch_shapes=[
            pltpu.VMEM((tm, d), BF16),
            pltpu.VMEM((2, d, tf), BF16),
            pltpu.VMEM((2, d, tf), BF16),
            pltpu.VMEM((2, tf, d), BF16),
            pltpu.SemaphoreType.DMA((3, 2)),
        ],
        compiler_params=_cparams(1),
        name="ffn",
    )(h, gain, w_gate, w_up, w_down, final_gain)


def kernel(x, meta, norm_mix, w_in, b_f, conv_w, out_gain, w_out, norm_ffn, w_gate, w_up, w_down, final_norm):
    bsz, seq, d = x.shape
    n_meta = meta.shape[0]
    depth = w_in.shape[0]
    d_attn = d // 2
    n_heads = d_attn // HEAD_DIM
    assert n_heads <= HEAD_PAD and seq % SEQ_TILE == 0 and n_meta <= SEQ_TILE
    assert w_in.shape[2] == 6 * d_attn + n_heads
    lp = seq + SEQ_TILE
    n_pad = lp - seq - n_meta
    scale = HEAD_DIM ** -0.5

    m = jnp.broadcast_to(meta.astype(x.dtype)[None], (bsz, n_meta, d))
    head = jnp.concatenate([jnp.zeros((bsz, n_pad, d), x.dtype), m], axis=1)
    hx, hm = x.reshape(bsz * seq, d), head.reshape(bsz * SEQ_TILE, d)
    n_t = lp // SEQ_TILE
    final_gain = final_norm.reshape(1, d)

    w_in_bf, w_out_bf = w_in.astype(BF16), w_out.astype(BF16)
    w_gate_bf, w_up_bf, w_down_bf = w_gate.astype(BF16), w_up.astype(BF16), w_down.astype(BF16)
    for l in range(depth):
        wf_t = jnp.zeros((HEAD_PAD, d), BF16).at[:n_heads].set(w_in[l, :, 6 * d_attn:].T.astype(BF16))
        bf_col = jnp.zeros((HEAD_PAD, 1), F32).at[:n_heads, 0].set(b_f[l])
        cw = jnp.zeros((HALO, d_attn), F32).at[:CONV_WIDTH].set(conv_w[l])
        og_heads = out_gain[l, :d_attn].reshape(n_heads, 1, HEAD_DIM)
        og_conv = out_gain[l, d_attn:].reshape(1, d_attn)

        gain = norm_mix[l].reshape(1, d)
        q, k, vt, stats, hn = _attn_proj(hx, hm, gain, w_in_bf, l, wf_t, bf_col, bsz=bsz, n_t=n_t,
                                         n_pad=n_pad, scale=scale)
        ya = _flash(q, k, vt, stats[:, :, STAT_C, :, 0].reshape(-1),
                    _first_key_tile(stats, n_heads), og_heads)

        h = _mix_out(hx, hm, hn, ya, w_in_bf, w_out_bf, l, cw, og_conv, n_pad=n_pad)
        final = l == depth - 1
        h = _ffn(h, norm_ffn[l].reshape(1, d), w_gate_bf, w_up_bf, w_down_bf, l, final_gain,
                 rows=bsz * seq if final else h.shape[0], final=final)
        hx = hm = h

    return h.reshape(bsz, seq, d)
```

```python
import functools
import math

import jax
import jax.numpy as jnp
from jax import lax
from jax.experimental import pallas as pl
from jax.experimental.pallas import tpu as pltpu

F32 = jnp.float32
BF16 = jnp.bfloat16

HEAD_DIM = 128
AUG_DIM = 2 * HEAD_DIM
N_SPLIT = 3
CONV_WIDTH = 3
EPS = 1e-6
NEG = -1e30
FAKE = 2 * NEG
LOG2E = math.log2(math.e)
SEQ_TILE = 512
HEAD_PAD = 16
ONES_ROWS = 16
LANES = 128
STAT_C, STAT_BMAX, STAT_QN, STAT_KN, N_STAT = 0, 1, 2, 3, 4
SKIP_LOG2 = 140.0
BOUND_REL_SLACK = 1.001
BOUND_ABS_SLACK = 1.0
HALO = 8
VMEM_LIMIT = 60 * 1024 * 1024

_NT = (((1,), (1,)), ((), ()))


def _cparams(n_axes):
    return pltpu.CompilerParams(dimension_semantics=("arbitrary",) * n_axes,
                                vmem_limit_bytes=VMEM_LIMIT)


def _rms_scale(x, axis=-1):
    return lax.rsqrt(jnp.mean(x * x, axis=axis, keepdims=True) + EPS)


def _cumsum_lanes(x):
    n = x.shape[1]
    lane = lax.broadcasted_iota(jnp.int32, x.shape, 1)
    shift = 1
    while shift < n:
        x = x + jnp.where(lane >= shift, pltpu.roll(x, shift, axis=1), 0.0)
        shift *= 2
    return x


def _round_bf16(x):
    return x.astype(BF16).astype(F32)


def _max_row_norm(x):
    xf = x.astype(F32)
    n2 = jnp.max(jnp.sum(xf * xf, axis=1, keepdims=True), axis=0, keepdims=True)
    return jnp.broadcast_to(jnp.sqrt(n2), (1, LANES))


def _attn_proj_kernel(hx_ref, hm_ref, g_ref, wqk_ref, wv_ref, wf_ref, bf_ref,
                      q_ref, k_ref, vt_ref, stat_ref, hn_ref, carry_ref, wvt_ref,
                      *, tm, n_heads, n_pad, scale):
    i = pl.program_id(1)
    d_attn = n_heads * HEAD_DIM

    @pl.when(i == 0)
    def _first_tile():
        carry_ref[...] = jnp.zeros_like(carry_ref)
        x = hm_ref[...]
        hn_ref[...] = (x * _rms_scale(x) * g_ref[...]).astype(BF16)

    @pl.when(i != 0)
    def _x_tile():
        x = hx_ref[...]
        hn_ref[...] = (x * _rms_scale(x) * g_ref[...]).astype(BF16)

    hn = hn_ref[...]

    @pl.when((pl.program_id(0) == 0) & (i == 0))
    def _transpose_v_weights():
        wvt_ref[pl.ds(0, d_attn), :] = wv_ref[...].T
        wvt_ref[pl.ds(d_attn, HEAD_PAD), :] = wf_ref[...]

    zvt = lax.dot_general(wvt_ref[...], hn, _NT, preferred_element_type=F32)
    vt_ref[0] = zvt[:d_attn].astype(BF16)

    logit = zvt[d_attn:]
    log_f = jax.nn.log_sigmoid(logit + bf_ref[...])
    r = _cumsum_lanes(log_f)
    stat_ref[0, 0] = jnp.zeros(stat_ref.shape[2:], F32)
    stat_ref[0, 0, STAT_C] = carry_ref[...] * LOG2E
    carry_ref[...] = carry_ref[...] + jnp.broadcast_to(r[:, tm - 1:tm], carry_ref.shape)
    pos = i * tm + lax.broadcasted_iota(jnp.int32, r.shape, 1)
    bias = jnp.where(pos >= n_pad, -LOG2E * r, NEG)
    stat_ref[0, 0, STAT_BMAX] = jnp.broadcast_to(jnp.max(bias, axis=1, keepdims=True),
                                                 (HEAD_PAD, LANES))
    bias_col = jnp.concatenate([bias, jnp.zeros((LANES - HEAD_PAD, tm), F32)], axis=0).T

    lane = lax.broadcasted_iota(jnp.int32, (tm, HEAD_DIM), 1)
    ones = jnp.where(lane < N_SPLIT, 1.0, 0.0).astype(BF16)
    zk = jnp.dot(hn, wqk_ref[:, d_attn:], preferred_element_type=F32)
    for h in range(n_heads):
        kh = zk[:, h * HEAD_DIM:(h + 1) * HEAD_DIM].astype(BF16)
        k_ref[0, :, h * AUG_DIM:h * AUG_DIM + HEAD_DIM] = kh
        stat_ref[0, 0, STAT_KN, h:h + 1, :] = _max_row_norm(kh)
        xb = jnp.broadcast_to(bias_col[:, h:h + 1], (tm, HEAD_DIM))
        hi = _round_bf16(xb)
        mid = _round_bf16(xb - hi)
        lo = xb - hi - mid
        aug = jnp.where(lane == 0, hi, jnp.where(lane == 1, mid, jnp.where(lane == 2, lo, 0.0)))
        k_ref[0, :, h * AUG_DIM + HEAD_DIM:(h + 1) * AUG_DIM] = aug.astype(BF16)

    zq = jnp.dot(hn, wqk_ref[:, :d_attn], preferred_element_type=F32)
    for h in range(n_heads):
        qh = (zq[:, h * HEAD_DIM:(h + 1) * HEAD_DIM] * (scale * LOG2E)).astype(BF16)
        q_ref[0, :, h * AUG_DIM:h * AUG_DIM + HEAD_DIM] = qh
        q_ref[0, :, h * AUG_DIM + HEAD_DIM:(h + 1) * AUG_DIM] = ones
        stat_ref[0, 0, STAT_QN, h:h + 1, :] = _max_row_norm(qh)


def _row_tile_of(bsz, n_t):
    n_x = n_t - 1
    return lambda b, i: (jnp.where(i == 0, bsz * n_x + b, b * n_x + i - 1), 0)


def _residual_specs(hx, hm, bsz, n_t, tm):
    n_x = n_t - 1
    first_meta = hm.shape[0] // tm - bsz
    d = hx.shape[1]
    return (pl.BlockSpec((tm, d), lambda b, i: (b * n_x + jnp.maximum(i - 1, 0), 0)),
            pl.BlockSpec((tm, d), lambda b, i: (first_meta + b, 0), pipeline_mode=pl.Buffered(1)))


def _attn_proj(hx, hm, gain, w_in, layer, wf_t, bf_col, *, bsz, n_t, n_pad, scale):
    d = hx.shape[1]
    d_attn = d // 2
    n_heads = d_attn // HEAD_DIM
    tm = SEQ_TILE
    lp = n_t * tm
    kern = functools.partial(_attn_proj_kernel, tm=tm, n_heads=n_heads, n_pad=n_pad, scale=scale)
    aug_blk = pl.BlockSpec((1, tm, n_heads * AUG_DIM), lambda b, i: (b, i, 0))
    const = dict(pipeline_mode=pl.Buffered(1))
    return pl.pallas_call(
        kern,
        grid=(bsz, n_t),
        in_specs=[
            *_residual_specs(hx, hm, bsz, n_t, tm),
            pl.BlockSpec((1, d), lambda b, i: (0, 0)),
            pl.BlockSpec((None, d, 2 * d_attn), lambda b, i: (layer, 0, 0), **const),
            pl.BlockSpec((None, d, d_attn), lambda b, i: (layer, 0, 2), **const),
            pl.BlockSpec((HEAD_PAD, d), lambda b, i: (0, 0)),
            pl.BlockSpec((HEAD_PAD, 1), lambda b, i: (0, 0)),
        ],
        out_specs=[
            aug_blk, aug_blk,
            pl.BlockSpec((1, d_attn, tm), lambda b, i: (b, 0, i)),
            pl.BlockSpec((1, 1, N_STAT, HEAD_PAD, LANES), lambda b, i: (b, i, 0, 0, 0)),
            pl.BlockSpec((tm, d), _row_tile_of(bsz, n_t)),
        ],
        out_shape=[
            jax.ShapeDtypeStruct((bsz, lp, n_heads * AUG_DIM), BF16),
            jax.ShapeDtypeStruct((bsz, lp, n_heads * AUG_DIM), BF16),
            jax.ShapeDtypeStruct((bsz, d_attn, lp), BF16),
            jax.ShapeDtypeStruct((bsz, n_t, N_STAT, HEAD_PAD, LANES), F32),
            jax.ShapeDtypeStruct((bsz * lp, d), BF16),
        ],
        scratch_shapes=[pltpu.VMEM((HEAD_PAD, LANES), F32),
                        pltpu.VMEM((d_attn + HEAD_PAD, d), BF16)],
        compiler_params=_cparams(2),
        name="attn_proj",
    )(hx, hm, gain, w_in, w_in, wf_t, bf_col)


def _flash_kernel(c_ref, first_ref, q_ref, qn_ref, k_ref, vt_ref, g_ref, o_ref,
                  acc_ref, sa_ref, sb_ref, maxa_ref, *, tq, n_q, hpb):
    b = pl.program_id(0)
    hp = pl.program_id(1)
    i = pl.program_id(2)
    heads = range(hpb)
    c_base = [(b * n_q) * HEAD_PAD + hp * hpb + hh for hh in heads]
    c_q = [c_ref[c_base[hh] + i * HEAD_PAD] for hh in heads]
    first_base = (b * pl.num_programs(1) + hp) * n_q
    first = first_ref[first_base + i]
    n_unmasked = i - first
    last = first + jnp.maximum(n_unmasked - 1, 0)
    key = lax.broadcasted_iota(jnp.int32, (tq, tq), 0)
    qry = lax.broadcasted_iota(jnp.int32, (tq, tq), 1)
    ones_rows = jnp.ones((ONES_ROWS, tq), BF16)

    def scores_into(s_ref, j, masked=False, queries=q_ref):
        start = pl.multiple_of(j * tq, tq)
        col_max = []
        for hh in heads:
            cols = slice(hh * AUG_DIM, (hh + 1) * AUG_DIM)
            st = lax.dot_general(k_ref[0, pl.ds(start, tq), cols], queries[0, :, cols], _NT,
                                 preferred_element_type=F32)
            if masked:
                st = jnp.where(key <= qry, st, NEG)
            s_ref[hh] = st
            col_max.append(jnp.max(st, axis=0, keepdims=True))
        return tuple(col_max)

    def accumulate(s_ref, col_max, j, delta, m_all):
        start = pl.multiple_of(j * tq, tq)
        m_out = []
        for hh in heads:
            vt = vt_ref[0, hh * HEAD_DIM:(hh + 1) * HEAD_DIM, pl.ds(start, tq)]
            m_new = jnp.maximum(m_all[hh], col_max[hh] + delta[hh])
            alpha = jnp.exp2(m_all[hh] - m_new)
            p = jnp.exp2(s_ref[hh] - (m_new - delta[hh])).astype(BF16)
            pv = jnp.dot(jnp.concatenate([vt, ones_rows], axis=0), p,
                         preferred_element_type=F32)
            acc_ref[hh] = alpha * acc_ref[hh] + pv
            m_out.append(m_new)
        return tuple(m_out)

    def delta_of(j):
        return [c_q[hh] - c_ref[c_base[hh] + j * HEAD_PAD] for hh in heads]

    def tail_delta_of(j):
        real = delta_of(jnp.minimum(j, last))
        return [jnp.where(j < i, real[hh], FAKE) for hh in heads]

    acc_ref[...] = jnp.zeros_like(acc_ref)

    @pl.when(i == 0)
    def _first_query_tile():
        for hh, col_max in enumerate(scores_into(sa_ref, first)):
            maxa_ref[hh] = col_max

    max_a = tuple(maxa_ref[hh] for hh in heads)
    m_all = tuple(jnp.full((1, tq), NEG, F32) for _ in heads)
    n_trips = jnp.maximum((n_unmasked - 1) // 2, 0)

    def trip(jj, carry):
        m_all, max_a = carry
        j0 = first + 2 * jj
        max_b = scores_into(sb_ref, j0 + 1)
        m_all = accumulate(sa_ref, max_a, j0, delta_of(j0), m_all)
        max_a = scores_into(sa_ref, j0 + 2)
        m_all = accumulate(sb_ref, max_b, j0 + 1, delta_of(j0 + 1), m_all)
        return m_all, max_a

    def two_trips(jj, carry):
        return trip(2 * jj + 1, trip(2 * jj, carry))

    def four_trips(jj, carry):
        return two_trips(2 * jj + 1, two_trips(2 * jj, carry))

    carry = lax.fori_loop(0, n_trips // 4, four_trips, (m_all, max_a))
    carry = lax.fori_loop(2 * (n_trips // 4), n_trips // 2, two_trips, carry)
    m_all, max_a = lax.fori_loop(2 * (n_trips // 2), n_trips, trip, carry)
    j0 = first + 2 * n_trips
    no_delta = [0.0 for _ in heads]

    def finish():
        first_next = first_ref[first_base + jnp.minimum(i + 1, n_q - 1)]
        for hh, col_max in enumerate(scores_into(sa_ref, first_next, queries=qn_ref)):
            maxa_ref[hh] = col_max
        for hh in heads:
            ot = acc_ref[hh, :HEAD_DIM, :] / acc_ref[hh, HEAD_DIM:HEAD_DIM + 1, :]
            ot = ot * _rms_scale(ot, axis=0)
            o_ref[0, :, hh * HEAD_DIM:(hh + 1) * HEAD_DIM] = (ot.T * g_ref[hh]).astype(BF16)

    @pl.when(n_unmasked - 2 * n_trips == 2)
    def _tail_two_unmasked():
        max_b = scores_into(sb_ref, j0 + 1)
        m_1 = accumulate(sa_ref, max_a, j0, delta_of(j0), m_all)
        max_d = scores_into(sa_ref, i, masked=True)
        m_2 = accumulate(sb_ref, max_b, j0 + 1, delta_of(j0 + 1), m_1)
        accumulate(sa_ref, max_d, i, no_delta, m_2)
        finish()

    @pl.when(n_unmasked - 2 * n_trips < 2)
    def _tail_one_unmasked():
        max_d = scores_into(sb_ref, i, masked=True)
        m_1 = accumulate(sa_ref, max_a, jnp.minimum(j0, last), tail_delta_of(j0), m_all)
        accumulate(sb_ref, max_d, i, no_delta, m_1)
        finish()


def _heads_per_block(n_heads):
    return 2 if n_heads % 2 == 0 else 1


def _first_key_tile(stats, n_heads):
    hpb = _heads_per_block(n_heads)
    st = stats[:, :, :, :n_heads, 0]
    c, bmax, qn, kn = (jnp.moveaxis(st[:, :, s], 1, 2) for s in
                       (STAT_C, STAT_BMAX, STAT_QN, STAT_KN))
    upper = (qn[..., :, None] * (kn[..., None, :] + kn[..., :, None]) * BOUND_REL_SLACK
             + BOUND_ABS_SLACK
             + (c[..., :, None] - c[..., None, :]) + bmax[..., None, :])
    n_t = c.shape[-1]
    earlier = jnp.arange(n_t)[None, :] < jnp.arange(n_t)[:, None]
    negligible = (upper < -SKIP_LOG2) & earlier
    first = jnp.min(jnp.where(negligible, n_t, jnp.arange(n_t)), axis=-1)
    first = first.reshape(first.shape[0], n_heads // hpb, hpb, n_t).min(axis=2)
    return first.astype(jnp.int32).reshape(-1)


def _flash(q, k, vt, c_tiles, first_tile, og_heads):
    bsz, lp, _ = q.shape
    n_heads = vt.shape[1] // HEAD_DIM
    hpb = _heads_per_block(n_heads)
    tq = SEQ_TILE
    n_q = lp // tq
    kern = functools.partial(_flash_kernel, tq=tq, n_q=n_q, hpb=hpb)
    resident = pl.Buffered(2)
    return pl.pallas_call(
        kern,
        grid=(bsz, n_heads // hpb, n_q),
        in_specs=[
            pl.BlockSpec(memory_space=pltpu.SMEM),
            pl.BlockSpec(memory_space=pltpu.SMEM),
            pl.BlockSpec((1, tq, hpb * AUG_DIM), lambda b, h, i: (b, i, h)),
            pl.BlockSpec((1, tq, hpb * AUG_DIM), lambda b, h, i: (b, jnp.minimum(i + 1, n_q - 1), h)),
            pl.BlockSpec((1, lp, hpb * AUG_DIM), lambda b, h, i: (b, 0, h),
                         pipeline_mode=resident),
            pl.BlockSpec((1, hpb * HEAD_DIM, lp), lambda b, h, i: (b, h, 0),
                         pipeline_mode=resident),
            pl.BlockSpec((hpb, 1, HEAD_DIM), lambda b, h, i: (h, 0, 0)),
        ],
        out_specs=pl.BlockSpec((1, tq, hpb * HEAD_DIM), lambda b, h, i: (b, i, h)),
        out_shape=jax.ShapeDtypeStruct((bsz, lp, n_heads * HEAD_DIM), BF16),
        scratch_shapes=[
            pltpu.VMEM((hpb, HEAD_DIM + ONES_ROWS, tq), F32),
            pltpu.VMEM((hpb, tq, tq), F32),
            pltpu.VMEM((hpb, tq, tq), F32),
            pltpu.VMEM((hpb, 1, tq), F32),
        ],
        compiler_params=_cparams(3),
        name="flash",
    )(c_tiles, first_tile, q, q, k, vt, og_heads)


def _mix_out_kernel(hx_hbm, hm_hbm, hn_hbm, ya_hbm, wc_ref, cw_ref, og_ref, wo_ref, o_hbm,
                    u_ref, xbuf, hnbuf, yabuf, obuf, sem_in, sem_out,
                    *, tm, n_t, bsz, first_meta, n_groups, n_pad):
    b = pl.program_id(0)
    n_x = n_t - 1
    d_conv = n_groups * HEAD_DIM
    d_attn = yabuf.shape[2]

    def seq_row(i):
        return pl.multiple_of(jnp.where(i == 0, bsz * n_x + b, b * n_x + i - 1) * tm, tm)

    def in_copies(i, slot):
        x_row = pl.multiple_of((b * n_x + jnp.maximum(i - 1, 0)) * tm, tm)
        m_row = pl.multiple_of((first_meta + b) * tm, tm)
        return (pltpu.make_async_copy(hx_hbm.at[pl.ds(x_row, tm)], xbuf.at[slot], sem_in.at[0, slot]),
                pltpu.make_async_copy(hm_hbm.at[pl.ds(m_row, tm)], xbuf.at[slot], sem_in.at[0, slot]),
                pltpu.make_async_copy(hn_hbm.at[pl.ds(seq_row(i), tm)], hnbuf.at[slot],
                                      sem_in.at[1, slot]),
                pltpu.make_async_copy(ya_hbm.at[b, pl.ds(pl.multiple_of(i * tm, tm), tm)],
                                      yabuf.at[slot], sem_in.at[2, slot]))

    def for_in_copies(i, slot, act):
        from_x, from_meta, c_hn, c_ya = in_copies(i, slot)

        @pl.when(i == 0)
        def _():
            act(from_meta)

        @pl.when(i != 0)
        def _():
            act(from_x)

        act(c_hn)
        act(c_ya)

    def out_copy(i, slot):
        return pltpu.make_async_copy(obuf.at[slot], o_hbm.at[pl.ds(seq_row(i), tm)],
                                     sem_out.at[slot])

    u_ref[pl.ds(0, HALO), :] = jnp.zeros((HALO, d_conv), F32)
    for_in_copies(0, 0, lambda c: c.start())

    def tile(i, carry):
        slot = i % 2
        for_in_copies(i, slot, lambda c: c.wait())

        @pl.when(i + 1 < n_t)
        def _prefetch():
            for_in_copies(i + 1, 1 - slot, lambda c: c.start())

        @pl.when(i >= 2)
        def _slot_written_back():
            out_copy(i - 2, slot).wait()

        hn = hnbuf[slot]
        gate_b = jnp.dot(hn, wc_ref[:, :d_conv], preferred_element_type=F32)
        gate_c = jnp.dot(hn, wc_ref[:, d_conv:2 * d_conv], preferred_element_type=F32)
        hc = jnp.dot(hn, wc_ref[:, 2 * d_conv:], preferred_element_type=F32)
        u = gate_c * hc
        u_ref[pl.ds(HALO, tm), :] = u
        u1 = u_ref[pl.ds(HALO - 1, tm), :]
        u2 = u_ref[pl.ds(HALO - 2, tm), :]
        conv = cw_ref[0:1, :] * u2 + cw_ref[1:2, :] * u1 + cw_ref[2:3, :] * u
        y = gate_b * conv
        u_ref[pl.ds(0, HALO), :] = u_ref[pl.ds(tm, HALO), :]
        yc = jnp.concatenate(
            [(y[:, g * HEAD_DIM:(g + 1) * HEAD_DIM]
              * _rms_scale(y[:, g * HEAD_DIM:(g + 1) * HEAD_DIM])
              * og_ref[:, g * HEAD_DIM:(g + 1) * HEAD_DIM]).astype(BF16) for g in range(n_groups)],
            axis=1)
        out = (jnp.dot(yabuf[slot], wo_ref[:d_attn, :], preferred_element_type=F32)
               + jnp.dot(yc, wo_ref[d_attn:, :], preferred_element_type=F32))
        out = xbuf[slot] + out
        pos = i * tm + lax.broadcasted_iota(jnp.int32, (tm, 1), 0)
        obuf[slot] = jnp.where(pos >= n_pad, out, 0.0)
        out_copy(i, slot).start()
        return carry

    lax.fori_loop(0, n_t, tile, 0)
    out_copy(n_t - 2, (n_t - 2) % 2).wait()
    out_copy(n_t - 1, (n_t - 1) % 2).wait()


def _mix_out(hx, hm, hn, ya, w_in, w_out, layer, conv_w, og_conv, *, n_pad):
    d = hx.shape[1]
    bsz, lp, d_attn = ya.shape
    d_conv = d - d_attn
    assert d_conv == d_attn
    tm = SEQ_TILE
    n_t = lp // tm
    assert n_t >= 2
    kern = functools.partial(_mix_out_kernel, tm=tm, n_t=n_t, bsz=bsz,
                             first_meta=hm.shape[0] // tm - bsz,
                             n_groups=d_conv // HEAD_DIM, n_pad=n_pad)
    const = dict(pipeline_mode=pl.Buffered(1))
    in_hbm = pl.BlockSpec(memory_space=pl.ANY)
    return pl.pallas_call(
        kern,
        grid=(bsz,),
        in_specs=[
            in_hbm, in_hbm, in_hbm, in_hbm,
            pl.BlockSpec((None, d, 3 * d_conv), lambda b: (layer, 0, 1), **const),
            pl.BlockSpec((HALO, d_conv), lambda b: (0, 0)),
            pl.BlockSpec((1, d_conv), lambda b: (0, 0)),
            pl.BlockSpec((None, d_attn + d_conv, d), lambda b: (layer, 0, 0), **const),
        ],
        out_specs=pl.BlockSpec(memory_space=pl.ANY),
        out_shape=jax.ShapeDtypeStruct((bsz * lp, d), F32),
        scratch_shapes=[
            pltpu.VMEM((tm + HALO, d_conv), F32),
            pltpu.VMEM((2, tm, d), F32),
            pltpu.VMEM((2, tm, d), BF16),
            pltpu.VMEM((2, tm, d_attn), BF16),
            pltpu.VMEM((2, tm, d), F32),
            pltpu.SemaphoreType.DMA((3, 2)),
            pltpu.SemaphoreType.DMA((2,)),
        ],
        compiler_params=_cparams(1),
        name="mix_out",
    )(hx, hm, hn, ya, w_in, conv_w, og_conv, w_out)


def _ffn_kernel(h_ref, g_ref, wg_hbm, wu_hbm, wd_hbm, gf_ref, o_ref,
                hn_ref, wg_buf, wu_buf, wd_buf, sem, *, layer, tf, n_f, final):
    r = pl.program_id(0)
    n_r = pl.num_programs(0)

    def chunk_copies(f, slot):
        cols = pl.ds(pl.multiple_of(f * tf, tf), tf)
        return (pltpu.make_async_copy(wg_hbm.at[layer, :, cols], wg_buf.at[slot], sem.at[0, slot]),
                pltpu.make_async_copy(wu_hbm.at[layer, :, cols], wu_buf.at[slot], sem.at[1, slot]),
                pltpu.make_async_copy(wd_hbm.at[layer, cols, :], wd_buf.at[slot], sem.at[2, slot]))

    @pl.when(r == 0)
    def _first_chunk():
        for copy in chunk_copies(0, 0):
            copy.start()

    x = h_ref[...]
    hn_ref[...] = (x * _rms_scale(x) * g_ref[...]).astype(BF16)
    o_ref[...] = x

    def chunk(f, carry):
        slot = (r * n_f + f) % 2
        for copy in chunk_copies(f, slot):
            copy.wait()

        @pl.when((f + 1 < n_f) | (r + 1 < n_r))
        def _prefetch():
            for copy in chunk_copies(jnp.where(f + 1 < n_f, f + 1, 0), 1 - slot):
                copy.start()

        hn = hn_ref[...]
        gate = jnp.dot(hn, wg_buf[slot], preferred_element_type=F32)
        up = jnp.dot(hn, wu_buf[slot], preferred_element_type=F32)
        act = (jax.nn.silu(gate) * up).astype(BF16)
        o_ref[...] += jnp.dot(act, wd_buf[slot], preferred_element_type=F32)
        return carry

    lax.fori_loop(0, n_f, chunk, 0)

    if final:
        y = o_ref[...]
        o_ref[...] = y * _rms_scale(y) * gf_ref[...]


def _ffn(h, gain, w_gate, w_up, w_down, layer, final_gain, *, rows, final):
    d = h.shape[1]
    d_ff = w_gate.shape[2]
    tm = 1024 if rows % 1024 == 0 else 512
    tf = 512 if d_ff % 512 == 0 else 256
    in_hbm = pl.BlockSpec(memory_space=pl.ANY)
    return pl.pallas_call(
        functools.partial(_ffn_kernel, layer=layer, tf=tf, n_f=d_ff // tf, final=final),
        grid=(rows // tm,),
        in_specs=[
            pl.BlockSpec((tm, d), lambda r: (r, 0)),
            pl.BlockSpec((1, d), lambda r: (0, 0)),
            in_hbm, in_hbm, in_hbm,
            pl.BlockSpec((1, d), lambda r: (0, 0)),
        ],
        out_specs=pl.BlockSpec((tm, d), lambda r: (r, 0)),
        out_shape=jax.ShapeDtypeStruct((rows, d), F32),
        scratch_shapes=[
            pltpu.VMEM((tm, d), BF16),
            pltpu.VMEM((2, d, tf), BF16),
            pltpu.VMEM((2, d, tf), BF16),
            pltpu.VMEM((2, tf, d), BF16),
            pltpu.SemaphoreType.DMA((3, 2)),
        ],
        compiler_params=_cparams(1),
        name="ffn",
    )(h, gain, w_gate, w_up, w_down, final_gain)


def kernel(x, meta, norm_mix, w_in, b_f, conv_w, out_gain, w_out, norm_ffn, w_gate, w_up, w_down, final_norm):
    bsz, seq, d = x.shape
    n_meta = meta.shape[0]
    depth = w_in.shape[0]
    d_attn = d // 2
    n_heads = d_attn // HEAD_DIM
    assert n_heads <= HEAD_PAD and seq % SEQ_TILE == 0 and n_meta <= SEQ_TILE
    assert w_in.shape[2] == 6 * d_attn + n_heads
    lp = seq + SEQ_TILE
    n_pad = lp - seq - n_meta
    scale = HEAD_DIM ** -0.5

    m = jnp.broadcast_to(meta.astype(x.dtype)[None], (bsz, n_meta, d))
    head = jnp.concatenate([jnp.zeros((bsz, n_pad, d), x.dtype), m], axis=1)
    hx, hm = x.reshape(bsz * seq, d), head.reshape(bsz * SEQ_TILE, d)
    n_t = lp // SEQ_TILE
    final_gain = final_norm.reshape(1, d)

    w_in_bf, w_out_bf = w_in.astype(BF16), w_out.astype(BF16)
    w_gate_bf, w_up_bf, w_down_bf = w_gate.astype(BF16), w_up.astype(BF16), w_down.astype(BF16)
    for l in range(depth):
        wf_t = jnp.zeros((HEAD_PAD, d), BF16).at[:n_heads].set(w_in[l, :, 6 * d_attn:].T.astype(BF16))
        bf_col = jnp.zeros((HEAD_PAD, 1), F32).at[:n_heads, 0].set(b_f[l])
        cw = jnp.zeros((HALO, d_attn), F32).at[:CONV_WIDTH].set(conv_w[l])
        og_heads = out_gain[l, :d_attn].reshape(n_heads, 1, HEAD_DIM)
        og_conv = out_gain[l, d_attn:].reshape(1, d_attn)

        gain = norm_mix[l].reshape(1, d)
        q, k, vt, stats, hn = _attn_proj(hx, hm, gain, w_in_bf, l, wf_t, bf_col, bsz=bsz, n_t=n_t,
                                         n_pad=n_pad, scale=scale)
        ya = _flash(q, k, vt, stats[:, :, STAT_C, :, 0].reshape(-1),
                    _first_key_tile(stats, n_heads), og_heads)

        h = _mix_out(hx, hm, hn, ya, w_in_bf, w_out_bf, l, cw, og_conv, n_pad=n_pad)
        final = l == depth - 1
        h = _ffn(h, norm_ffn[l].reshape(1, d), w_gate_bf, w_up_bf, w_down_bf, l, final_gain,
                 rows=bsz * seq if final else h.shape[0], final=final)
        hx = hm = h

    return h.reshape(bsz, seq, d)
```

```python
import functools
import math

import jax
import jax.numpy as jnp
from jax import lax
from jax.experimental import pallas as pl
from jax.experimental.pallas import tpu as pltpu

F32 = jnp.float32
BF16 = jnp.bfloat16

HEAD_DIM = 128
AUG_DIM = 2 * HEAD_DIM
N_SPLIT = 3
CONV_WIDTH = 3
EPS = 1e-6
NEG = -1e30
FAKE = 2 * NEG
LOG2E = math.log2(math.e)
SEQ_TILE = 512
HEAD_PAD = 16
ONES_ROWS = 16
LANES = 128
STAT_C, STAT_BMAX, STAT_QN, STAT_KN, N_STAT = 0, 1, 2, 3, 4
SKIP_LOG2 = 140.0
BOUND_REL_SLACK = 1.001
BOUND_ABS_SLACK = 1.0
HALO = 8
CONV_CHUNK = 2 * HEAD_DIM
VMEM_LIMIT = 60 * 1024 * 1024

_NT = (((1,), (1,)), ((), ()))


def _cparams(n_axes):
    return pltpu.CompilerParams(dimension_semantics=("arbitrary",) * n_axes,
                                vmem_limit_bytes=VMEM_LIMIT)


def _rms_scale(x, axis=-1):
    return lax.rsqrt(jnp.mean(x * x, axis=axis, keepdims=True) + EPS)


def _cumsum_lanes(x):
    n = x.shape[1]
    lane = lax.broadcasted_iota(jnp.int32, x.shape, 1)
    shift = 1
    while shift < n:
        x = x + jnp.where(lane >= shift, pltpu.roll(x, shift, axis=1), 0.0)
        shift *= 2
    return x


def _round_bf16(x):
    return x.astype(BF16).astype(F32)


def _max_row_norm(x):
    xf = x.astype(F32)
    n2 = jnp.max(jnp.sum(xf * xf, axis=1, keepdims=True), axis=0, keepdims=True)
    return jnp.broadcast_to(jnp.sqrt(n2), (1, LANES))


def _attn_proj_kernel(hx_ref, hm_ref, g_ref, wqk_ref, wv_ref, wf_ref, bf_ref,
                      q_ref, k_ref, vt_ref, stat_ref, hn_ref, carry_ref, wvt_ref,
                      *, tm, n_heads, n_pad, scale):
    i = pl.program_id(1)
    d_attn = n_heads * HEAD_DIM

    @pl.when(i == 0)
    def _first_tile():
        carry_ref[...] = jnp.zeros_like(carry_ref)
        x = hm_ref[...]
        hn_ref[...] = (x * _rms_scale(x) * g_ref[...]).astype(BF16)

    @pl.when(i != 0)
    def _x_tile():
        x = hx_ref[...]
        hn_ref[...] = (x * _rms_scale(x) * g_ref[...]).astype(BF16)

    hn = hn_ref[...]

    @pl.when((pl.program_id(0) == 0) & (i == 0))
    def _transpose_v_weights():
        wvt_ref[pl.ds(0, d_attn), :] = wv_ref[...].T
        wvt_ref[pl.ds(d_attn, HEAD_PAD), :] = wf_ref[...]

    zvt = lax.dot_general(wvt_ref[...], hn, _NT, preferred_element_type=F32)
    vt_ref[0] = zvt[:d_attn].astype(BF16)

    logit = zvt[d_attn:]
    log_f = jax.nn.log_sigmoid(logit + bf_ref[...])
    r = _cumsum_lanes(log_f)
    stat_ref[0, 0] = jnp.zeros(stat_ref.shape[2:], F32)
    stat_ref[0, 0, STAT_C] = carry_ref[...] * LOG2E
    carry_ref[...] = carry_ref[...] + jnp.broadcast_to(r[:, tm - 1:tm], carry_ref.shape)
    pos = i * tm + lax.broadcasted_iota(jnp.int32, r.shape, 1)
    bias = jnp.where(pos >= n_pad, -LOG2E * r, NEG)
    stat_ref[0, 0, STAT_BMAX] = jnp.broadcast_to(jnp.max(bias, axis=1, keepdims=True),
                                                 (HEAD_PAD, LANES))
    bias_col = jnp.concatenate([bias, jnp.zeros((LANES - HEAD_PAD, tm), F32)], axis=0).T

    lane = lax.broadcasted_iota(jnp.int32, (tm, HEAD_DIM), 1)
    ones = jnp.where(lane < N_SPLIT, 1.0, 0.0).astype(BF16)
    zk = jnp.dot(hn, wqk_ref[:, d_attn:], preferred_element_type=F32)
    for h in range(n_heads):
        kh = zk[:, h * HEAD_DIM:(h + 1) * HEAD_DIM].astype(BF16)
        k_ref[0, :, h * AUG_DIM:h * AUG_DIM + HEAD_DIM] = kh
        stat_ref[0, 0, STAT_KN, h:h + 1, :] = _max_row_norm(kh)
        xb = jnp.broadcast_to(bias_col[:, h:h + 1], (tm, HEAD_DIM))
        hi = _round_bf16(xb)
        mid = _round_bf16(xb - hi)
        lo = xb - hi - mid
        aug = jnp.where(lane == 0, hi, jnp.where(lane == 1, mid, jnp.where(lane == 2, lo, 0.0)))
        k_ref[0, :, h * AUG_DIM + HEAD_DIM:(h + 1) * AUG_DIM] = aug.astype(BF16)

    zq = jnp.dot(hn, wqk_ref[:, :d_attn], preferred_element_type=F32)
    for h in range(n_heads):
        qh = (zq[:, h * HEAD_DIM:(h + 1) * HEAD_DIM] * (scale * LOG2E)).astype(BF16)
        q_ref[0, :, h * AUG_DIM:h * AUG_DIM + HEAD_DIM] = qh
        q_ref[0, :, h * AUG_DIM + HEAD_DIM:(h + 1) * AUG_DIM] = ones
        stat_ref[0, 0, STAT_QN, h:h + 1, :] = _max_row_norm(qh)


def _row_tile_of(bsz, n_t):
    n_x = n_t - 1
    return lambda b, i: (jnp.where(i == 0, bsz * n_x + b, b * n_x + i - 1), 0)


def _residual_specs(hx, hm, bsz, n_t, tm):
    n_x = n_t - 1
    first_meta = hm.shape[0] // tm - bsz
    d = hx.shape[1]
    return (pl.BlockSpec((tm, d), lambda b, i: (b * n_x + jnp.maximum(i - 1, 0), 0)),
            pl.BlockSpec((tm, d), lambda b, i: (first_meta + b, 0), pipeline_mode=pl.Buffered(1)))


def _attn_proj(hx, hm, gain, w_in, layer, wf_t, bf_col, *, bsz, n_t, n_pad, scale):
    d = hx.shape[1]
    d_attn = d // 2
    n_heads = d_attn // HEAD_DIM
    tm = SEQ_TILE
    lp = n_t * tm
    kern = functools.partial(_attn_proj_kernel, tm=tm, n_heads=n_heads, n_pad=n_pad, scale=scale)
    aug_blk = pl.BlockSpec((1, tm, n_heads * AUG_DIM), lambda b, i: (b, i, 0))
    const = dict(pipeline_mode=pl.Buffered(1))
    return pl.pallas_call(
        kern,
        grid=(bsz, n_t),
        in_specs=[
            *_residual_specs(hx, hm, bsz, n_t, tm),
            pl.BlockSpec((1, d), lambda b, i: (0, 0)),
            pl.BlockSpec((None, d, 2 * d_attn), lambda b, i: (layer, 0, 0), **const),
            pl.BlockSpec((None, d, d_attn), lambda b, i: (layer, 0, 2), **const),
            pl.BlockSpec((HEAD_PAD, d), lambda b, i: (0, 0)),
            pl.BlockSpec((HEAD_PAD, 1), lambda b, i: (0, 0)),
        ],
        out_specs=[
            aug_blk, aug_blk,
            pl.BlockSpec((1, d_attn, tm), lambda b, i: (b, 0, i)),
            pl.BlockSpec((1, 1, N_STAT, HEAD_PAD, LANES), lambda b, i: (b, i, 0, 0, 0)),
            pl.BlockSpec((tm, d), _row_tile_of(bsz, n_t)),
        ],
        out_shape=[
            jax.ShapeDtypeStruct((bsz, lp, n_heads * AUG_DIM), BF16),
            jax.ShapeDtypeStruct((bsz, lp, n_heads * AUG_DIM), BF16),
            jax.ShapeDtypeStruct((bsz, d_attn, lp), BF16),
            jax.ShapeDtypeStruct((bsz, n_t, N_STAT, HEAD_PAD, LANES), F32),
            jax.ShapeDtypeStruct((bsz * lp, d), BF16),
        ],
        scratch_shapes=[pltpu.VMEM((HEAD_PAD, LANES), F32),
                        pltpu.VMEM((d_attn + HEAD_PAD, d), BF16)],
        compiler_params=_cparams(2),
        name="attn_proj",
    )(hx, hm, gain, w_in, w_in, wf_t, bf_col)


def _flash_kernel(c_ref, first_ref, q_ref, qn_ref, k_ref, vt_ref, g_ref, o_ref,
                  acc_ref, sa_ref, sb_ref, maxa_ref, *, tq, n_q, hpb):
    b = pl.program_id(0)
    hp = pl.program_id(1)
    i = pl.program_id(2)
    heads = range(hpb)
    c_base = [(b * n_q) * HEAD_PAD + hp * hpb + hh for hh in heads]
    c_q = [c_ref[c_base[hh] + i * HEAD_PAD] for hh in heads]
    first_base = (b * pl.num_programs(1) + hp) * n_q
    first = first_ref[first_base + i]
    n_unmasked = i - first
    last = first + jnp.maximum(n_unmasked - 1, 0)
    key = lax.broadcasted_iota(jnp.int32, (tq, tq), 0)
    qry = lax.broadcasted_iota(jnp.int32, (tq, tq), 1)
    ones_rows = jnp.ones((ONES_ROWS, tq), BF16)

    def scores_into(s_ref, j, masked=False, queries=q_ref):
        start = pl.multiple_of(j * tq, tq)
        col_max = []
        for hh in heads:
            cols = slice(hh * AUG_DIM, (hh + 1) * AUG_DIM)
            st = lax.dot_general(k_ref[0, pl.ds(start, tq), cols], queries[0, :, cols], _NT,
                                 preferred_element_type=F32)
            if masked:
                st = jnp.where(key <= qry, st, NEG)
            s_ref[hh] = st
            col_max.append(jnp.max(st, axis=0, keepdims=True))
        return tuple(col_max)

    def accumulate(s_ref, col_max, j, delta, m_all):
        start = pl.multiple_of(j * tq, tq)
        m_out = []
        for hh in heads:
            vt = vt_ref[0, hh * HEAD_DIM:(hh + 1) * HEAD_DIM, pl.ds(start, tq)]
            m_new = jnp.maximum(m_all[hh], col_max[hh] + delta[hh])
            alpha = jnp.exp2(m_all[hh] - m_new)
            p = jnp.exp2(s_ref[hh] - (m_new - delta[hh])).astype(BF16)
            pv = jnp.dot(jnp.concatenate([vt, ones_rows], axis=0), p,
                         preferred_element_type=F32)
            acc_ref[hh] = alpha * acc_ref[hh] + pv
            m_out.append(m_new)
        return tuple(m_out)

    def delta_of(j):
        return [c_q[hh] - c_ref[c_base[hh] + j * HEAD_PAD] for hh in heads]

    def tail_delta_of(j):
        real = delta_of(jnp.minimum(j, last))
        return [jnp.where(j < i, real[hh], FAKE) for hh in heads]

    acc_ref[...] = jnp.zeros_like(acc_ref)

    @pl.when(i == 0)
    def _first_query_tile():
        for hh, col_max in enumerate(scores_into(sa_ref, first)):
            maxa_ref[hh] = col_max

    max_a = tuple(maxa_ref[hh] for hh in heads)
    m_all = tuple(jnp.full((1, tq), NEG, F32) for _ in heads)
    n_trips = jnp.maximum((n_unmasked - 1) // 2, 0)

    def trip(jj, carry):
        m_all, max_a = carry
        j0 = first + 2 * jj
        max_b = scores_into(sb_ref, j0 + 1)
        m_all = accumulate(sa_ref, max_a, j0, delta_of(j0), m_all)
        max_a = scores_into(sa_ref, j0 + 2)
        m_all = accumulate(sb_ref, max_b, j0 + 1, delta_of(j0 + 1), m_all)
        return m_all, max_a

    def two_trips(jj, carry):
        return trip(2 * jj + 1, trip(2 * jj, carry))

    def four_trips(jj, carry):
        return two_trips(2 * jj + 1, two_trips(2 * jj, carry))

    carry = lax.fori_loop(0, n_trips // 4, four_trips, (m_all, max_a))
    carry = lax.fori_loop(2 * (n_trips // 4), n_trips // 2, two_trips, carry)
    m_all, max_a = lax.fori_loop(2 * (n_trips // 2), n_trips, trip, carry)
    j0 = first + 2 * n_trips
    no_delta = [0.0 for _ in heads]

    def finish():
        first_next = first_ref[first_base + jnp.minimum(i + 1, n_q - 1)]
        for hh, col_max in enumerate(scores_into(sa_ref, first_next, queries=qn_ref)):
            maxa_ref[hh] = col_max
        for hh in heads:
            ot = acc_ref[hh, :HEAD_DIM, :] / acc_ref[hh, HEAD_DIM:HEAD_DIM + 1, :]
            ot = ot * _rms_scale(ot, axis=0)
            o_ref[0, :, hh * HEAD_DIM:(hh + 1) * HEAD_DIM] = (ot.T * g_ref[hh]).astype(BF16)

    @pl.when(n_unmasked - 2 * n_trips == 2)
    def _tail_two_unmasked():
        max_b = scores_into(sb_ref, j0 + 1)
        m_1 = accumulate(sa_ref, max_a, j0, delta_of(j0), m_all)
        max_d = scores_into(sa_ref, i, masked=True)
        m_2 = accumulate(sb_ref, max_b, j0 + 1, delta_of(j0 + 1), m_1)
        accumulate(sa_ref, max_d, i, no_delta, m_2)
        finish()

    @pl.when(n_unmasked - 2 * n_trips < 2)
    def _tail_one_unmasked():
        max_d = scores_into(sb_ref, i, masked=True)
        m_1 = accumulate(sa_ref, max_a, jnp.minimum(j0, last), tail_delta_of(j0), m_all)
        accumulate(sb_ref, max_d, i, no_delta, m_1)
        finish()


def _heads_per_block(n_heads):
    return 2 if n_heads % 2 == 0 else 1


def _first_key_tile(stats, n_heads):
    hpb = _heads_per_block(n_heads)
    st = stats[:, :, :, :n_heads, 0]
    c, bmax, qn, kn = (jnp.moveaxis(st[:, :, s], 1, 2) for s in
                       (STAT_C, STAT_BMAX, STAT_QN, STAT_KN))
    upper = (qn[..., :, None] * (kn[..., None, :] + kn[..., :, None]) * BOUND_REL_SLACK
             + BOUND_ABS_SLACK
             + (c[..., :, None] - c[..., None, :]) + bmax[..., None, :])
    n_t = c.shape[-1]
    earlier = jnp.arange(n_t)[None, :] < jnp.arange(n_t)[:, None]
    negligible = (upper < -SKIP_LOG2) & earlier
    first = jnp.min(jnp.where(negligible, n_t, jnp.arange(n_t)), axis=-1)
    first = first.reshape(first.shape[0], n_heads // hpb, hpb, n_t).min(axis=2)
    return first.astype(jnp.int32).reshape(-1)


def _flash(q, k, vt, c_tiles, first_tile, og_heads):
    bsz, lp, _ = q.shape
    n_heads = vt.shape[1] // HEAD_DIM
    hpb = _heads_per_block(n_heads)
    tq = SEQ_TILE
    n_q = lp // tq
    kern = functools.partial(_flash_kernel, tq=tq, n_q=n_q, hpb=hpb)
    resident = pl.Buffered(2)
    return pl.pallas_call(
        kern,
        grid=(bsz, n_heads // hpb, n_q),
        in_specs=[
            pl.BlockSpec(memory_space=pltpu.SMEM),
            pl.BlockSpec(memory_space=pltpu.SMEM),
            pl.BlockSpec((1, tq, hpb * AUG_DIM), lambda b, h, i: (b, i, h)),
            pl.BlockSpec((1, tq, hpb * AUG_DIM), lambda b, h, i: (b, jnp.minimum(i + 1, n_q - 1), h)),
            pl.BlockSpec((1, lp, hpb * AUG_DIM), lambda b, h, i: (b, 0, h),
                         pipeline_mode=resident),
            pl.BlockSpec((1, hpb * HEAD_DIM, lp), lambda b, h, i: (b, h, 0),
                         pipeline_mode=resident),
            pl.BlockSpec((hpb, 1, HEAD_DIM), lambda b, h, i: (h, 0, 0)),
        ],
        out_specs=pl.BlockSpec((1, tq, hpb * HEAD_DIM), lambda b, h, i: (b, i, h)),
        out_shape=jax.ShapeDtypeStruct((bsz, lp, n_heads * HEAD_DIM), BF16),
        scratch_shapes=[
            pltpu.VMEM((hpb, HEAD_DIM + ONES_ROWS, tq), F32),
            pltpu.VMEM((hpb, tq, tq), F32),
            pltpu.VMEM((hpb, tq, tq), F32),
            pltpu.VMEM((hpb, 1, tq), F32),
        ],
        compiler_params=_cparams(3),
        name="flash",
    )(c_tiles, first_tile, q, q, k, vt, og_heads)


def _mix_out_kernel(hx_hbm, hm_hbm, hn_hbm, ya_hbm, wc_ref, cw_ref, og_ref, wo_ref, o_hbm,
                    u_ref, xbuf, hnbuf, yabuf, obuf, sem_in, sem_out,
                    *, tm, n_t, bsz, first_meta, n_groups, n_pad):
    b = pl.program_id(0)
    n_x = n_t - 1
    d_conv = n_groups * HEAD_DIM
    d_attn = yabuf.shape[2]

    def seq_row(i):
        return pl.multiple_of(jnp.where(i == 0, bsz * n_x + b, b * n_x + i - 1) * tm, tm)

    def in_copies(i, slot):
        x_row = pl.multiple_of((b * n_x + jnp.maximum(i - 1, 0)) * tm, tm)
        m_row = pl.multiple_of((first_meta + b) * tm, tm)
        return (pltpu.make_async_copy(hx_hbm.at[pl.ds(x_row, tm)], xbuf.at[slot], sem_in.at[0, slot]),
                pltpu.make_async_copy(hm_hbm.at[pl.ds(m_row, tm)], xbuf.at[slot], sem_in.at[0, slot]),
                pltpu.make_async_copy(hn_hbm.at[pl.ds(seq_row(i), tm)], hnbuf.at[slot],
                                      sem_in.at[1, slot]),
                pltpu.make_async_copy(ya_hbm.at[b, pl.ds(pl.multiple_of(i * tm, tm), tm)],
                                      yabuf.at[slot], sem_in.at[2, slot]))

    def for_in_copies(i, slot, act):
        from_x, from_meta, c_hn, c_ya = in_copies(i, slot)

        @pl.when(i == 0)
        def _():
            act(from_meta)

        @pl.when(i != 0)
        def _():
            act(from_x)

        act(c_hn)
        act(c_ya)

    def out_copy(i, slot):
        return pltpu.make_async_copy(obuf.at[slot], o_hbm.at[pl.ds(seq_row(i), tm)],
                                     sem_out.at[slot])

    u_ref[pl.ds(0, HALO), :] = jnp.zeros((HALO, d_conv), F32)
    for_in_copies(0, 0, lambda c: c.start())

    def tile(i, carry):
        slot = i % 2
        for_in_copies(i, slot, lambda c: c.wait())

        @pl.when(i + 1 < n_t)
        def _prefetch():
            for_in_copies(i + 1, 1 - slot, lambda c: c.start())

        @pl.when(i >= 2)
        def _slot_written_back():
            out_copy(i - 2, slot).wait()

        hn = hnbuf[slot]
        yc_chunks = []
        for c0 in range(0, d_conv, CONV_CHUNK):
            cols = slice(c0, c0 + CONV_CHUNK)
            gate_b = jnp.dot(hn, wc_ref[:, c0:c0 + CONV_CHUNK], preferred_element_type=F32)
            gate_c = jnp.dot(hn, wc_ref[:, d_conv + c0:d_conv + c0 + CONV_CHUNK],
                             preferred_element_type=F32)
            hc = jnp.dot(hn, wc_ref[:, 2 * d_conv + c0:2 * d_conv + c0 + CONV_CHUNK],
                         preferred_element_type=F32)
            u = gate_c * hc
            u_ref[pl.ds(HALO, tm), cols] = u
            u1 = u_ref[pl.ds(HALO - 1, tm), cols]
            u2 = u_ref[pl.ds(HALO - 2, tm), cols]
            conv = cw_ref[0:1, cols] * u2 + cw_ref[1:2, cols] * u1 + cw_ref[2:3, cols] * u
            y = gate_b * conv
            u_ref[pl.ds(0, HALO), cols] = u_ref[pl.ds(tm, HALO), cols]
            for g0 in range(0, CONV_CHUNK, HEAD_DIM):
                yg = y[:, g0:g0 + HEAD_DIM]
                yc_chunks.append((yg * _rms_scale(yg)
                                  * og_ref[:, c0 + g0:c0 + g0 + HEAD_DIM]).astype(BF16))
        yc = jnp.concatenate(yc_chunks, axis=1)
        out = (jnp.dot(yabuf[slot], wo_ref[:d_attn, :], preferred_element_type=F32)
               + jnp.dot(yc, wo_ref[d_attn:, :], preferred_element_type=F32))
        out = xbuf[slot] + out
        pos = i * tm + lax.broadcasted_iota(jnp.int32, (tm, 1), 0)
        obuf[slot] = jnp.where(pos >= n_pad, out, 0.0)
        out_copy(i, slot).start()
        return carry

    lax.fori_loop(0, n_t, tile, 0)
    out_copy(n_t - 2, (n_t - 2) % 2).wait()
    out_copy(n_t - 1, (n_t - 1) % 2).wait()


def _mix_out(hx, hm, hn, ya, w_in, w_out, layer, conv_w, og_conv, *, n_pad):
    d = hx.shape[1]
    bsz, lp, d_attn = ya.shape
    d_conv = d - d_attn
    assert d_conv == d_attn
    tm = SEQ_TILE
    n_t = lp // tm
    assert n_t >= 2
    kern = functools.partial(_mix_out_kernel, tm=tm, n_t=n_t, bsz=bsz,
                             first_meta=hm.shape[0] // tm - bsz,
                             n_groups=d_conv // HEAD_DIM, n_pad=n_pad)
    const = dict(pipeline_mode=pl.Buffered(1))
    in_hbm = pl.BlockSpec(memory_space=pl.ANY)
    return pl.pallas_call(
        kern,
        grid=(bsz,),
        in_specs=[
            in_hbm, in_hbm, in_hbm, in_hbm,
            pl.BlockSpec((None, d, 3 * d_conv), lambda b: (layer, 0, 1), **const),
            pl.BlockSpec((HALO, d_conv), lambda b: (0, 0)),
            pl.BlockSpec((1, d_conv), lambda b: (0, 0)),
            pl.BlockSpec((None, d_attn + d_conv, d), lambda b: (layer, 0, 0), **const),
        ],
        out_specs=pl.BlockSpec(memory_space=pl.ANY),
        out_shape=jax.ShapeDtypeStruct((bsz * lp, d), F32),
        scratch_shapes=[
            pltpu.VMEM((tm + HALO, d_conv), F32),
            pltpu.VMEM((2, tm, d), F32),
            pltpu.VMEM((2, tm, d), BF16),
            pltpu.VMEM((2, tm, d_attn), BF16),
            pltpu.VMEM((2, tm, d), F32),
            pltpu.SemaphoreType.DMA((3, 2)),
            pltpu.SemaphoreType.DMA((2,)),
        ],
        compiler_params=_cparams(1),
        name="mix_out",
    )(hx, hm, hn, ya, w_in, conv_w, og_conv, w_out)


def _ffn_kernel(h_ref, g_ref, wg_hbm, wu_hbm, wd_hbm, gf_ref, o_ref,
                hn_ref, wg_buf, wu_buf, wd_buf, sem, *, layer, tf, n_f, final):
    r = pl.program_id(0)
    n_r = pl.num_programs(0)

    def chunk_copies(f, slot):
        cols = pl.ds(pl.multiple_of(f * tf, tf), tf)
        return (pltpu.make_async_copy(wg_hbm.at[layer, :, cols], wg_buf.at[slot], sem.at[0, slot]),
                pltpu.make_async_copy(wu_hbm.at[layer, :, cols], wu_buf.at[slot], sem.at[1, slot]),
                pltpu.make_async_copy(wd_hbm.at[layer, cols, :], wd_buf.at[slot], sem.at[2, slot]))

    @pl.when(r == 0)
    def _first_chunk():
        for copy in chunk_copies(0, 0):
            copy.start()

    x = h_ref[...]
    hn_ref[...] = (x * _rms_scale(x) * g_ref[...]).astype(BF16)
    o_ref[...] = x

    def chunk(f, carry):
        slot = (r * n_f + f) % 2
        for copy in chunk_copies(f, slot):
            copy.wait()

        @pl.when((f + 1 < n_f) | (r + 1 < n_r))
        def _prefetch():
            for copy in chunk_copies(jnp.where(f + 1 < n_f, f + 1, 0), 1 - slot):
                copy.start()

        hn = hn_ref[...]
        gate = jnp.dot(hn, wg_buf[slot], preferred_element_type=F32)
        up = jnp.dot(hn, wu_buf[slot], preferred_element_type=F32)
        act = (jax.nn.silu(gate) * up).astype(BF16)
        o_ref[...] += jnp.dot(act, wd_buf[slot], preferred_element_type=F32)
        return carry

    lax.fori_loop(0, n_f, chunk, 0)

    if final:
        y = o_ref[...]
        o_ref[...] = y * _rms_scale(y) * gf_ref[...]


def _ffn(h, gain, w_gate, w_up, w_down, layer, final_gain, *, rows, final):
    d = h.shape[1]
    d_ff = w_gate.shape[2]
    tm = 1024 if rows % 1024 == 0 else 512
    tf = 512 if d_ff % 512 == 0 else 256
    in_hbm = pl.BlockSpec(memory_space=pl.ANY)
    return pl.pallas_call(
        functools.partial(_ffn_kernel, layer=layer, tf=tf, n_f=d_ff // tf, final=final),
        grid=(rows // tm,),
        in_specs=[
            pl.BlockSpec((tm, d), lambda r: (r, 0)),
            pl.BlockSpec((1, d), lambda r: (0, 0)),
            in_hbm, in_hbm, in_hbm,
            pl.BlockSpec((1, d), lambda r: (0, 0)),
        ],
        out_specs=pl.BlockSpec((tm, d), lambda r: (r, 0)),
        out_shape=jax.ShapeDtypeStruct((rows, d), F32),
        scratch_shapes=[
            pltpu.VMEM((tm, d), BF16),
            pltpu.VMEM((2, d, tf), BF16),
            pltpu.VMEM((2, d, tf), BF16),
            pltpu.VMEM((2, tf, d), BF16),
            pltpu.SemaphoreType.DMA((3, 2)),
        ],
        compiler_params=_cparams(1),
        name="ffn",
    )(h, gain, w_gate, w_up, w_down, final_gain)


def kernel(x, meta, norm_mix, w_in, b_f, conv_w, out_gain, w_out, norm_ffn, w_gate, w_up, w_down, final_norm):
    bsz, seq, d = x.shape
    n_meta = meta.shape[0]
    depth = w_in.shape[0]
    d_attn = d // 2
    n_heads = d_attn // HEAD_DIM
    assert n_heads <= HEAD_PAD and seq % SEQ_TILE == 0 and n_meta <= SEQ_TILE
    assert w_in.shape[2] == 6 * d_attn + n_heads
    lp = seq + SEQ_TILE
    n_pad = lp - seq - n_meta
    scale = HEAD_DIM ** -0.5

    m = jnp.broadcast_to(meta.astype(x.dtype)[None], (bsz, n_meta, d))
    head = jnp.concatenate([jnp.zeros((bsz, n_pad, d), x.dtype), m], axis=1)
    hx, hm = x.reshape(bsz * seq, d), head.reshape(bsz * SEQ_TILE, d)
    n_t = lp // SEQ_TILE
    final_gain = final_norm.reshape(1, d)

    w_in_bf, w_out_bf = w_in.astype(BF16), w_out.astype(BF16)
    w_gate_bf, w_up_bf, w_down_bf = w_gate.astype(BF16), w_up.astype(BF16), w_down.astype(BF16)
    for l in range(depth):
        wf_t = jnp.zeros((HEAD_PAD, d), BF16).at[:n_heads].set(w_in[l, :, 6 * d_attn:].T.astype(BF16))
        bf_col = jnp.zeros((HEAD_PAD, 1), F32).at[:n_heads, 0].set(b_f[l])
        cw = jnp.zeros((HALO, d_attn), F32).at[:CONV_WIDTH].set(conv_w[l])
        og_heads = out_gain[l, :d_attn].reshape(n_heads, 1, HEAD_DIM)
        og_conv = out_gain[l, d_attn:].reshape(1, d_attn)

        gain = norm_mix[l].reshape(1, d)
        q, k, vt, stats, hn = _attn_proj(hx, hm, gain, w_in_bf, l, wf_t, bf_col, bsz=bsz, n_t=n_t,
                                         n_pad=n_pad, scale=scale)
        ya = _flash(q, k, vt, stats[:, :, STAT_C, :, 0].reshape(-1),
                    _first_key_tile(stats, n_heads), og_heads)

        h = _mix_out(hx, hm, hn, ya, w_in_bf, w_out_bf, l, cw, og_conv, n_pad=n_pad)
        final = l == depth - 1
        h = _ffn(h, norm_ffn[l].reshape(1, d), w_gate_bf, w_up_bf, w_down_bf, l, final_gain,
                 rows=bsz * seq if final else h.shape[0], final=final)
        hx = hm = h

    return h.reshape(bsz, seq, d)
```
